```python
import math
import jax
import jax.numpy as jnp
from jax import lax
import numpy as np

D_MODEL = 1024
BATCH = 16
SEQ = 2048
DEPTH = 2

CTX_LEN = 256
GRID_W = 64
EPS = 1e-6

HY_WIDTH = 256
HY_HEADS = 4
HY_SHORT = 3
HY_EMB = 33
HY_BANDS = (HY_EMB - 1) // 2
HY_HIDDEN = 64
HY_DECAY_TARGET = 1e-2
HY_SHORT_DECAY_PCT = 0.3
HY_LONG_DECAY_PCT = 1.5

SSD_HEADS = 8
SSD_HEAD_DIM = 64
SSD_INNER = SSD_HEADS * SSD_HEAD_DIM
SSD_GROUPS = 2
SSD_STATE = 128
SSD_CONV = 3
SSD_CHUNK = 128
SSD_XBC = SSD_INNER + 2 * SSD_GROUPS * SSD_STATE

FN_WIDTH = 256
FN_GROUPS = 4
FN_GROUP_DIM = FN_WIDTH // FN_GROUPS

D_MIX = HY_WIDTH + SSD_INNER + FN_WIDTH

OFF_HY = 0
OFF_Z = OFF_HY + 3 * HY_WIDTH
OFF_XBC = OFF_Z + SSD_INNER
OFF_DT = OFF_XBC + SSD_XBC
OFF_FN = OFF_DT + 2 * SSD_HEADS
D_IN_PROJ = OFF_FN + FN_WIDTH

PEER_HEADS = 8
PEER_TOPK = 16
N_KEYS = 128
N_EXPERTS = N_KEYS * N_KEYS
PEER_QDIM = 256
PEER_HALF = PEER_QDIM // 2
PEER_BLOCK = 128

kernel_name = 'hybrid_hyena_ssd_fnet_peer_dit'


def rmsnorm(x, g):
    xf = x.astype(jnp.float32)
    y = xf * lax.rsqrt(jnp.mean(xf * xf, axis=-1, keepdims=True) + EPS)
    return y.astype(x.dtype) * g


def modulate(h, shift, scale):
    return h * (1 + scale) + shift


def dwconv(u, w, b):
    k_w, ch = w.shape
    y = lax.conv_general_dilated(u, w.astype(u.dtype)[:, None, :], window_strides=(1,),
                                 padding=[(k_w // 2, k_w // 2)],
                                 dimension_numbers=('NWC', 'WIO', 'NWC'), feature_group_count=ch)
    return y + b


def hyena_filter(seq_len, w1, b1, w2, b2, w3, freq):
    f32 = jnp.float32
    t = jnp.linspace(0.0, 1.0, seq_len, dtype=f32)[:, None]
    w = 2.0 * math.pi * jnp.arange(seq_len, dtype=f32)[:, None] / seq_len
    f = jnp.linspace(1e-4, HY_BANDS - 1, HY_BANDS, dtype=f32)[None, :]
    z = jnp.concatenate([t, jnp.cos(f * w), -jnp.sin(f * w)], axis=-1)
    fr = freq.astype(f32)
    h = jnp.sin(fr * (z @ w1.astype(f32) + b1.astype(f32)))
    h = jnp.sin(fr * (h @ w2.astype(f32) + b2.astype(f32)))
    h = h @ w3.astype(f32)
    max_decay = math.log(HY_DECAY_TARGET) / HY_SHORT_DECAY_PCT
    min_decay = math.log(HY_DECAY_TARGET) / HY_LONG_DECAY_PCT
    deltas = jnp.abs(jnp.linspace(min_decay, max_decay, HY_WIDTH, dtype=f32))
    window = jnp.exp(-t * deltas)
    h_fwd = h[:, :HY_WIDTH] * window
    h_bwd = h[:, HY_WIDTH:] * window
    k = jnp.concatenate([h_fwd, jnp.zeros((1, HY_WIDTH), f32), jnp.flip(h_bwd[1:], axis=0)], axis=0)
    return k * lax.rsqrt(jnp.sum(k * k, axis=0, keepdims=True) + EPS)


def long_conv(u, k, bias):
    seq_len = u.shape[1]
    uf = u.astype(jnp.float32)
    u_hat = jnp.fft.rfft(uf, n=2 * seq_len, axis=1)
    k_hat = jnp.fft.rfft(k, n=2 * seq_len, axis=0)
    y = jnp.fft.irfft(u_hat * k_hat[None], n=2 * seq_len, axis=1)[:, :seq_len]
    return (y + uf * bias.astype(jnp.float32)).astype(u.dtype)


def hyena_mixer(u, conv_w, conv_b, w1, b1, w2, b2, w3, freq, bias):
    u = dwconv(u, conv_w, conv_b)
    v, x1, x2 = jnp.split(u, 3, axis=-1)
    k = hyena_filter(u.shape[1], w1, b1, w2, b2, w3, freq)
    return x1 * long_conv(x2 * v, k, bias)


def fourier_mixer(u):
    b, seq_len, _ = u.shape
    uf = u.astype(jnp.float32).reshape(b, seq_len, FN_GROUPS, FN_GROUP_DIM)
    y = jnp.fft.fft2(uf, axes=(1, 3), norm='ortho').real
    return y.reshape(b, seq_len, FN_WIDTH).astype(u.dtype)


def ssd_scan(x, dt, a, bm, cm, h0, with_output):
    f32 = jnp.float32
    b, seq_len, n_heads, hd = x.shape
    n_groups, n_state = bm.shape[2], bm.shape[3]
    hg = n_heads // n_groups
    q = SSD_CHUNK
    nc = seq_len // q
    xdt = (x.astype(f32) * dt[..., None]).reshape(b, nc, q, n_groups, hg, hd)
    a_cum = jnp.cumsum((dt * a).reshape(b, nc, q, n_groups, hg), axis=2)
    bc = bm.astype(f32).reshape(b, nc, q, n_groups, n_state)
    decay_end = jnp.exp(a_cum[:, :, -1:] - a_cum)
    states = jnp.einsum('bcjgn,bcjgkp->bcgkpn', bc, decay_end[..., None] * xdt)
    chunk_decay = jnp.exp(a_cum[:, :, -1])

    def step(h, inp):
        s, d = inp
        return h * d[..., None, None] + s, h

    h_last, h_in = lax.scan(step, h0, (jnp.moveaxis(states, 1, 0), jnp.moveaxis(chunk_decay, 1, 0)))
    if not with_output:
        return h_last
    h_in = jnp.moveaxis(h_in, 0, 1)
    cc = cm.astype(f32).reshape(b, nc, q, n_groups, n_state)
    lower = jnp.tril(jnp.ones((q, q), dtype=bool))[:, :, None, None]
    seg = a_cum[:, :, :, None] - a_cum[:, :, None, :]
    lmat = jnp.where(lower, jnp.exp(jnp.where(lower, seg, 0.0)), 0.0)
    cb = jnp.einsum('bcign,bcjgn->bcijg', cc, bc)
    y_diag = jnp.einsum('bcijgk,bcjgkp->bcigkp', cb[..., None] * lmat, xdt)
    y_off = jnp.einsum('bcign,bcgkpn->bcigkp', cc, h_in) * jnp.exp(a_cum)[..., None]
    return (y_diag + y_off).reshape(b, seq_len, n_heads, hd), h_last


def gated_group_rmsnorm(y, z, g):
    b, seq_len, _ = y.shape
    yf = (y * jax.nn.silu(z.astype(jnp.float32))).reshape(b, seq_len, SSD_GROUPS, SSD_INNER // SSD_GROUPS)
    yf = yf * lax.rsqrt(jnp.mean(yf * yf, axis=-1, keepdims=True) + EPS)
    return yf.reshape(b, seq_len, SSD_INNER).astype(z.dtype) * g


def ssd_mixer(z, xbc, dt_raw, conv_w, conv_b, dt_bias, a_log, d_skip, norm_g, h0_fwd, h0_bwd, with_output):
    f32 = jnp.float32
    b, seq_len, _ = xbc.shape
    gn = SSD_GROUPS * SSD_STATE
    xbc = jax.nn.silu(dwconv(xbc, conv_w, conv_b))
    x = xbc[..., :SSD_INNER].reshape(b, seq_len, SSD_HEADS, SSD_HEAD_DIM)
    bm = xbc[..., SSD_INNER:SSD_INNER + gn].reshape(b, seq_len, SSD_GROUPS, SSD_STATE)
    cm = xbc[..., SSD_INNER + gn:].reshape(b, seq_len, SSD_GROUPS, SSD_STATE)
    dt = jax.nn.softplus(dt_raw.astype(f32).reshape(b, seq_len, 2, SSD_HEADS) + dt_bias.astype(f32))
    a = -jnp.exp(a_log.astype(f32))
    flip = lambda arr: jnp.flip(arr, axis=1)
    out_f = ssd_scan(x, dt[:, :, 0], a[0], bm, cm, h0_fwd, with_output)
    out_b = ssd_scan(flip(x), flip(dt[:, :, 1]), a[1], flip(bm), flip(cm), h0_bwd, with_output)
    if not with_output:
        return None, out_f, out_b
    y_f, h_f = out_f
    y_b, h_b = out_b
    y = y_f + flip(y_b) + d_skip.astype(f32)[:, None] * x.astype(f32)
    y = gated_group_rmsnorm(y.reshape(b, seq_len, SSD_INNER), z, norm_g)
    return y, h_f, h_b


def peer_ffn(h, wq, k1, k2, u_tab, v_tab):
    f32 = jnp.float32
    b, seq_len, d = h.shape
    n_tok = b * seq_len
    hf = h.reshape(n_tok, d)
    q = (hf @ wq).reshape(n_tok, PEER_HEADS, 2, PEER_HALF).astype(f32)
    s1 = jnp.einsum('thd,nd->thn', q[:, :, 0], k1.astype(f32))
    s2 = jnp.einsum('thd,nd->thn', q[:, :, 1], k2.astype(f32))
    v1, i1 = lax.top_k(s1, PEER_TOPK)
    v2, i2 = lax.top_k(s2, PEER_TOPK)
    cand = (v1[..., :, None] + v2[..., None, :]).reshape(n_tok, PEER_HEADS, PEER_TOPK * PEER_TOPK)
    sc, ci = lax.top_k(cand, PEER_TOPK)
    e1 = jnp.take_along_axis(i1, ci // PEER_TOPK, axis=-1)
    e2 = jnp.take_along_axis(i2, ci % PEER_TOPK, axis=-1)
    idx = (e1 * N_KEYS + e2).reshape(n_tok, PEER_HEADS * PEER_TOPK)
    gates = jax.nn.softmax(sc, axis=-1).reshape(n_tok, PEER_HEADS * PEER_TOPK)
    nb = n_tok // PEER_BLOCK

    def block(args):
        hb, ib, gb = args
        us = jnp.take(u_tab, ib, axis=0)
        act = jax.nn.gelu(jnp.einsum('td,tkd->tk', hb, us).astype(f32)) * gb
        vs = jnp.take(v_tab, ib, axis=0)
        return jnp.einsum('tk,tkd->td', act.astype(hb.dtype), vs)

    out = lax.map(block, (hf.reshape(nb, PEER_BLOCK, d), idx.reshape(nb, PEER_BLOCK, -1),
                          gates.reshape(nb, PEER_BLOCK, -1)))
    return out.reshape(b, seq_len, d).astype(h.dtype)


def trunk_layer(xl, xc, c, c_ctx, p, ctx_out):
    f32 = jnp.float32
    mod_l = (jax.nn.silu(c) @ p['w_ada'] + p['b_ada'])[:, None, :]
    mod_c = jax.nn.silu(c_ctx) @ p['w_ada'] + p['b_ada']
    sh1, sc1, ga1, sh2, sc2, ga2 = jnp.split(mod_l, 6, axis=-1)
    csh1, csc1, cga1, csh2, csc2, cga2 = jnp.split(mod_c, 6, axis=-1)
    pl = modulate(rmsnorm(xl, p['g_norm1']), sh1, sc1) @ p['w_in']
    pc = modulate(rmsnorm(xc, p['g_norm1']), csh1, csc1) @ p['w_in']
    ssd_w = (p['ssd_conv_w'], p['ssd_conv_b'], p['ssd_dt_bias'], p['ssd_a_log'], p['ssd_d'], p['ssd_norm_g'])

    def ssd_cols(proj):
        return proj[..., OFF_Z:OFF_XBC], proj[..., OFF_XBC:OFF_DT], proj[..., OFF_DT:OFF_FN]

    h0 = jnp.zeros((xl.shape[0], SSD_GROUPS, SSD_HEADS // SSD_GROUPS, SSD_HEAD_DIM, SSD_STATE), f32)
    yc_ssd, hc_fwd, hc_bwd = ssd_mixer(*ssd_cols(pc), *ssd_w, h0, h0, ctx_out)
    yl_ssd, _, _ = ssd_mixer(*ssd_cols(pl), *ssd_w, hc_fwd, hc_bwd, True)

    def token_mix(proj, y_ssd):
        y_hy = hyena_mixer(proj[..., OFF_HY:OFF_Z], p['hy_conv_w'], p['hy_conv_b'], p['hf_w1'], p['hf_b1'],
                           p['hf_w2'], p['hf_b2'], p['hf_w3'], p['hf_freq'], p['hy_bias'])
        y_fn = fourier_mixer(proj[..., OFF_FN:D_IN_PROJ])
        return jnp.concatenate([y_hy, y_ssd.astype(y_hy.dtype), y_fn], axis=-1) @ p['w_out']

    def channel_mix(h_in, shift, scale):
        return peer_ffn(modulate(rmsnorm(h_in, p['g_norm2']), shift, scale),
                        p['peer_wq'], p['peer_k1'], p['peer_k2'], p['peer_u'], p['peer_v'])

    xl = xl + ga1 * token_mix(pl, yl_ssd)
    xl = xl + ga2 * channel_mix(xl, sh2, sc2)
    if ctx_out:
        xc = xc + cga1 * token_mix(pc, yc_ssd)
        xc = xc + cga2 * channel_mix(xc, csh2, csc2)
    return xl, xc


def setup_inputs(seed: int = 0) -> dict:
    key = jax.random.key(seed)
    ks = list(jax.random.split(key, 40))
    f32 = jnp.float32

    def nrm(shape, scale):
        return jax.random.normal(ks.pop(), shape, f32) * scale

    nl = DEPTH
    x = nrm((BATCH, SEQ, D_MODEL), 1.0)
    c = nrm((BATCH, D_MODEL), 1.0)
    ctx = nrm((BATCH, CTX_LEN, D_MODEL), 1.0)
    c_ctx = nrm((D_MODEL,), 1.0)
    w_ada = nrm((nl, D_MODEL, 6 * D_MODEL), D_MODEL ** -0.5)
    b_ada = nrm((nl, 6 * D_MODEL), 0.02)
    g_norm1 = 1.0 + nrm((nl, D_MODEL), 0.02)
    g_norm2 = 1.0 + nrm((nl, D_MODEL), 0.02)
    w_in = nrm((nl, D_MODEL, D_IN_PROJ), D_MODEL ** -0.5)
    hy_conv_w = nrm((nl, HY_SHORT, 3 * HY_WIDTH), HY_SHORT ** -0.5)
    hy_conv_b = nrm((nl, 3 * HY_WIDTH), 0.02)
    hf_w1 = nrm((nl, HY_EMB, HY_HIDDEN), HY_EMB ** -0.5)
    hf_b1 = nrm((nl, HY_HIDDEN), 0.1)
    hf_w2 = nrm((nl, HY_HIDDEN, HY_HIDDEN), HY_HIDDEN ** -0.5)
    hf_b2 = nrm((nl, HY_HIDDEN), 0.1)
    hf_w3 = nrm((nl, HY_HIDDEN, 2 * HY_WIDTH), HY_HIDDEN ** -0.5)
    hf_freq = 1.0 + nrm((nl, HY_HIDDEN), 0.1)
    hy_bias = nrm((nl, HY_WIDTH), 0.5)
    ssd_conv_w = nrm((nl, SSD_CONV, SSD_XBC), SSD_CONV ** -0.5)
    ssd_conv_b = nrm((nl, SSD_XBC), 0.02)
    dt0 = jnp.exp(jax.random.uniform(ks.pop(), (nl, 2, SSD_HEADS), f32, math.log(1e-3), math.log(1e-1)))
    ssd_dt_bias = dt0 + jnp.log(-jnp.expm1(-dt0))
    ssd_a_log = jnp.log(jax.random.uniform(ks.pop(), (nl, 2, SSD_HEADS), f32, 1.0, 16.0))
    ssd_d = 1.0 + nrm((nl, SSD_HEADS), 0.1)
    ssd_norm_g = 1.0 + nrm((nl, SSD_INNER), 0.02)
    w_out = nrm((nl, D_MIX, D_MODEL), D_MIX ** -0.5)
    peer_wq = nrm((nl, D_MODEL, PEER_HEADS * PEER_QDIM), D_MODEL ** -0.5)
    peer_k1 = nrm((nl, N_KEYS, PEER_HALF), PEER_HALF ** -0.5)
    peer_k2 = nrm((nl, N_KEYS, PEER_HALF), PEER_HALF ** -0.5)
    peer_u = nrm((nl, N_EXPERTS, D_MODEL), D_MODEL ** -0.5)
    peer_v = nrm((nl, N_EXPERTS, D_MODEL), PEER_HEADS ** -0.5)
    g_final = 1.0 + nrm((D_MODEL,), 0.02)
    return {'x': x, 'c': c, 'ctx': ctx, 'c_ctx': c_ctx, 'w_ada': w_ada, 'b_ada': b_ada,
            'g_norm1': g_norm1, 'g_norm2': g_norm2, 'w_in': w_in, 'hy_conv_w': hy_conv_w,
            'hy_conv_b': hy_conv_b, 'hf_w1': hf_w1, 'hf_b1': hf_b1, 'hf_w2': hf_w2, 'hf_b2': hf_b2,
            'hf_w3': hf_w3, 'hf_freq': hf_freq, 'hy_bias': hy_bias, 'ssd_conv_w': ssd_conv_w,
            'ssd_conv_b': ssd_conv_b, 'ssd_dt_bias': ssd_dt_bias, 'ssd_a_log': ssd_a_log, 'ssd_d': ssd_d,
            'ssd_norm_g': ssd_norm_g, 'w_out': w_out, 'peer_wq': peer_wq, 'peer_k1': peer_k1,
            'peer_k2': peer_k2, 'peer_u': peer_u, 'peer_v': peer_v, 'g_final': g_final}


def reference(x, c, ctx, c_ctx, w_ada, b_ada, g_norm1, g_norm2, w_in, hy_conv_w, hy_conv_b, hf_w1, hf_b1,
              hf_w2, hf_b2, hf_w3, hf_freq, hy_bias, ssd_conv_w, ssd_conv_b, ssd_dt_bias, ssd_a_log, ssd_d,
              ssd_norm_g, w_out, peer_wq, peer_k1, peer_k2, peer_u, peer_v, g_final):
    xl, xc = x, ctx
    for i in range(DEPTH):
        p = {'w_ada': w_ada[i], 'b_ada': b_ada[i], 'g_norm1': g_norm1[i], 'g_norm2': g_norm2[i],
             'w_in': w_in[i], 'hy_conv_w': hy_conv_w[i], 'hy_conv_b': hy_conv_b[i], 'hf_w1': hf_w1[i],
             'hf_b1': hf_b1[i], 'hf_w2': hf_w2[i], 'hf_b2': hf_b2[i], 'hf_w3': hf_w3[i], 'hf_freq': hf_freq[i],
             'hy_bias': hy_bias[i], 'ssd_conv_w': ssd_conv_w[i], 'ssd_conv_b': ssd_conv_b[i],
             'ssd_dt_bias': ssd_dt_bias[i], 'ssd_a_log': ssd_a_log[i], 'ssd_d': ssd_d[i],
             'ssd_norm_g': ssd_norm_g[i], 'w_out': w_out[i], 'peer_wq': peer_wq[i], 'peer_k1': peer_k1[i],
             'peer_k2': peer_k2[i], 'peer_u': peer_u[i], 'peer_v': peer_v[i]}
        xl, xc = trunk_layer(xl, xc, c, c_ctx, p, i < DEPTH - 1)
    return rmsnorm(xl, g_final)
```

```python
import functools
import math

import jax
import jax.numpy as jnp
import numpy as np
from jax import lax
from jax.experimental import pallas as pl
from jax.experimental.pallas import tpu as pltpu

F32 = jnp.float32
BF16 = jnp.bfloat16
HI = lax.Precision.HIGHEST

EPS = 1e-6
LANES = 128
SUBLANES = 8
VMEM_LIMIT = 56 * 1024 * 1024

HY_WIDTH = 256
HY_EMB = 33
HY_BANDS = (HY_EMB - 1) // 2
HY_DECAY_TARGET = 1e-2
HY_SHORT_DECAY_PCT = 0.3
HY_LONG_DECAY_PCT = 1.5
SSD_HEADS = 8
SSD_HEAD_DIM = 64
SSD_INNER = SSD_HEADS * SSD_HEAD_DIM
SSD_GROUPS = 2
SSD_STATE = 128
SSD_CHUNK = 128
SSD_XBC = SSD_INNER + 2 * SSD_GROUPS * SSD_STATE
FN_WIDTH = 256
FN_GROUPS = 4
FN_GROUP_DIM = FN_WIDTH // FN_GROUPS
PEER_HEADS = 8
PEER_TOPK = 16
N_KEYS = 128
PEER_HALF = 128
OFF_HY = 0
OFF_Z = OFF_HY + 3 * HY_WIDTH
OFF_XBC = OFF_Z + SSD_INNER
OFF_DT = OFF_XBC + SSD_XBC
OFF_FN = OFF_DT + 2 * SSD_HEADS
D_IN_PROJ = OFF_FN + FN_WIDTH


def _params(*sem):
    return pltpu.CompilerParams(dimension_semantics=sem, vmem_limit_bytes=VMEM_LIMIT)


def _silu(x):
    return x * jax.nn.sigmoid(x)


def _softplus(x):
    return jnp.maximum(x, 0.0) + jnp.log1p(jnp.exp(-jnp.abs(x)))


def _gelu_tanh(x):
    return 0.5 * x * (1.0 + jnp.tanh(math.sqrt(2.0 / math.pi) * (x + 0.044715 * (x * x * x))))


def _ada_kernel(c_ref, w_ref, b_ref, o_ref):
    s = _silu(c_ref[...])
    o_ref[0] = jnp.dot(s, w_ref[0], preferred_element_type=F32, precision=HI) + b_ref[0]


def _ada_mod(cc, w_ada, b_ada):
    depth, d, n = w_ada.shape
    r = cc.shape[0]
    tn = 1536
    return pl.pallas_call(
        _ada_kernel,
        out_shape=jax.ShapeDtypeStruct((depth, r, n), F32),
        grid=(depth, n // tn),
        in_specs=[pl.BlockSpec((r, d), lambda l, j: (0, 0)),
                  pl.BlockSpec((1, d, tn), lambda l, j: (l, 0, j)),
                  pl.BlockSpec((1, 1, tn), lambda l, j: (l, 0, j))],
        out_specs=pl.BlockSpec((1, r, tn), lambda l, j: (l, 0, j)),
        compiler_params=_params("arbitrary", "arbitrary"),
        name="ada_mod",
    )(cc, w_ada, b_ada.reshape(depth, 1, n))


def _normmod(x, g, sh, sc):
    y = x * lax.rsqrt(jnp.mean(x * x, axis=-1, keepdims=True) + EPS) * g
    return y * (1.0 + sc) + sh


def _inproj_kernel(x_ref, g_ref, sh_ref, sc_ref, *rest, n_w):
    w_refs, o_refs = rest[:n_w], rest[n_w:]
    hb = _normmod(x_ref[0], g_ref[...], sh_ref[0], sc_ref[0]).astype(BF16)
    for w_ref, o_ref in zip(w_refs, o_refs):
        o_ref[0] = jnp.dot(hb, w_ref[...], preferred_element_type=F32).astype(o_ref.dtype)


def _in_proj(x, g, sh, sc, ws):
    b, l, d = x.shape
    tm = min(512, l)
    return pl.pallas_call(
        functools.partial(_inproj_kernel, n_w=len(ws)),
        out_shape=[jax.ShapeDtypeStruct((b, l, w.shape[1]), F32) for w in ws],
        grid=(b, l // tm),
        in_specs=[pl.BlockSpec((1, tm, d), lambda i, j: (i, j, 0)),
                  pl.BlockSpec((1, d), lambda i, j: (0, 0)),
                  pl.BlockSpec((1, 1, d), lambda i, j: (i, 0, 0)),
                  pl.BlockSpec((1, 1, d), lambda i, j: (i, 0, 0))]
        + [pl.BlockSpec(w.shape, lambda i, j: (0, 0)) for w in ws],
        out_specs=[pl.BlockSpec((1, tm, w.shape[1]), lambda i, j: (i, j, 0)) for w in ws],
        compiler_params=_params("arbitrary", "arbitrary"),
        name="in_proj",
    )(x, g, sh, sc, *ws)


def _dwconv_kernel(u_ref, w_ref, b_ref, o_ref, *, act):
    u = u_ref[0]
    l = u.shape[0]
    row = lax.broadcasted_iota(jnp.int32, u.shape, 0)
    um = jnp.where(row == 0, 0.0, pltpu.roll(u, 1, 0))
    up = jnp.where(row == l - 1, 0.0, pltpu.roll(u, l - 1, 0))
    y = um * w_ref[0:1, :] + u * w_ref[1:2, :] + up * w_ref[2:3, :] + b_ref[...]
    if act:
        y = _silu(y)
    o_ref[0] = y


def _dwconv(u, w, bias, act):
    b, l, c = u.shape
    cb = 256
    return pl.pallas_call(
        functools.partial(_dwconv_kernel, act=act),
        out_shape=jax.ShapeDtypeStruct((b, l, c), F32),
        grid=(b, c // cb),
        in_specs=[pl.BlockSpec((1, l, cb), lambda i, j: (i, 0, j)),
                  pl.BlockSpec((3, cb), lambda i, j: (0, j)),
                  pl.BlockSpec((1, cb), lambda i, j: (0, j))],
        out_specs=pl.BlockSpec((1, l, cb), lambda i, j: (i, 0, j)),
        compiler_params=_params("arbitrary", "arbitrary"),
        name="dwconv",
    )(u, w, bias.reshape(1, c))


def _ssd_kernel(xbc_ref, dt_ref, dtb_ref, arow_ref, h0_ref, y_ref, hl_ref, st_ref, *, direction, nc):
    q = SSD_CHUNK
    c = pl.program_id(1)

    @pl.when(c == 0)
    def _():
        st_ref[...] = h0_ref[0]

    xbc = xbc_ref[0]
    x = xbc[:, :SSD_INNER]
    gn = SSD_GROUPS * SSD_STATE
    bm = xbc[:, SSD_INNER:SSD_INNER + gn].astype(BF16)
    cm = xbc[:, SSD_INNER + gn:].astype(BF16)
    dtp = _softplus(dt_ref[0] + dtb_ref[...])
    da = dtp * arow_ref[...]
    ii = lax.broadcasted_iota(jnp.int32, (q, q), 0)
    jj = lax.broadcasted_iota(jnp.int32, (q, q), 1)
    tri = (ii >= jj) if direction == 0 else (ii <= jj)
    cs = jnp.dot(tri.astype(F32), da, preferred_element_type=F32, precision=HI)
    cs_t = cs.T
    lane = lax.broadcasted_iota(jnp.int32, (LANES, SSD_INNER), 0)
    chan = lax.broadcasted_iota(jnp.int32, (LANES, SSD_INNER), 1)
    head_of = lax.shift_right_logical(chan, int(math.log2(SSD_HEAD_DIM)))
    expand = (lane == direction * SSD_HEADS + head_of).astype(F32)
    acum = jnp.dot(cs, expand, preferred_element_type=F32, precision=HI)
    dtf = jnp.dot(dtp, expand, preferred_element_type=F32, precision=HI)
    tot = acum[q - 1:q, :] if direction == 0 else acum[0:1, :]
    xdt = x * dtf
    ea = jnp.exp(acum)
    xdec = (jnp.exp(tot - acum) * xdt).astype(BF16)
    cd = jnp.exp(tot)
    xdt_b = xdt.astype(BF16)
    hg = SSD_HEADS // SSD_GROUPS
    gw = hg * SSD_HEAD_DIM
    outs = []
    for g in range(SSD_GROUPS):
        bg = bm[:, g * SSD_STATE:(g + 1) * SSD_STATE]
        cg = cm[:, g * SSD_STATE:(g + 1) * SSD_STATE]
        cb = lax.dot_general(cg, bg, (((1,), (1,)), ((), ())), preferred_element_type=F32)
        st = st_ref[g]
        y_off = jnp.dot(cg, st.astype(BF16), preferred_element_type=F32) * ea[:, g * gw:(g + 1) * gw]
        xg = xdt_b[:, g * gw:(g + 1) * gw]
        head_g = lax.shift_right_logical(lax.broadcasted_iota(jnp.int32, xg.shape, 1), int(math.log2(SSD_HEAD_DIM)))
        y_g = y_off
        for k in range(hg):
            ln = direction * SSD_HEADS + g * hg + k
            seg = cs[:, ln:ln + 1] - cs_t[ln:ln + 1, :]
            lmat = jnp.where(tri, jnp.exp(jnp.where(tri, seg, 0.0)), 0.0)
            m = (cb * lmat).astype(BF16)
            y_g = y_g + jnp.dot(m, jnp.where(head_g == k, xg, jnp.zeros_like(xg)), preferred_element_type=F32)
        outs.append(y_g)
        sg = lax.dot_general(bg, xdec[:, g * gw:(g + 1) * gw], (((0,), (0,)), ((), ())),
                             preferred_element_type=F32)
        st_ref[g] = st * cd[:, g * gw:(g + 1) * gw] + sg
    y_ref[0] = jnp.concatenate(outs, axis=1)

    @pl.when(c == nc - 1)
    def _():
        hl_ref[0] = st_ref[...]


def _ssd_scan(xbc, dt, dtb_row, a_row, h0, direction):
    b, l, _ = xbc.shape
    nc = l // SSD_CHUNK
    cidx = (lambda i, c: (i, c, 0)) if direction == 0 else (lambda i, c: (i, nc - 1 - c, 0))
    st_shape = (SSD_GROUPS, SSD_STATE, SSD_INNER // SSD_GROUPS)
    return pl.pallas_call(
        functools.partial(_ssd_kernel, direction=direction, nc=nc),
        out_shape=[jax.ShapeDtypeStruct((b, l, SSD_INNER), F32),
                   jax.ShapeDtypeStruct((b,) + st_shape, F32)],
        grid=(b, nc),
        in_specs=[pl.BlockSpec((1, SSD_CHUNK, SSD_XBC), cidx),
                  pl.BlockSpec((1, SSD_CHUNK, LANES), cidx),
                  pl.BlockSpec((1, LANES), lambda i, c: (0, 0)),
                  pl.BlockSpec((1, LANES), lambda i, c: (0, 0)),
                  pl.BlockSpec((1,) + st_shape, lambda i, c: (i, 0, 0, 0))],
        out_specs=[pl.BlockSpec((1, SSD_CHUNK, SSD_INNER), cidx),
                   pl.BlockSpec((1,) + st_shape, lambda i, c: (i, 0, 0, 0))],
        scratch_shapes=[pltpu.VMEM(st_shape, F32)],
        compiler_params=_params("arbitrary", "arbitrary"),
        name=f"ssd_scan_d{direction}",
    )(xbc, dt, dtb_row, a_row, h0)


def _hy_filter_kernel(z_ref, w1_ref, b1_ref, w2_ref, b2_ref, w3_ref, fr_ref, win_ref, alt_ref,
                      fs_ref, fd_ref, nyq_ref):
    fr = fr_ref[...]
    h = jnp.sin(fr * (jnp.dot(z_ref[...], w1_ref[...], preferred_element_type=F32, precision=HI) + b1_ref[...]))
    h = jnp.sin(fr * (jnp.dot(h, w2_ref[...], preferred_element_type=F32, precision=HI) + b2_ref[...]))
    h = jnp.dot(h, w3_ref[...], preferred_element_type=F32, precision=HI)
    win = win_ref[...]
    row = lax.broadcasted_iota(jnp.int32, win.shape, 0)
    hf = h[:, :HY_WIDTH] * win
    hb = jnp.where(row == 0, 0.0, h[:, HY_WIDTH:] * win)
    nrm = lax.rsqrt(jnp.sum(hf * hf, axis=0, keepdims=True) + jnp.sum(hb * hb, axis=0, keepdims=True) + EPS)
    fs = (hf + hb) * nrm
    fs_ref[...] = fs
    fd_ref[...] = (hb - hf) * nrm
    nyq_ref[...] = jnp.sum(fs * alt_ref[...], axis=0, keepdims=True)


def _hy_filter(l, w1, b1, w2, b2, w3, freq):
    t = np.linspace(0.0, 1.0, l, dtype=np.float32)[:, None]
    w = (np.float32(2.0 * math.pi) * np.arange(l, dtype=np.float32)[:, None] / np.float32(l)).astype(np.float32)
    f = np.linspace(1e-4, HY_BANDS - 1, HY_BANDS, dtype=np.float32)[None, :]
    zf = jnp.asarray(f) * jnp.asarray(w)
    z = jnp.concatenate([jnp.asarray(t), jnp.cos(zf), -jnp.sin(zf)], axis=-1)
    z = jnp.pad(z, ((0, 0), (0, LANES - HY_EMB)))
    hid = w1.shape[1]
    w1p = jnp.pad(w1, ((0, LANES - HY_EMB), (0, LANES - hid)))
    w2p = jnp.pad(w2, ((0, LANES - hid), (0, LANES - hid)))
    w3p = jnp.pad(w3, ((0, LANES - hid), (0, 0)))
    pad_row = lambda v: jnp.pad(v.reshape(1, hid), ((0, 0), (0, LANES - hid)))
    max_decay = math.log(HY_DECAY_TARGET) / HY_SHORT_DECAY_PCT
    min_decay = math.log(HY_DECAY_TARGET) / HY_LONG_DECAY_PCT
    deltas = jnp.abs(jnp.linspace(min_decay, max_decay, HY_WIDTH, dtype=F32))
    win = jnp.exp(-jnp.asarray(t) * deltas)
    alt = jnp.asarray(np.where(np.arange(l) % 2 == 0, 1.0, -1.0).astype(np.float32)[:, None] * np.ones((1, HY_WIDTH), np.float32))
    return pl.pallas_call(
        _hy_filter_kernel,
        out_shape=[jax.ShapeDtypeStruct((l, HY_WIDTH), F32), jax.ShapeDtypeStruct((l, HY_WIDTH), F32),
                   jax.ShapeDtypeStruct((1, HY_WIDTH), F32)],
        compiler_params=pltpu.CompilerParams(vmem_limit_bytes=VMEM_LIMIT),
        name="hy_filter",
    )(z, w1p, pad_row(b1), w2p, pad_row(b2), w3p, pad_row(freq), win, alt)


def _mm_kernel(a_ref, b_ref, o_ref):
    o_ref[...] = jnp.dot(a_ref[...], b_ref[...], preferred_element_type=F32)


def _mm(a, b, tm):
    m, k = a.shape
    n = b.shape[1]
    return pl.pallas_call(
        _mm_kernel,
        out_shape=jax.ShapeDtypeStruct((m, n), F32),
        grid=(m // tm,),
        in_specs=[pl.BlockSpec((tm, k), lambda i: (i, 0)), pl.BlockSpec((k, n), lambda i: (0, 0))],
        out_specs=pl.BlockSpec((tm, n), lambda i: (i, 0)),
        compiler_params=_params("arbitrary"),
        name="mm",
    )(a, b)


def _dft_tables(l):
    n = 2 * l
    k = np.arange(l, dtype=np.int64)[:, None]
    s = np.arange(l, dtype=np.int64)[None, :]
    ang = (2.0 * np.pi / n) * ((k * s) % n).astype(np.float64)
    cos, sin = np.cos(ang), np.sin(ang)
    alt = np.where(np.arange(l) % 2 == 0, 1.0, -1.0)
    sin[0, :] = alt
    fwd = np.concatenate([cos, sin], axis=0)
    wgt = np.full((l, 1), 2.0)
    wgt[0, 0] = 1.0
    sin_i = -sin * wgt
    sin_i[0, :] = alt
    inv = np.concatenate([(cos * wgt).T, sin_i.T], axis=1) / n
    return fwd, inv


def _hy_conv_kernel(u_ref, ff_ref, fi_ref, ka_ref, kb_ref, kc_ref, bias_ref, o_ref, ub_ref, acc_ref, *, nf):
    f = pl.program_id(1)
    w = HY_WIDTH

    @pl.when(f == 0)
    def _():
        u3 = u_ref[0]
        ub_ref[...] = (u3[:, 2 * w:] * u3[:, :w]).astype(BF16)
        acc_ref[...] = jnp.zeros_like(acc_ref)

    uf = jnp.dot(ff_ref[0], ub_ref[...], preferred_element_type=F32)
    fb = uf.shape[0] // 2
    ur, us = uf[:fb], uf[fb:]
    ka, kb, kc = ka_ref[...], kb_ref[...], kc_ref[...]
    p = jnp.concatenate([ur * ka + us * kb, ur * kb - us * kc], axis=0).astype(BF16)
    acc_ref[...] += jnp.dot(fi_ref[0], p, preferred_element_type=F32)

    @pl.when(f == nf - 1)
    def _():
        u3 = u_ref[0]
        o_ref[0] = u3[:, w:2 * w] * (acc_ref[...] + u3[:, 2 * w:] * u3[:, :w] * bias_ref[...])


def _hy_conv(u3, fs, fd, nyq, bias):
    b, l, _ = u3.shape
    fwd, inv = _dft_tables(l)
    fb = min(512, l)
    nf = l // fb
    ff = jnp.asarray(np.stack([np.concatenate([fwd[i * fb:(i + 1) * fb], fwd[l + i * fb:l + (i + 1) * fb]], axis=0)
                               for i in range(nf)]), dtype=BF16)
    fi = jnp.asarray(np.stack([np.concatenate([inv[:, i * fb:(i + 1) * fb], inv[:, l + i * fb:l + (i + 1) * fb]], axis=1)
                               for i in range(nf)]), dtype=BF16)
    fwd_b = jnp.asarray(fwd, dtype=BF16)
    tm = min(512, l)
    k_r = _mm(fwd_b[:l], fs.astype(BF16), tm)
    k_i = _mm(fwd_b[l:], fd.astype(BF16), tm)
    first = (jnp.arange(l) == 0)[:, None]
    ka = k_r
    kb = jnp.where(first, 0.0, k_i)
    kc = jnp.where(first, -nyq, k_r)
    return pl.pallas_call(
        functools.partial(_hy_conv_kernel, nf=nf),
        out_shape=jax.ShapeDtypeStruct((b, l, HY_WIDTH), F32),
        grid=(b, nf),
        in_specs=[pl.BlockSpec((1, l, 3 * HY_WIDTH), lambda i, f: (i, 0, 0)),
                  pl.BlockSpec((1, 2 * fb, l), lambda i, f: (f, 0, 0)),
                  pl.BlockSpec((1, l, 2 * fb), lambda i, f: (f, 0, 0)),
                  pl.BlockSpec((fb, HY_WIDTH), lambda i, f: (f, 0)),
                  pl.BlockSpec((fb, HY_WIDTH), lambda i, f: (f, 0)),
                  pl.BlockSpec((fb, HY_WIDTH), lambda i, f: (f, 0)),
                  pl.BlockSpec((1, HY_WIDTH), lambda i, f: (0, 0))],
        out_specs=pl.BlockSpec((1, l, HY_WIDTH), lambda i, f: (i, 0, 0)),
        scratch_shapes=[pltpu.VMEM((l, HY_WIDTH), BF16), pltpu.VMEM((l, HY_WIDTH), F32)],
        compiler_params=_params("arbitrary", "arbitrary"),
        name="hy_conv",
    )(u3, ff, fi, ka, kb, kc, bias.reshape(1, HY_WIDTH))


def _fnet_kernel(x_ref, cc_ref, sc_ref, m_ref, o_ref, xx_ref):
    r = pl.program_id(1)
    l = x_ref.shape[1]

    @pl.when(r == 0)
    def _():
        xb = x_ref[0].astype(BF16)
        xx_ref[:l, :] = jnp.dot(xb, cc_ref[...], preferred_element_type=F32).astype(BF16)
        xx_ref[l:, :] = jnp.dot(xb, sc_ref[...], preferred_element_type=F32).astype(BF16)

    o_ref[0] = jnp.dot(m_ref[...], xx_ref[...], preferred_element_type=F32)


def _fnet(x):
    b, l, c = x.shape
    gd = FN_GROUP_DIM
    kc = np.arange(gd)[:, None] * np.arange(gd)[None, :]
    ang_c = 2.0 * np.pi * (kc % gd) / gd
    eye = np.eye(FN_GROUPS)
    cblk = jnp.asarray(np.kron(eye, np.cos(ang_c)), dtype=BF16)
    sblk = jnp.asarray(np.kron(eye, np.sin(ang_c)), dtype=BF16)
    kl = (np.arange(l, dtype=np.int64)[:, None] * np.arange(l, dtype=np.int64)[None, :]) % l
    ang_l = 2.0 * np.pi * kl / l
    scale = 1.0 / math.sqrt(l * gd)
    mat = jnp.asarray(np.concatenate([np.cos(ang_l), -np.sin(ang_l)], axis=1) * scale, dtype=BF16)
    tr = min(512, l)
    return pl.pallas_call(
        _fnet_kernel,
        out_shape=jax.ShapeDtypeStruct((b, l, c), F32),
        grid=(b, l // tr),
        in_specs=[pl.BlockSpec((1, l, c), lambda i, r: (i, 0, 0)),
                  pl.BlockSpec((c, c), lambda i, r: (0, 0)),
                  pl.BlockSpec((c, c), lambda i, r: (0, 0)),
                  pl.BlockSpec((tr, 2 * l), lambda i, r: (r, 0))],
        out_specs=pl.BlockSpec((1, tr, c), lambda i, r: (i, r, 0)),
        scratch_shapes=[pltpu.VMEM((2 * l, c), BF16)],
        compiler_params=_params("arbitrary", "arbitrary"),
        name="fnet",
    )(x, cblk, sblk, mat)


def _outproj_kernel(x_ref, ga_ref, yhy_ref, yf_ref, yb_ref, xs_ref, z_ref, yfn_ref, dsk_ref, ng_ref,
                    w1_ref, w2_ref, w3_ref, o_ref):
    y = yf_ref[0] + yb_ref[0] + dsk_ref[...] * xs_ref[0]
    y = y * _silu(z_ref[0])
    gw = SSD_INNER // SSD_GROUPS
    parts = []
    for g in range(SSD_GROUPS):
        yg = y[:, g * gw:(g + 1) * gw]
        parts.append(yg * lax.rsqrt(jnp.mean(yg * yg, axis=-1, keepdims=True) + EPS))
    yn = jnp.concatenate(parts, axis=1) * ng_ref[...]
    out = jnp.dot(yhy_ref[0].astype(BF16), w1_ref[...], preferred_element_type=F32)
    out += jnp.dot(yn.astype(BF16), w2_ref[...], preferred_element_type=F32)
    out += jnp.dot(yfn_ref[0].astype(BF16), w3_ref[...], preferred_element_type=F32)
    o_ref[0] = x_ref[0] + ga_ref[0] * out


def _out_proj(x, ga, y_hy, y_f, y_b, xbc, z, y_fn, dsk_row, ng_row, w1, w2, w3):
    b, l, d = x.shape
    tm = min(512, l)
    tok = lambda n: pl.BlockSpec((1, tm, n), lambda i, j: (i, j, 0))
    const = lambda a: pl.BlockSpec(a.shape, lambda i, j: (0,) * a.ndim)
    return pl.pallas_call(
        _outproj_kernel,
        out_shape=jax.ShapeDtypeStruct((b, l, d), F32),
        grid=(b, l // tm),
        in_specs=[tok(d), pl.BlockSpec((1, 1, d), lambda i, j: (i, 0, 0)), tok(HY_WIDTH), tok(SSD_INNER),
                  tok(SSD_INNER), tok(SSD_INNER), tok(SSD_INNER), tok(FN_WIDTH), const(dsk_row), const(ng_row),
                  const(w1), const(w2), const(w3)],
        out_specs=tok(d),
        compiler_params=_params("arbitrary", "arbitrary"),
        name="out_proj",
    )(x, ga, y_hy, y_f, y_b, xbc, z, y_fn, dsk_row, ng_row, w1, w2, w3)


def _peer_q_kernel(x_ref, g_ref, sh_ref, sc_ref, wq_ref, k1_ref, k2_ref, ht_ref, s1_ref, s2_ref):
    hm = _normmod(x_ref[...], g_ref[...], sh_ref[0], sc_ref[0])
    ht_ref[...] = hm.T.astype(BF16)
    qb = jnp.dot(hm.astype(BF16), wq_ref[...], preferred_element_type=F32).astype(BF16)
    nt = (((1,), (1,)), ((), ()))
    for h in range(PEER_HEADS):
        q1 = qb[:, (2 * h) * PEER_HALF:(2 * h + 1) * PEER_HALF]
        q2 = qb[:, (2 * h + 1) * PEER_HALF:(2 * h + 2) * PEER_HALF]
        s1_ref[h] = lax.dot_general(k1_ref[...], q1, nt, preferred_element_type=F32)
        s2_ref[h] = lax.dot_general(k2_ref[...], q2, nt, preferred_element_type=F32)


def _peer_q(x2, g, sh, sc, wq, k1, k2, l):
    n, d = x2.shape
    tm = min(256, l)
    per = l // tm
    return pl.pallas_call(
        _peer_q_kernel,
        out_shape=[jax.ShapeDtypeStruct((d, n), BF16),
                   jax.ShapeDtypeStruct((PEER_HEADS, N_KEYS, n), F32),
                   jax.ShapeDtypeStruct((PEER_HEADS, N_KEYS, n), F32)],
        grid=(n // tm,),
        in_specs=[pl.BlockSpec((tm, d), lambda i: (i, 0)),
                  pl.BlockSpec((1, d), lambda i: (0, 0)),
                  pl.BlockSpec((1, 1, d), lambda i: (i // per, 0, 0)),
                  pl.BlockSpec((1, 1, d), lambda i: (i // per, 0, 0)),
                  pl.BlockSpec(wq.shape, lambda i: (0, 0)),
                  pl.BlockSpec(k1.shape, lambda i: (0, 0)),
                  pl.BlockSpec(k2.shape, lambda i: (0, 0))],
        out_specs=[pl.BlockSpec((d, tm), lambda i: (0, i)),
                   pl.BlockSpec((PEER_HEADS, N_KEYS, tm), lambda i: (0, 0, i)),
                   pl.BlockSpec((PEER_HEADS, N_KEYS, tm), lambda i: (0, 0, i))],
        compiler_params=_params("arbitrary"),
        name="peer_q",
    )(x2, g, sh, sc, wq, k1, k2)


def _sorting_network(n):
    pairs = []

    def merge(lo, m, r):
        step = 2 * r
        if step < m:
            merge(lo, m, step)
            merge(lo + r, m, step)
            for i in range(lo + r, lo + m - r, step):
                pairs.append((i, i + r))
        else:
            pairs.append((lo, lo + r))

    def sort(lo, m):
        if m > 1:
            half = m // 2
            sort(lo, half)
            sort(lo + half, half)
            merge(lo, m, 1)

    sort(0, n)
    return tuple(pairs)


NEG = -3.0e38
N_TOP = PEER_TOPK + 1


def _top_sorted(rows, n_out):
    rows = list(rows)
    for i, j in _sorting_network(len(rows)):
        rows[i], rows[j] = jnp.maximum(rows[i], rows[j]), jnp.minimum(rows[i], rows[j])
    sub = lax.broadcasted_iota(jnp.int32, rows[0].shape, 0)
    out = []
    for r in range(n_out):
        m = jnp.max(rows[0], axis=0, keepdims=True)
        out.append(m)
        first = jnp.min(jnp.where(rows[0] == m, sub, SUBLANES), axis=0, keepdims=True)
        hit = sub == first
        keep = min(len(rows), n_out - r - 1)
        rows = [jnp.where(hit, rows[k + 1] if k + 1 < len(rows) else NEG, rows[k]) for k in range(keep)]
    return out


def _peer_stats_kernel(s1_ref, s2_ref, m1_ref, m2_ref, tau_ref, zinv_ref):
    nrow = N_KEYS // SUBLANES
    t = s1_ref.shape[2]
    sub = lax.broadcasted_iota(jnp.int32, (SUBLANES, t), 0)
    for h in range(PEER_HEADS):
        v1 = _top_sorted([s1_ref[h, SUBLANES * k:SUBLANES * (k + 1), :] for k in range(nrow)], N_TOP)
        v2 = _top_sorted([s2_ref[h, SUBLANES * k:SUBLANES * (k + 1), :] for k in range(nrow)], N_TOP)
        cands = [v1[i] + v2[j] for i in range(N_TOP) for j in range(N_TOP) if (i + 1) * (j + 1) <= N_TOP]
        packed = []
        for k in range(0, len(cands), SUBLANES):
            blk = jnp.full((SUBLANES, t), NEG, F32)
            for s, cv in enumerate(cands[k:k + SUBLANES]):
                blk = jnp.where(sub == s, cv, blk)
            packed.append(blk)
        ids = [sub + SUBLANES * k for k in range(len(packed))]
        big = SUBLANES * len(packed)
        top = []
        for r in range(N_TOP):
            m = packed[0]
            for blk in packed[1:]:
                m = jnp.maximum(m, blk)
            m = jnp.max(m, axis=0, keepdims=True)
            top.append(m)
            sel = jnp.where(packed[0] == m, ids[0], big)
            for blk, idk in zip(packed[1:], ids[1:]):
                sel = jnp.minimum(sel, jnp.where(blk == m, idk, big))
            sel = jnp.min(sel, axis=0, keepdims=True)
            packed = [jnp.where(idk == sel, NEG, blk) for blk, idk in zip(packed, ids)]
        mx = top[0]
        z = jnp.zeros((1, t), F32)
        for r in range(PEER_TOPK):
            z = z + jnp.exp(top[r] - mx)
        m1_ref[h:h + 1, :] = v1[0]
        m2_ref[h:h + 1, :] = v2[0]
        tau_ref[h:h + 1, :] = 0.5 * (top[PEER_TOPK - 1] + top[PEER_TOPK])
        zinv_ref[h:h + 1, :] = 1.0 / z


def _peer_stats(s1t, s2t):
    n = s1t.shape[2]
    t = LANES
    blk = pl.BlockSpec((PEER_HEADS, N_KEYS, t), lambda i: (0, 0, i))
    oblk = pl.BlockSpec((PEER_HEADS, t), lambda i: (0, i))
    return pl.pallas_call(
        _peer_stats_kernel,
        out_shape=[jax.ShapeDtypeStruct((PEER_HEADS, n), F32)] * 4,
        grid=(n // t,),
        in_specs=[blk, blk],
        out_specs=[oblk] * 4,
        compiler_params=_params("arbitrary"),
        name="peer_stats",
    )(s1t, s2t)


E_TILE = 1024


def _peer_mix_kernel(ht_ref, u_ref, vt_ref, s1_ref, s2_ref, m1_ref, m2_ref, tau_ref, zinv_ref, x_ref, ga_ref,
                     o_ref, th_ref, cc_ref, a2_ref, acc_ref, act_ref, st_ref, *, ne):
    j = pl.program_id(1)
    rows_per = E_TILE // N_KEYS

    @pl.when(j == 0)
    def _():
        for h in range(PEER_HEADS):
            s1 = s1_ref[h]
            th_ref[h] = tau_ref[h:h + 1, :] - s1
            cc_ref[h] = jnp.exp(s1 - m1_ref[h:h + 1, :]) * zinv_ref[h:h + 1, :]
            a2_ref[h] = jnp.exp(s2_ref[h] - m2_ref[h:h + 1, :])
        acc_ref[...] = jnp.zeros_like(acc_ref)

    st_ref[...] = jnp.dot(u_ref[...], ht_ref[...], preferred_element_type=F32)

    def first_key_row(e, carry):
        e1 = j * rows_per + e
        gate = jnp.zeros((N_KEYS, st_ref.shape[1]), F32)
        for h in range(PEER_HEADS):
            thr = th_ref[h, pl.ds(e1, 1), :]
            cr = cc_ref[h, pl.ds(e1, 1), :]
            gate = gate + jnp.where(s2_ref[h] >= thr, a2_ref[h] * cr, 0.0)
        rows = pl.ds(pl.multiple_of(e * N_KEYS, N_KEYS), N_KEYS)
        act_ref[rows, :] = (_gelu_tanh(st_ref[rows, :]) * gate).astype(BF16)
        return carry

    lax.fori_loop(0, rows_per, first_key_row, 0)

    acc_ref[...] += jnp.dot(vt_ref[...], act_ref[...], preferred_element_type=F32)

    @pl.when(j == ne - 1)
    def _():
        o_ref[...] = x_ref[...] + ga_ref[0] * acc_ref[...].T


def _peer_mix(ht, u, vt, s1t, s2t, stats, x2, ga, l):
    d, n = ht.shape
    n_exp = u.shape[0]
    t = min(512, l)
    per = l // t
    ne = n_exp // E_TILE
    sblk = pl.BlockSpec((PEER_HEADS, N_KEYS, t), lambda i, j: (0, 0, i))
    stblk = pl.BlockSpec((PEER_HEADS, t), lambda i, j: (0, i))
    return pl.pallas_call(
        functools.partial(_peer_mix_kernel, ne=ne),
        out_shape=jax.ShapeDtypeStruct((n, d), F32),
        grid=(n // t, ne),
        in_specs=[pl.BlockSpec((d, t), lambda i, j: (0, i)),
                  pl.BlockSpec((E_TILE, d), lambda i, j: (j, 0)),
                  pl.BlockSpec((d, E_TILE), lambda i, j: (0, j)),
                  sblk, sblk, stblk, stblk, stblk, stblk,
                  pl.BlockSpec((t, d), lambda i, j: (i, 0)),
                  pl.BlockSpec((1, 1, d), lambda i, j: (i // per, 0, 0))],
        out_specs=pl.BlockSpec((t, d), lambda i, j: (i, 0)),
        scratch_shapes=[pltpu.VMEM((PEER_HEADS, N_KEYS, t), F32), pltpu.VMEM((PEER_HEADS, N_KEYS, t), F32),
                        pltpu.VMEM((PEER_HEADS, N_KEYS, t), F32), pltpu.VMEM((d, t), F32),
                        pltpu.VMEM((E_TILE, t), BF16), pltpu.VMEM((E_TILE, t), F32)],
        compiler_params=_params("arbitrary", "arbitrary"),
        name="peer_mix",
    )(ht, u, vt, s1t, s2t, *stats, x2, ga)


def _final_norm_kernel(x_ref, g_ref, o_ref):
    x = x_ref[...]
    o_ref[...] = x * lax.rsqrt(jnp.mean(x * x, axis=-1, keepdims=True) + EPS) * g_ref[...]


def _final_norm(x2, g):
    n, d = x2.shape
    tm = 512
    return pl.pallas_call(
        _final_norm_kernel,
        out_shape=jax.ShapeDtypeStruct((n, d), F32),
        grid=(n // tm,),
        in_specs=[pl.BlockSpec((tm, d), lambda i: (i, 0)), pl.BlockSpec((1, d), lambda i: (0, 0))],
        out_specs=pl.BlockSpec((tm, d), lambda i: (i, 0)),
        compiler_params=_params("arbitrary"),
        name="final_norm",
    )(x2, g)


def _pad_lanes(a):
    return jnp.pad(a, ((0, 0), (0, LANES - a.shape[1])))


def _token_mix_inputs(p, proj_hy, l):
    u3 = _dwconv(proj_hy, p['hy_conv_w'], p['hy_conv_b'], act=False)
    fs, fd, nyq = _hy_filter(l, p['hf_w1'], p['hf_b1'], p['hf_w2'], p['hf_b2'], p['hf_w3'], p['hf_freq'])
    return _hy_conv(u3, fs, fd, nyq, p['hy_bias'])


def _peer(x, p, sh, sc, ga):
    b, l, d = x.shape
    x2 = x.reshape(b * l, d)
    ht, s1t, s2t = _peer_q(x2, p['g_norm2'], sh, sc, p['wq'], p['k1'], p['k2'], l)
    stats = _peer_stats(s1t, s2t)
    out = _peer_mix(ht, p['u'], p['vt'], s1t, s2t, stats, x2, ga, l)
    return out.reshape(b, l, d)


def _layer(xl, xc, mod_l, mod_c, p, ctx_out):
    b = xl.shape[0]
    d = xl.shape[2]
    sh1, sc1, ga1, sh2, sc2, ga2 = [m.reshape(b, 1, d) for m in jnp.split(mod_l, 6, axis=-1)]
    csh1, csc1, cga1, csh2, csc2, cga2 = [jnp.broadcast_to(m.reshape(1, 1, d), (b, 1, d))
                                          for m in jnp.split(mod_c, 6, axis=-1)]
    w_in = p['w_in']
    w_hy, w_z, w_xbc, w_fn = (w_in[:, OFF_HY:OFF_Z], w_in[:, OFF_Z:OFF_XBC], w_in[:, OFF_XBC:OFF_DT],
                              w_in[:, OFF_FN:D_IN_PROJ])
    w_dt = _pad_lanes(w_in[:, OFF_DT:OFF_FN])
    g1 = p['g_norm1']
    pl_hy, pl_z, pl_xbc, pl_dt, pl_fn = _in_proj(xl, g1, sh1, sc1, [w_hy, w_z, w_xbc, w_dt, w_fn])
    if ctx_out:
        pc_hy, pc_z, pc_xbc, pc_dt, pc_fn = _in_proj(xc, g1, csh1, csc1, [w_hy, w_z, w_xbc, w_dt, w_fn])
    else:
        pc_xbc, pc_dt = _in_proj(xc, g1, csh1, csc1, [w_xbc, w_dt])

    xbc_c = _dwconv(pc_xbc, p['ssd_conv_w'], p['ssd_conv_b'], act=True)
    xbc_l = _dwconv(pl_xbc, p['ssd_conv_w'], p['ssd_conv_b'], act=True)
    dtb, arow = p['dtb_row'], p['a_row']
    h0 = jnp.zeros((b, SSD_GROUPS, SSD_STATE, SSD_INNER // SSD_GROUPS), F32)
    yc_f, hc_f = _ssd_scan(xbc_c, pc_dt, dtb, arow, h0, 0)
    yc_b, hc_b = _ssd_scan(xbc_c, pc_dt, dtb, arow, h0, 1)
    yl_f, _ = _ssd_scan(xbc_l, pl_dt, dtb, arow, hc_f, 0)
    yl_b, _ = _ssd_scan(xbc_l, pl_dt, dtb, arow, hc_b, 1)

    def token_mix(x, ga, proj_hy, y_f, y_b, xbc, z, proj_fn):
        y_hy = _token_mix_inputs(p, proj_hy, x.shape[1])
        y_fn = _fnet(proj_fn)
        return _out_proj(x, ga, y_hy, y_f, y_b, xbc, z, y_fn, p['dsk_row'], p['ng_row'], p['wo1'], p['wo2'], p['wo3'])

    xl = token_mix(xl, ga1, pl_hy, yl_f, yl_b, xbc_l, pl_z, pl_fn)
    xl = _peer(xl, p, sh2, sc2, ga2)
    if ctx_out:
        xc = token_mix(xc, cga1, pc_hy, yc_f, yc_b, xbc_c, pc_z, pc_fn)
        xc = _peer(xc, p, csh2, csc2, cga2)
    return xl, xc


def kernel(x, c, ctx, c_ctx, w_ada, b_ada, g_norm1, g_norm2, w_in, hy_conv_w, hy_conv_b, hf_w1, hf_b1, hf_w2, hf_b2, hf_w3, hf_freq, hy_bias, ssd_conv_w, ssd_conv_b, ssd_dt_bias, ssd_a_log, ssd_d, ssd_norm_g, w_out, peer_wq, peer_k1, peer_k2, peer_u, peer_v, g_final):
    depth = w_ada.shape[0]
    b, l, d = x.shape
    rows = -(-(b + 1) // SUBLANES) * SUBLANES
    cc = jnp.concatenate([c, c_ctx[None, :], jnp.zeros((rows - b - 1, d), F32)], axis=0)
    mods = _ada_mod(cc, w_ada, b_ada)
    xl, xc = x, ctx
    for i in range(depth):
        wo = w_out[i].astype(BF16)
        p = {
            'g_norm1': g_norm1[i].reshape(1, d), 'g_norm2': g_norm2[i].reshape(1, d),
            'w_in': w_in[i].astype(BF16),
            'hy_conv_w': hy_conv_w[i], 'hy_conv_b': hy_conv_b[i],
            'hf_w1': hf_w1[i], 'hf_b1': hf_b1[i], 'hf_w2': hf_w2[i], 'hf_b2': hf_b2[i], 'hf_w3': hf_w3[i],
            'hf_freq': hf_freq[i], 'hy_bias': hy_bias[i],
            'ssd_conv_w': ssd_conv_w[i], 'ssd_conv_b': ssd_conv_b[i],
            'dtb_row': _pad_lanes(ssd_dt_bias[i].reshape(1, 2 * SSD_HEADS)),
            'a_row': _pad_lanes(-jnp.exp(ssd_a_log[i].astype(F32)).reshape(1, 2 * SSD_HEADS)),
            'dsk_row': jnp.repeat(ssd_d[i].astype(F32), SSD_HEAD_DIM).reshape(1, SSD_INNER),
            'ng_row': ssd_norm_g[i].reshape(1, SSD_INNER),
            'wo1': wo[:HY_WIDTH], 'wo2': wo[HY_WIDTH:HY_WIDTH + SSD_INNER], 'wo3': wo[HY_WIDTH + SSD_INNER:],
            'wq': peer_wq[i].astype(BF16), 'k1': peer_k1[i].astype(BF16), 'k2': peer_k2[i].astype(BF16),
            'u': peer_u[i].astype(BF16), 'vt': peer_v[i].astype(BF16).T,
        }
        xl, xc = _layer(xl, xc, mods[i, :b], mods[i, b], p, i < depth - 1)
    return _final_norm(xl.reshape(b * l, d), g_final.reshape(1, d)).reshape(b, l, d)
```

```python
import functools
import math

import jax
import jax.numpy as jnp
import numpy as np
from jax import lax
from jax.experimental import pallas as pl
from jax.experimental.pallas import tpu as pltpu

F32 = jnp.float32
BF16 = jnp.bfloat16
HI = lax.Precision.HIGHEST

EPS = 1e-6
LANES = 128
SUBLANES = 8
VMEM_LIMIT = 56 * 1024 * 1024

HY_WIDTH = 256
HY_EMB = 33
HY_BANDS = (HY_EMB - 1) // 2
HY_DECAY_TARGET = 1e-2
HY_SHORT_DECAY_PCT = 0.3
HY_LONG_DECAY_PCT = 1.5
SSD_HEADS = 8
SSD_HEAD_DIM = 64
SSD_INNER = SSD_HEADS * SSD_HEAD_DIM
SSD_GROUPS = 2
SSD_STATE = 128
SSD_CHUNK = 128
SSD_XBC = SSD_INNER + 2 * SSD_GROUPS * SSD_STATE
FN_WIDTH = 256
FN_GROUPS = 4
FN_GROUP_DIM = FN_WIDTH // FN_GROUPS
PEER_HEADS = 8
PEER_TOPK = 16
N_KEYS = 128
PEER_HALF = 128
OFF_HY = 0
OFF_Z = OFF_HY + 3 * HY_WIDTH
OFF_XBC = OFF_Z + SSD_INNER
OFF_DT = OFF_XBC + SSD_XBC
OFF_FN = OFF_DT + 2 * SSD_HEADS
D_IN_PROJ = OFF_FN + FN_WIDTH


def _params(*sem):
    return pltpu.CompilerParams(dimension_semantics=sem, vmem_limit_bytes=VMEM_LIMIT)


def _silu(x):
    return x * jax.nn.sigmoid(x)


def _softplus(x):
    return jnp.maximum(x, 0.0) + jnp.log1p(jnp.exp(-jnp.abs(x)))


def _gelu_tanh(x):
    return 0.5 * x * (1.0 + jnp.tanh(math.sqrt(2.0 / math.pi) * (x + 0.044715 * (x * x * x))))


def _ada_kernel(c_ref, w_ref, b_ref, o_ref):
    s = _silu(c_ref[...])
    o_ref[0] = jnp.dot(s, w_ref[0], preferred_element_type=F32, precision=HI) + b_ref[0]


def _ada_mod(cc, w_ada, b_ada):
    depth, d, n = w_ada.shape
    r = cc.shape[0]
    tn = 1536
    return pl.pallas_call(
        _ada_kernel,
        out_shape=jax.ShapeDtypeStruct((depth, r, n), F32),
        grid=(depth, n // tn),
        in_specs=[pl.BlockSpec((r, d), lambda l, j: (0, 0)),
                  pl.BlockSpec((1, d, tn), lambda l, j: (l, 0, j)),
                  pl.BlockSpec((1, 1, tn), lambda l, j: (l, 0, j))],
        out_specs=pl.BlockSpec((1, r, tn), lambda l, j: (l, 0, j)),
        compiler_params=_params("arbitrary", "arbitrary"),
        name="ada_mod",
    )(cc, w_ada, b_ada.reshape(depth, 1, n))


def _normmod(x, g, sh, sc):
    y = x * lax.rsqrt(jnp.mean(x * x, axis=-1, keepdims=True) + EPS) * g
    return y * (1.0 + sc) + sh


def _inproj_kernel(x_ref, g_ref, sh_ref, sc_ref, *rest, n_w):
    w_refs, o_refs = rest[:n_w], rest[n_w:]
    hb = _normmod(x_ref[0], g_ref[...], sh_ref[0], sc_ref[0]).astype(BF16)
    for w_ref, o_ref in zip(w_refs, o_refs):
        o_ref[0] = jnp.dot(hb, w_ref[...], preferred_element_type=F32).astype(o_ref.dtype)


def _in_proj(x, g, sh, sc, ws):
    b, l, d = x.shape
    tm = min(512, l)
    return pl.pallas_call(
        functools.partial(_inproj_kernel, n_w=len(ws)),
        out_shape=[jax.ShapeDtypeStruct((b, l, w.shape[1]), F32) for w in ws],
        grid=(b, l // tm),
        in_specs=[pl.BlockSpec((1, tm, d), lambda i, j: (i, j, 0)),
                  pl.BlockSpec((1, d), lambda i, j: (0, 0)),
                  pl.BlockSpec((1, 1, d), lambda i, j: (i, 0, 0)),
                  pl.BlockSpec((1, 1, d), lambda i, j: (i, 0, 0))]
        + [pl.BlockSpec(w.shape, lambda i, j: (0, 0)) for w in ws],
        out_specs=[pl.BlockSpec((1, tm, w.shape[1]), lambda i, j: (i, j, 0)) for w in ws],
        compiler_params=_params("arbitrary", "arbitrary"),
        name="in_proj",
    )(x, g, sh, sc, *ws)


def _dwconv_kernel(u_ref, w_ref, b_ref, o_ref, *, act):
    u = u_ref[0]
    l = u.shape[0]
    row = lax.broadcasted_iota(jnp.int32, u.shape, 0)
    um = jnp.where(row == 0, 0.0, pltpu.roll(u, 1, 0))
    up = jnp.where(row == l - 1, 0.0, pltpu.roll(u, l - 1, 0))
    y = um * w_ref[0:1, :] + u * w_ref[1:2, :] + up * w_ref[2:3, :] + b_ref[...]
    if act:
        y = _silu(y)
    o_ref[0] = y


def _dwconv(u, w, bias, act):
    b, l, c = u.shape
    cb = 256
    return pl.pallas_call(
        functools.partial(_dwconv_kernel, act=act),
        out_shape=jax.ShapeDtypeStruct((b, l, c), F32),
        grid=(b, c // cb),
        in_specs=[pl.BlockSpec((1, l, cb), lambda i, j: (i, 0, j)),
                  pl.BlockSpec((3, cb), lambda i, j: (0, j)),
                  pl.BlockSpec((1, cb), lambda i, j: (0, j))],
        out_specs=pl.BlockSpec((1, l, cb), lambda i, j: (i, 0, j)),
        compiler_params=_params("arbitrary", "arbitrary"),
        name="dwconv",
    )(u, w, bias.reshape(1, c))


def _ssd_kernel(xbc_ref, dt_ref, dtb_ref, arow_ref, h0_ref, y_ref, hl_ref, st_ref, *, direction, nc):
    q = SSD_CHUNK
    c = pl.program_id(1)

    @pl.when(c == 0)
    def _():
        st_ref[...] = h0_ref[0]

    xbc = xbc_ref[0]
    x = xbc[:, :SSD_INNER]
    gn = SSD_GROUPS * SSD_STATE
    bm = xbc[:, SSD_INNER:SSD_INNER + gn].astype(BF16)
    cm = xbc[:, SSD_INNER + gn:].astype(BF16)
    dtp = _softplus(dt_ref[0] + dtb_ref[...])
    da = dtp * arow_ref[...]
    ii = lax.broadcasted_iota(jnp.int32, (q, q), 0)
    jj = lax.broadcasted_iota(jnp.int32, (q, q), 1)
    tri = (ii >= jj) if direction == 0 else (ii <= jj)
    cs = jnp.dot(tri.astype(F32), da, preferred_element_type=F32, precision=HI)
    cs_t = cs.T
    lane = lax.broadcasted_iota(jnp.int32, (LANES, SSD_INNER), 0)
    chan = lax.broadcasted_iota(jnp.int32, (LANES, SSD_INNER), 1)
    head_of = lax.shift_right_logical(chan, int(math.log2(SSD_HEAD_DIM)))
    expand = (lane == direction * SSD_HEADS + head_of).astype(F32)
    acum = jnp.dot(cs, expand, preferred_element_type=F32, precision=HI)
    dtf = jnp.dot(dtp, expand, preferred_element_type=F32, precision=HI)
    tot = acum[q - 1:q, :] if direction == 0 else acum[0:1, :]
    xdt = x * dtf
    ea = jnp.exp(acum)
    xdec = (jnp.exp(tot - acum) * xdt).astype(BF16)
    cd = jnp.exp(tot)
    xdt_b = xdt.astype(BF16)
    hg = SSD_HEADS // SSD_GROUPS
    gw = hg * SSD_HEAD_DIM
    outs = []
    for g in range(SSD_GROUPS):
        bg = bm[:, g * SSD_STATE:(g + 1) * SSD_STATE]
        cg = cm[:, g * SSD_STATE:(g + 1) * SSD_STATE]
        cb = lax.dot_general(cg, bg, (((1,), (1,)), ((), ())), preferred_element_type=F32)
        st = st_ref[g]
        y_off = jnp.dot(cg, st.astype(BF16), preferred_element_type=F32) * ea[:, g * gw:(g + 1) * gw]
        xg = xdt_b[:, g * gw:(g + 1) * gw]
        head_g = lax.shift_right_logical(lax.broadcasted_iota(jnp.int32, xg.shape, 1), int(math.log2(SSD_HEAD_DIM)))
        y_g = y_off
        for k in range(hg):
            ln = direction * SSD_HEADS + g * hg + k
            seg = cs[:, ln:ln + 1] - cs_t[ln:ln + 1, :]
            lmat = jnp.where(tri, jnp.exp(jnp.where(tri, seg, 0.0)), 0.0)
            m = (cb * lmat).astype(BF16)
            y_g = y_g + jnp.dot(m, jnp.where(head_g == k, xg, jnp.zeros_like(xg)), preferred_element_type=F32)
        outs.append(y_g)
        sg = lax.dot_general(bg, xdec[:, g * gw:(g + 1) * gw], (((0,), (0,)), ((), ())),
                             preferred_element_type=F32)
        st_ref[g] = st * cd[:, g * gw:(g + 1) * gw] + sg
    y_ref[0] = jnp.concatenate(outs, axis=1)

    @pl.when(c == nc - 1)
    def _():
        hl_ref[0] = st_ref[...]


def _ssd_scan(xbc, dt, dtb_row, a_row, h0, direction):
    b, l, _ = xbc.shape
    nc = l // SSD_CHUNK
    cidx = (lambda i, c: (i, c, 0)) if direction == 0 else (lambda i, c: (i, nc - 1 - c, 0))
    st_shape = (SSD_GROUPS, SSD_STATE, SSD_INNER // SSD_GROUPS)
    return pl.pallas_call(
        functools.partial(_ssd_kernel, direction=direction, nc=nc),
        out_shape=[jax.ShapeDtypeStruct((b, l, SSD_INNER), F32),
                   jax.ShapeDtypeStruct((b,) + st_shape, F32)],
        grid=(b, nc),
        in_specs=[pl.BlockSpec((1, SSD_CHUNK, SSD_XBC), cidx),
                  pl.BlockSpec((1, SSD_CHUNK, LANES), cidx),
                  pl.BlockSpec((1, LANES), lambda i, c: (0, 0)),
                  pl.BlockSpec((1, LANES), lambda i, c: (0, 0)),
                  pl.BlockSpec((1,) + st_shape, lambda i, c: (i, 0, 0, 0))],
        out_specs=[pl.BlockSpec((1, SSD_CHUNK, SSD_INNER), cidx),
                   pl.BlockSpec((1,) + st_shape, lambda i, c: (i, 0, 0, 0))],
        scratch_shapes=[pltpu.VMEM(st_shape, F32)],
        compiler_params=_params("arbitrary", "arbitrary"),
        name=f"ssd_scan_d{direction}",
    )(xbc, dt, dtb_row, a_row, h0)


def _hy_filter_kernel(z_ref, w1_ref, b1_ref, w2_ref, b2_ref, w3_ref, fr_ref, win_ref, alt_ref,
                      fs_ref, fd_ref, nyq_ref):
    fr = fr_ref[...]
    h = jnp.sin(fr * (jnp.dot(z_ref[...], w1_ref[...], preferred_element_type=F32, precision=HI) + b1_ref[...]))
    h = jnp.sin(fr * (jnp.dot(h, w2_ref[...], preferred_element_type=F32, precision=HI) + b2_ref[...]))
    h = jnp.dot(h, w3_ref[...], preferred_element_type=F32, precision=HI)
    win = win_ref[...]
    row = lax.broadcasted_iota(jnp.int32, win.shape, 0)
    hf = h[:, :HY_WIDTH] * win
    hb = jnp.where(row == 0, 0.0, h[:, HY_WIDTH:] * win)
    nrm = lax.rsqrt(jnp.sum(hf * hf, axis=0, keepdims=True) + jnp.sum(hb * hb, axis=0, keepdims=True) + EPS)
    fs = (hf + hb) * nrm
    fs_ref[...] = fs
    fd_ref[...] = (hb - hf) * nrm
    nyq_ref[...] = jnp.sum(fs * alt_ref[...], axis=0, keepdims=True)


def _hy_filter(l, w1, b1, w2, b2, w3, freq):
    t = np.linspace(0.0, 1.0, l, dtype=np.float32)[:, None]
    w = (np.float32(2.0 * math.pi) * np.arange(l, dtype=np.float32)[:, None] / np.float32(l)).astype(np.float32)
    f = np.linspace(1e-4, HY_BANDS - 1, HY_BANDS, dtype=np.float32)[None, :]
    zf = jnp.asarray(f) * jnp.asarray(w)
    z = jnp.concatenate([jnp.asarray(t), jnp.cos(zf), -jnp.sin(zf)], axis=-1)
    z = jnp.pad(z, ((0, 0), (0, LANES - HY_EMB)))
    hid = w1.shape[1]
    w1p = jnp.pad(w1, ((0, LANES - HY_EMB), (0, LANES - hid)))
    w2p = jnp.pad(w2, ((0, LANES - hid), (0, LANES - hid)))
    w3p = jnp.pad(w3, ((0, LANES - hid), (0, 0)))
    pad_row = lambda v: jnp.pad(v.reshape(1, hid), ((0, 0), (0, LANES - hid)))
    max_decay = math.log(HY_DECAY_TARGET) / HY_SHORT_DECAY_PCT
    min_decay = math.log(HY_DECAY_TARGET) / HY_LONG_DECAY_PCT
    deltas = jnp.abs(jnp.linspace(min_decay, max_decay, HY_WIDTH, dtype=F32))
    win = jnp.exp(-jnp.asarray(t) * deltas)
    alt = jnp.asarray(np.where(np.arange(l) % 2 == 0, 1.0, -1.0).astype(np.float32)[:, None] * np.ones((1, HY_WIDTH), np.float32))
    return pl.pallas_call(
        _hy_filter_kernel,
        out_shape=[jax.ShapeDtypeStruct((l, HY_WIDTH), F32), jax.ShapeDtypeStruct((l, HY_WIDTH), F32),
                   jax.ShapeDtypeStruct((1, HY_WIDTH), F32)],
        compiler_params=pltpu.CompilerParams(vmem_limit_bytes=VMEM_LIMIT),
        name="hy_filter",
    )(z, w1p, pad_row(b1), w2p, pad_row(b2), w3p, pad_row(freq), win, alt)


def _mm_kernel(a_ref, b_ref, o_ref):
    o_ref[...] = jnp.dot(a_ref[...], b_ref[...], preferred_element_type=F32)


def _mm(a, b, tm):
    m, k = a.shape
    n = b.shape[1]
    return pl.pallas_call(
        _mm_kernel,
        out_shape=jax.ShapeDtypeStruct((m, n), F32),
        grid=(m // tm,),
        in_specs=[pl.BlockSpec((tm, k), lambda i: (i, 0)), pl.BlockSpec((k, n), lambda i: (0, 0))],
        out_specs=pl.BlockSpec((tm, n), lambda i: (i, 0)),
        compiler_params=_params("arbitrary"),
        name="mm",
    )(a, b)


def _dft_tables(l):
    n = 2 * l
    k = np.arange(l, dtype=np.int64)[:, None]
    s = np.arange(l, dtype=np.int64)[None, :]
    ang = (2.0 * np.pi / n) * ((k * s) % n).astype(np.float64)
    cos, sin = np.cos(ang), np.sin(ang)
    alt = np.where(np.arange(l) % 2 == 0, 1.0, -1.0)
    sin[0, :] = alt
    fwd = np.concatenate([cos, sin], axis=0)
    wgt = np.full((l, 1), 2.0)
    wgt[0, 0] = 1.0
    sin_i = -sin * wgt
    sin_i[0, :] = alt
    inv = np.concatenate([(cos * wgt).T, sin_i.T], axis=1) / n
    return fwd, inv


def _hy_conv_kernel(u_ref, ff_ref, fi_ref, ka_ref, kb_ref, kc_ref, bias_ref, o_ref, ub_ref, acc_ref, *, nf):
    f = pl.program_id(1)
    w = HY_WIDTH

    @pl.when(f == 0)
    def _():
        u3 = u_ref[0]
        ub_ref[...] = (u3[:, 2 * w:] * u3[:, :w]).astype(BF16)
        acc_ref[...] = jnp.zeros_like(acc_ref)

    uf = jnp.dot(ff_ref[0], ub_ref[...], preferred_element_type=F32)
    fb = uf.shape[0] // 2
    ur, us = uf[:fb], uf[fb:]
    ka, kb, kc = ka_ref[...], kb_ref[...], kc_ref[...]
    p = jnp.concatenate([ur * ka + us * kb, ur * kb - us * kc], axis=0).astype(BF16)
    acc_ref[...] += jnp.dot(fi_ref[0], p, preferred_element_type=F32)

    @pl.when(f == nf - 1)
    def _():
        u3 = u_ref[0]
        o_ref[0] = u3[:, w:2 * w] * (acc_ref[...] + u3[:, 2 * w:] * u3[:, :w] * bias_ref[...])


def _hy_conv(u3, fs, fd, nyq, bias):
    b, l, _ = u3.shape
    fwd, inv = _dft_tables(l)
    fb = min(512, l)
    nf = l // fb
    ff = jnp.asarray(np.stack([np.concatenate([fwd[i * fb:(i + 1) * fb], fwd[l + i * fb:l + (i + 1) * fb]], axis=0)
                               for i in range(nf)]), dtype=BF16)
    fi = jnp.asarray(np.stack([np.concatenate([inv[:, i * fb:(i + 1) * fb], inv[:, l + i * fb:l + (i + 1) * fb]], axis=1)
                               for i in range(nf)]), dtype=BF16)
    fwd_b = jnp.asarray(fwd, dtype=BF16)
    tm = min(512, l)
    k_r = _mm(fwd_b[:l], fs.astype(BF16), tm)
    k_i = _mm(fwd_b[l:], fd.astype(BF16), tm)
    first = (jnp.arange(l) == 0)[:, None]
    ka = k_r
    kb = jnp.where(first, 0.0, k_i)
    kc = jnp.where(first, -nyq, k_r)
    return pl.pallas_call(
        functools.partial(_hy_conv_kernel, nf=nf),
        out_shape=jax.ShapeDtypeStruct((b, l, HY_WIDTH), F32),
        grid=(b, nf),
        in_specs=[pl.BlockSpec((1, l, 3 * HY_WIDTH), lambda i, f: (i, 0, 0)),
                  pl.BlockSpec((1, 2 * fb, l), lambda i, f: (f, 0, 0)),
                  pl.BlockSpec((1, l, 2 * fb), lambda i, f: (f, 0, 0)),
                  pl.BlockSpec((fb, HY_WIDTH), lambda i, f: (f, 0)),
                  pl.BlockSpec((fb, HY_WIDTH), lambda i, f: (f, 0)),
                  pl.BlockSpec((fb, HY_WIDTH), lambda i, f: (f, 0)),
                  pl.BlockSpec((1, HY_WIDTH), lambda i, f: (0, 0))],
        out_specs=pl.BlockSpec((1, l, HY_WIDTH), lambda i, f: (i, 0, 0)),
        scratch_shapes=[pltpu.VMEM((l, HY_WIDTH), BF16), pltpu.VMEM((l, HY_WIDTH), F32)],
        compiler_params=_params("arbitrary", "arbitrary"),
        name="hy_conv",
    )(u3, ff, fi, ka, kb, kc, bias.reshape(1, HY_WIDTH))


def _fnet_kernel(x_ref, cc_ref, sc_ref, m_ref, o_ref, xx_ref):
    r = pl.program_id(1)
    l = x_ref.shape[1]

    @pl.when(r == 0)
    def _():
        xb = x_ref[0].astype(BF16)
        xx_ref[:l, :] = jnp.dot(xb, cc_ref[...], preferred_element_type=F32).astype(BF16)
        xx_ref[l:, :] = jnp.dot(xb, sc_ref[...], preferred_element_type=F32).astype(BF16)

    o_ref[0] = jnp.dot(m_ref[...], xx_ref[...], preferred_element_type=F32)


def _fnet(x):
    b, l, c = x.shape
    gd = FN_GROUP_DIM
    kc = np.arange(gd)[:, None] * np.arange(gd)[None, :]
    ang_c = 2.0 * np.pi * (kc % gd) / gd
    eye = np.eye(FN_GROUPS)
    cblk = jnp.asarray(np.kron(eye, np.cos(ang_c)), dtype=BF16)
    sblk = jnp.asarray(np.kron(eye, np.sin(ang_c)), dtype=BF16)
    kl = (np.arange(l, dtype=np.int64)[:, None] * np.arange(l, dtype=np.int64)[None, :]) % l
    ang_l = 2.0 * np.pi * kl / l
    scale = 1.0 / math.sqrt(l * gd)
    mat = jnp.asarray(np.concatenate([np.cos(ang_l), -np.sin(ang_l)], axis=1) * scale, dtype=BF16)
    tr = min(512, l)
    return pl.pallas_call(
        _fnet_kernel,
        out_shape=jax.ShapeDtypeStruct((b, l, c), F32),
        grid=(b, l // tr),
        in_specs=[pl.BlockSpec((1, l, c), lambda i, r: (i, 0, 0)),
                  pl.BlockSpec((c, c), lambda i, r: (0, 0)),
                  pl.BlockSpec((c, c), lambda i, r: (0, 0)),
                  pl.BlockSpec((tr, 2 * l), lambda i, r: (r, 0))],
        out_specs=pl.BlockSpec((1, tr, c), lambda i, r: (i, r, 0)),
        scratch_shapes=[pltpu.VMEM((2 * l, c), BF16)],
        compiler_params=_params("arbitrary", "arbitrary"),
        name="fnet",
    )(x, cblk, sblk, mat)


def _outproj_kernel(x_ref, ga_ref, yhy_ref, yf_ref, yb_ref, xs_ref, z_ref, yfn_ref, dsk_ref, ng_ref,
                    w1_ref, w2_ref, w3_ref, o_ref):
    y = yf_ref[0] + yb_ref[0] + dsk_ref[...] * xs_ref[0]
    y = y * _silu(z_ref[0])
    gw = SSD_INNER // SSD_GROUPS
    parts = []
    for g in range(SSD_GROUPS):
        yg = y[:, g * gw:(g + 1) * gw]
        parts.append(yg * lax.rsqrt(jnp.mean(yg * yg, axis=-1, keepdims=True) + EPS))
    yn = jnp.concatenate(parts, axis=1) * ng_ref[...]
    out = jnp.dot(yhy_ref[0].astype(BF16), w1_ref[...], preferred_element_type=F32)
    out += jnp.dot(yn.astype(BF16), w2_ref[...], preferred_element_type=F32)
    out += jnp.dot(yfn_ref[0].astype(BF16), w3_ref[...], preferred_element_type=F32)
    o_ref[0] = x_ref[0] + ga_ref[0] * out


def _out_proj(x, ga, y_hy, y_f, y_b, xbc, z, y_fn, dsk_row, ng_row, w1, w2, w3):
    b, l, d = x.shape
    tm = min(512, l)
    tok = lambda n: pl.BlockSpec((1, tm, n), lambda i, j: (i, j, 0))
    const = lambda a: pl.BlockSpec(a.shape, lambda i, j: (0,) * a.ndim)
    return pl.pallas_call(
        _outproj_kernel,
        out_shape=jax.ShapeDtypeStruct((b, l, d), F32),
        grid=(b, l // tm),
        in_specs=[tok(d), pl.BlockSpec((1, 1, d), lambda i, j: (i, 0, 0)), tok(HY_WIDTH), tok(SSD_INNER),
                  tok(SSD_INNER), tok(SSD_INNER), tok(SSD_INNER), tok(FN_WIDTH), const(dsk_row), const(ng_row),
                  const(w1), const(w2), const(w3)],
        out_specs=tok(d),
        compiler_params=_params("arbitrary", "arbitrary"),
        name="out_proj",
    )(x, ga, y_hy, y_f, y_b, xbc, z, y_fn, dsk_row, ng_row, w1, w2, w3)


def _peer_q_kernel(x_ref, g_ref, sh_ref, sc_ref, wq_ref, k1_ref, k2_ref, ht_ref, s1_ref, s2_ref):
    hm = _normmod(x_ref[...], g_ref[...], sh_ref[0], sc_ref[0])
    ht_ref[...] = hm.T.astype(BF16)
    qb = jnp.dot(hm.astype(BF16), wq_ref[...], preferred_element_type=F32).astype(BF16)
    nt = (((1,), (1,)), ((), ()))
    for h in range(PEER_HEADS):
        q1 = qb[:, (2 * h) * PEER_HALF:(2 * h + 1) * PEER_HALF]
        q2 = qb[:, (2 * h + 1) * PEER_HALF:(2 * h + 2) * PEER_HALF]
        s1_ref[h] = lax.dot_general(k1_ref[...], q1, nt, preferred_element_type=F32)
        s2_ref[h] = lax.dot_general(k2_ref[...], q2, nt, preferred_element_type=F32)


def _peer_q(x2, g, sh, sc, wq, k1, k2, l):
    n, d = x2.shape
    tm = min(256, l)
    per = l // tm
    return pl.pallas_call(
        _peer_q_kernel,
        out_shape=[jax.ShapeDtypeStruct((d, n), BF16),
                   jax.ShapeDtypeStruct((PEER_HEADS, N_KEYS, n), F32),
                   jax.ShapeDtypeStruct((PEER_HEADS, N_KEYS, n), F32)],
        grid=(n // tm,),
        in_specs=[pl.BlockSpec((tm, d), lambda i: (i, 0)),
                  pl.BlockSpec((1, d), lambda i: (0, 0)),
                  pl.BlockSpec((1, 1, d), lambda i: (i // per, 0, 0)),
                  pl.BlockSpec((1, 1, d), lambda i: (i // per, 0, 0)),
                  pl.BlockSpec(wq.shape, lambda i: (0, 0)),
                  pl.BlockSpec(k1.shape, lambda i: (0, 0)),
                  pl.BlockSpec(k2.shape, lambda i: (0, 0))],
        out_specs=[pl.BlockSpec((d, tm), lambda i: (0, i)),
                   pl.BlockSpec((PEER_HEADS, N_KEYS, tm), lambda i: (0, 0, i)),
                   pl.BlockSpec((PEER_HEADS, N_KEYS, tm), lambda i: (0, 0, i))],
        compiler_params=_params("arbitrary"),
        name="peer_q",
    )(x2, g, sh, sc, wq, k1, k2)


def _sorting_network(n):
    pairs = []

    def merge(lo, m, r):
        step = 2 * r
        if step < m:
            merge(lo, m, step)
            merge(lo + r, m, step)
            for i in range(lo + r, lo + m - r, step):
                pairs.append((i, i + r))
        else:
            pairs.append((lo, lo + r))

    def sort(lo, m):
        if m > 1:
            half = m // 2
            sort(lo, half)
            sort(lo + half, half)
            merge(lo, m, 1)

    sort(0, n)
    return tuple(pairs)


NEG = -3.0e38
N_TOP = PEER_TOPK + 1


def _top_sorted(rows, n_out):
    rows = list(rows)
    for i, j in _sorting_network(len(rows)):
        rows[i], rows[j] = jnp.maximum(rows[i], rows[j]), jnp.minimum(rows[i], rows[j])
    sub = lax.broadcasted_iota(jnp.int32, rows[0].shape, 0)
    out = []
    for r in range(n_out):
        m = jnp.max(rows[0], axis=0, keepdims=True)
        out.append(m)
        first = jnp.min(jnp.where(rows[0] == m, sub, SUBLANES), axis=0, keepdims=True)
        hit = sub == first
        keep = min(len(rows), n_out - r - 1)
        rows = [jnp.where(hit, rows[k + 1] if k + 1 < len(rows) else NEG, rows[k]) for k in range(keep)]
    return out


def _count_above(vals, x, strict):
    r = jnp.zeros_like(x)
    for j, v in enumerate(vals):
        r = jnp.where((v > x) if strict else (v >= x), float(j + 1), r)
    return r


def _dup_bf16(x):
    hi = pltpu.bitcast(x.astype(BF16).astype(F32), jnp.uint32)
    return hi | (hi >> 16)


def _peer_stats_kernel(s1_ref, s2_ref, r2_ref, a2_ref, ns_ref, cc_ref):
    nrow = N_KEYS // SUBLANES
    t = s1_ref.shape[2]
    sub = lax.broadcasted_iota(jnp.int32, (SUBLANES, t), 0)
    for h in range(PEER_HEADS):
        s1_rows = [s1_ref[h, SUBLANES * k:SUBLANES * (k + 1), :] for k in range(nrow)]
        s2_rows = [s2_ref[h, SUBLANES * k:SUBLANES * (k + 1), :] for k in range(nrow)]
        v1 = _top_sorted(s1_rows, N_TOP)
        v2 = _top_sorted(s2_rows, N_TOP)
        cands = [v1[i] + v2[j] for i in range(N_TOP) for j in range(N_TOP) if (i + 1) * (j + 1) <= N_TOP]
        packed = []
        for k in range(0, len(cands), SUBLANES):
            blk = jnp.full((SUBLANES, t), NEG, F32)
            for s, cv in enumerate(cands[k:k + SUBLANES]):
                blk = jnp.where(sub == s, cv, blk)
            packed.append(blk)
        ids = [sub + SUBLANES * k for k in range(len(packed))]
        big = SUBLANES * len(packed)
        top = []
        for r in range(N_TOP):
            m = packed[0]
            for blk in packed[1:]:
                m = jnp.maximum(m, blk)
            m = jnp.max(m, axis=0, keepdims=True)
            top.append(m)
            sel = jnp.where(packed[0] == m, ids[0], big)
            for blk, idk in zip(packed[1:], ids[1:]):
                sel = jnp.minimum(sel, jnp.where(blk == m, idk, big))
            sel = jnp.min(sel, axis=0, keepdims=True)
            packed = [jnp.where(idk == sel, NEG, blk) for blk, idk in zip(packed, ids)]
        mx = top[0]
        z = jnp.zeros((1, t), F32)
        for r in range(PEER_TOPK):
            z = z + jnp.exp(top[r] - mx)
        tau = 0.5 * (top[PEER_TOPK - 1] + top[PEER_TOPK])
        cscale = 0.5 / z
        for k in range(0, nrow, 2):
            pair = slice(SUBLANES * k, SUBLANES * (k + 2))
            r2 = [_count_above(v2[:PEER_TOPK], s2_rows[k + q], True) for q in range(2)]
            r2_ref[h, pair, :] = jnp.concatenate(r2, axis=0).astype(BF16)
            a2 = [jnp.exp(s2_rows[k + q] - v2[0]) for q in range(2)]
            a2_ref[h, pair, :] = jnp.concatenate(a2, axis=0).astype(BF16)
        for k in range(nrow):
            blk = slice(SUBLANES * k, SUBLANES * (k + 1))
            ns_ref[h, blk, :] = _dup_bf16(_count_above(v2[:PEER_TOPK], tau - s1_rows[k], False))
            cc_ref[h, blk, :] = _dup_bf16(jnp.exp(s1_rows[k] - v1[0]) * cscale)


def _peer_stats(s1t, s2t):
    n = s1t.shape[2]
    t = LANES
    blk = pl.BlockSpec((PEER_HEADS, N_KEYS, t), lambda i: (0, 0, i))
    shp = (PEER_HEADS, N_KEYS, n)
    return pl.pallas_call(
        _peer_stats_kernel,
        out_shape=[jax.ShapeDtypeStruct(shp, BF16), jax.ShapeDtypeStruct(shp, BF16),
                   jax.ShapeDtypeStruct(shp, jnp.uint32), jax.ShapeDtypeStruct(shp, jnp.uint32)],
        grid=(n // t,),
        in_specs=[blk, blk],
        out_specs=[blk] * 4,
        compiler_params=_params("arbitrary"),
        name="peer_stats",
    )(s1t, s2t)


E_TILE = 1024
E_BLOCK = 2 * E_TILE


def _gelu_tanh_x2(x):
    return x * (1.0 + jnp.tanh(math.sqrt(2.0 / math.pi) * (x + 0.044715 * (x * x * x))))


ROWS_PER_TILE = E_TILE // N_KEYS


def _rows_bf16(word_row):
    tile = pltpu.bitcast(jnp.broadcast_to(word_row, (SUBLANES, word_row.shape[1])), BF16)
    return jnp.concatenate([tile] * (N_KEYS // tile.shape[0]), axis=0)


def _peer_mix_kernel(ht_ref, u_ref, vt_ref, r2_ref, a2_ref, nsp_ref, ccp_ref, nsc_ref, ccc_ref, x_ref, ga_ref,
                     o_ref, st0_ref, st1_ref, act0_ref, act1_ref, acc_ref, *, nblk):
    m = pl.program_id(1)

    def scores(half, st_ref):
        st_ref[...] = jnp.dot(u_ref[half * E_TILE:(half + 1) * E_TILE, :], ht_ref[...],
                              preferred_element_type=F32).astype(BF16)

    def gate(st_ref, act_ref, ns_ref, cc_ref):
        for e in range(ROWS_PER_TILE):
            g = None
            for h in range(PEER_HEADS):
                nb = _rows_bf16(ns_ref[h, e:e + 1, :])
                cb = _rows_bf16(cc_ref[h, e:e + 1, :])
                a2 = a2_ref[h]
                term = jnp.where(r2_ref[h] < nb, a2, jnp.zeros_like(a2)) * cb
                g = term if g is None else g + term
            rows = slice(e * N_KEYS, (e + 1) * N_KEYS)
            act_ref[rows, :] = _gelu_tanh_x2(st_ref[rows, :]) * g

    def accumulate(half, act_ref):
        acc_ref[...] += jnp.dot(vt_ref[:, half * E_TILE:(half + 1) * E_TILE], act_ref[...],
                                preferred_element_type=F32)

    @pl.when(m == 0)
    def _():
        acc_ref[...] = jnp.zeros_like(acc_ref)
        scores(0, st0_ref)
        gate(st0_ref, act0_ref, nsc_ref, ccc_ref)
        scores(1, st1_ref)

    @pl.when((m > 0) & (m < nblk))
    def _():
        accumulate(0, act0_ref)
        gate(st1_ref, act1_ref, nsp_ref, ccp_ref)
        scores(0, st0_ref)
        accumulate(1, act1_ref)
        gate(st0_ref, act0_ref, nsc_ref, ccc_ref)
        scores(1, st1_ref)

    @pl.when(m == nblk)
    def _():
        accumulate(0, act0_ref)
        gate(st1_ref, act1_ref, nsp_ref, ccp_ref)
        accumulate(1, act1_ref)
        o_ref[...] = x_ref[...] + ga_ref[0] * acc_ref[...].T


def _peer_mix(ht, u, vt, r2, a2, ns, cc, x2, ga, l):
    d, n = ht.shape
    n_exp = u.shape[0]
    t = min(512, l)
    per = l // t
    nblk = n_exp // E_BLOCK
    n_tiles = n_exp // E_TILE
    sblk = pl.BlockSpec((PEER_HEADS, N_KEYS, t), lambda i, m: (0, 0, i))
    prev = pl.BlockSpec((PEER_HEADS, ROWS_PER_TILE, t), lambda i, m: (0, jnp.maximum(2 * m - 1, 0), i))
    cur = pl.BlockSpec((PEER_HEADS, ROWS_PER_TILE, t), lambda i, m: (0, jnp.minimum(2 * m, n_tiles - 1), i))
    return pl.pallas_call(
        functools.partial(_peer_mix_kernel, nblk=nblk),
        out_shape=jax.ShapeDtypeStruct((n, d), F32),
        grid=(n // t, nblk + 1),
        in_specs=[pl.BlockSpec((d, t), lambda i, m: (0, i)),
                  pl.BlockSpec((E_BLOCK, d), lambda i, m: (jnp.minimum(m, nblk - 1), 0)),
                  pl.BlockSpec((d, E_BLOCK), lambda i, m: (0, jnp.maximum(m - 1, 0))),
                  sblk, sblk, prev, prev, cur, cur,
                  pl.BlockSpec((t, d), lambda i, m: (i, 0)),
                  pl.BlockSpec((1, 1, d), lambda i, m: (i // per, 0, 0))],
        out_specs=pl.BlockSpec((t, d), lambda i, m: (i, 0)),
        scratch_shapes=[pltpu.VMEM((E_TILE, t), BF16), pltpu.VMEM((E_TILE, t), BF16),
                        pltpu.VMEM((E_TILE, t), BF16), pltpu.VMEM((E_TILE, t), BF16),
                        pltpu.VMEM((d, t), F32)],
        compiler_params=_params("arbitrary", "arbitrary"),
        name="peer_mix",
    )(ht, u, vt, r2, a2, ns, cc, ns, cc, x2, ga)


def _final_norm_kernel(x_ref, g_ref, o_ref):
    x = x_ref[...]
    o_ref[...] = x * lax.rsqrt(jnp.mean(x * x, axis=-1, keepdims=True) + EPS) * g_ref[...]


def _final_norm(x2, g):
    n, d = x2.shape
    tm = 512
    return pl.pallas_call(
        _final_norm_kernel,
        out_shape=jax.ShapeDtypeStruct((n, d), F32),
        grid=(n // tm,),
        in_specs=[pl.BlockSpec((tm, d), lambda i: (i, 0)), pl.BlockSpec((1, d), lambda i: (0, 0))],
        out_specs=pl.BlockSpec((tm, d), lambda i: (i, 0)),
        compiler_params=_params("arbitrary"),
        name="final_norm",
    )(x2, g)


def _pad_lanes(a):
    return jnp.pad(a, ((0, 0), (0, LANES - a.shape[1])))


def _token_mix_inputs(p, proj_hy, l):
    u3 = _dwconv(proj_hy, p['hy_conv_w'], p['hy_conv_b'], act=False)
    fs, fd, nyq = _hy_filter(l, p['hf_w1'], p['hf_b1'], p['hf_w2'], p['hf_b2'], p['hf_w3'], p['hf_freq'])
    return _hy_conv(u3, fs, fd, nyq, p['hy_bias'])


def _peer(x, p, sh, sc, ga):
    b, l, d = x.shape
    x2 = x.reshape(b * l, d)
    ht, s1t, s2t = _peer_q(x2, p['g_norm2'], sh, sc, p['wq'], p['k1'], p['k2'], l)
    r2, a2, ns, cc = _peer_stats(s1t, s2t)
    out = _peer_mix(ht, p['u'], p['vt'], r2, a2, ns, cc, x2, ga, l)
    return out.reshape(b, l, d)


def _layer(xl, xc, mod_l, mod_c, p, ctx_out):
    b = xl.shape[0]
    d = xl.shape[2]
    sh1, sc1, ga1, sh2, sc2, ga2 = [m.reshape(b, 1, d) for m in jnp.split(mod_l, 6, axis=-1)]
    csh1, csc1, cga1, csh2, csc2, cga2 = [jnp.broadcast_to(m.reshape(1, 1, d), (b, 1, d))
                                          for m in jnp.split(mod_c, 6, axis=-1)]
    w_in = p['w_in']
    w_hy, w_z, w_xbc, w_fn = (w_in[:, OFF_HY:OFF_Z], w_in[:, OFF_Z:OFF_XBC], w_in[:, OFF_XBC:OFF_DT],
                              w_in[:, OFF_FN:D_IN_PROJ])
    w_dt = _pad_lanes(w_in[:, OFF_DT:OFF_FN])
    g1 = p['g_norm1']
    pl_hy, pl_z, pl_xbc, pl_dt, pl_fn = _in_proj(xl, g1, sh1, sc1, [w_hy, w_z, w_xbc, w_dt, w_fn])
    if ctx_out:
        pc_hy, pc_z, pc_xbc, pc_dt, pc_fn = _in_proj(xc, g1, csh1, csc1, [w_hy, w_z, w_xbc, w_dt, w_fn])
    else:
        pc_xbc, pc_dt = _in_proj(xc, g1, csh1, csc1, [w_xbc, w_dt])

    xbc_c = _dwconv(pc_xbc, p['ssd_conv_w'], p['ssd_conv_b'], act=True)
    xbc_l = _dwconv(pl_xbc, p['ssd_conv_w'], p['ssd_conv_b'], act=True)
    dtb, arow = p['dtb_row'], p['a_row']
    h0 = jnp.zeros((b, SSD_GROUPS, SSD_STATE, SSD_INNER // SSD_GROUPS), F32)
    yc_f, hc_f = _ssd_scan(xbc_c, pc_dt, dtb, arow, h0, 0)
    yc_b, hc_b = _ssd_scan(xbc_c, pc_dt, dtb, arow, h0, 1)
    yl_f, _ = _ssd_scan(xbc_l, pl_dt, dtb, arow, hc_f, 0)
    yl_b, _ = _ssd_scan(xbc_l, pl_dt, dtb, arow, hc_b, 1)

    def token_mix(x, ga, proj_hy, y_f, y_b, xbc, z, proj_fn):
        y_hy = _token_mix_inputs(p, proj_hy, x.shape[1])
        y_fn = _fnet(proj_fn)
        return _out_proj(x, ga, y_hy, y_f, y_b, xbc, z, y_fn, p['dsk_row'], p['ng_row'], p['wo1'], p['wo2'], p['wo3'])

    xl = token_mix(xl, ga1, pl_hy, yl_f, yl_b, xbc_l, pl_z, pl_fn)
    xl = _peer(xl, p, sh2, sc2, ga2)
    if ctx_out:
        xc = token_mix(xc, cga1, pc_hy, yc_f, yc_b, xbc_c, pc_z, pc_fn)
        xc = _peer(xc, p, csh2, csc2, cga2)
    return xl, xc


def kernel(x, c, ctx, c_ctx, w_ada, b_ada, g_norm1, g_norm2, w_in, hy_conv_w, hy_conv_b, hf_w1, hf_b1, hf_w2, hf_b2, hf_w3, hf_freq, hy_bias, ssd_conv_w, ssd_conv_b, ssd_dt_bias, ssd_a_log, ssd_d, ssd_norm_g, w_out, peer_wq, peer_k1, peer_k2, peer_u, peer_v, g_final):
    depth = w_ada.shape[0]
    b, l, d = x.shape
    rows = -(-(b + 1) // SUBLANES) * SUBLANES
    cc = jnp.concatenate([c, c_ctx[None, :], jnp.zeros((rows - b - 1, d), F32)], axis=0)
    mods = _ada_mod(cc, w_ada, b_ada)
    xl, xc = x, ctx
    for i in range(depth):
        wo = w_out[i].astype(BF16)
        p = {
            'g_norm1': g_norm1[i].reshape(1, d), 'g_norm2': g_norm2[i].reshape(1, d),
            'w_in': w_in[i].astype(BF16),
            'hy_conv_w': hy_conv_w[i], 'hy_conv_b': hy_conv_b[i],
            'hf_w1': hf_w1[i], 'hf_b1': hf_b1[i], 'hf_w2': hf_w2[i], 'hf_b2': hf_b2[i], 'hf_w3': hf_w3[i],
            'hf_freq': hf_freq[i], 'hy_bias': hy_bias[i],
            'ssd_conv_w': ssd_conv_w[i], 'ssd_conv_b': ssd_conv_b[i],
            'dtb_row': _pad_lanes(ssd_dt_bias[i].reshape(1, 2 * SSD_HEADS)),
            'a_row': _pad_lanes(-jnp.exp(ssd_a_log[i].astype(F32)).reshape(1, 2 * SSD_HEADS)),
            'dsk_row': jnp.repeat(ssd_d[i].astype(F32), SSD_HEAD_DIM).reshape(1, SSD_INNER),
            'ng_row': ssd_norm_g[i].reshape(1, SSD_INNER),
            'wo1': wo[:HY_WIDTH], 'wo2': wo[HY_WIDTH:HY_WIDTH + SSD_INNER], 'wo3': wo[HY_WIDTH + SSD_INNER:],
            'wq': peer_wq[i].astype(BF16), 'k1': peer_k1[i].astype(BF16), 'k2': peer_k2[i].astype(BF16),
            'u': peer_u[i].astype(BF16), 'vt': peer_v[i].astype(BF16).T,
        }
        xl, xc = _layer(xl, xc, mods[i, :b], mods[i, b], p, i < depth - 1)
    return _final_norm(xl.reshape(b * l, d), g_final.reshape(1, d)).reshape(b, l, d)
```

```python
import functools
import math

import jax
import jax.numpy as jnp
import numpy as np
from jax import lax
from jax.experimental import pallas as pl
from jax.experimental.pallas import tpu as pltpu

F32 = jnp.float32
BF16 = jnp.bfloat16
HI = lax.Precision.HIGHEST

EPS = 1e-6
LANES = 128
SUBLANES = 8
VMEM_LIMIT = 56 * 1024 * 1024

HY_WIDTH = 256
HY_EMB = 33
HY_BANDS = (HY_EMB - 1) // 2
HY_DECAY_TARGET = 1e-2
HY_SHORT_DECAY_PCT = 0.3
HY_LONG_DECAY_PCT = 1.5
SSD_HEADS = 8
SSD_HEAD_DIM = 64
SSD_INNER = SSD_HEADS * SSD_HEAD_DIM
SSD_GROUPS = 2
SSD_STATE = 128
SSD_CHUNK = 128
SSD_XBC = SSD_INNER + 2 * SSD_GROUPS * SSD_STATE
FN_WIDTH = 256
FN_GROUPS = 4
FN_GROUP_DIM = FN_WIDTH // FN_GROUPS
PEER_HEADS = 8
PEER_TOPK = 16
N_KEYS = 128
PEER_HALF = 128
OFF_HY = 0
OFF_Z = OFF_HY + 3 * HY_WIDTH
OFF_XBC = OFF_Z + SSD_INNER
OFF_DT = OFF_XBC + SSD_XBC
OFF_FN = OFF_DT + 2 * SSD_HEADS
D_IN_PROJ = OFF_FN + FN_WIDTH


def _params(*sem):
    return pltpu.CompilerParams(dimension_semantics=sem, vmem_limit_bytes=VMEM_LIMIT)


def _silu(x):
    return x * jax.nn.sigmoid(x)


def _softplus(x):
    return jnp.maximum(x, 0.0) + jnp.log1p(jnp.exp(-jnp.abs(x)))


def _gelu_tanh(x):
    return 0.5 * x * (1.0 + jnp.tanh(math.sqrt(2.0 / math.pi) * (x + 0.044715 * (x * x * x))))


def _ada_kernel(c_ref, w_ref, b_ref, o_ref):
    s = _silu(c_ref[...])
    o_ref[0] = jnp.dot(s, w_ref[0], preferred_element_type=F32, precision=HI) + b_ref[0]


def _ada_mod(cc, w_ada, b_ada):
    depth, d, n = w_ada.shape
    r = cc.shape[0]
    tn = 1536
    return pl.pallas_call(
        _ada_kernel,
        out_shape=jax.ShapeDtypeStruct((depth, r, n), F32),
        grid=(depth, n // tn),
        in_specs=[pl.BlockSpec((r, d), lambda l, j: (0, 0)),
                  pl.BlockSpec((1, d, tn), lambda l, j: (l, 0, j)),
                  pl.BlockSpec((1, 1, tn), lambda l, j: (l, 0, j))],
        out_specs=pl.BlockSpec((1, r, tn), lambda l, j: (l, 0, j)),
        compiler_params=_params("arbitrary", "arbitrary"),
        name="ada_mod",
    )(cc, w_ada, b_ada.reshape(depth, 1, n))


def _normmod(x, g, sh, sc):
    y = x * lax.rsqrt(jnp.mean(x * x, axis=-1, keepdims=True) + EPS) * g
    return y * (1.0 + sc) + sh


def _inproj_kernel(x_ref, g_ref, sh_ref, sc_ref, *rest, n_w):
    w_refs, o_refs = rest[:n_w], rest[n_w:]
    hb = _normmod(x_ref[0], g_ref[...], sh_ref[0], sc_ref[0]).astype(BF16)
    for w_ref, o_ref in zip(w_refs, o_refs):
        o_ref[0] = jnp.dot(hb, w_ref[...], preferred_element_type=F32).astype(o_ref.dtype)


def _in_proj(x, g, sh, sc, ws):
    b, l, d = x.shape
    tm = min(512, l)
    return pl.pallas_call(
        functools.partial(_inproj_kernel, n_w=len(ws)),
        out_shape=[jax.ShapeDtypeStruct((b, l, w.shape[1]), F32) for w in ws],
        grid=(b, l // tm),
        in_specs=[pl.BlockSpec((1, tm, d), lambda i, j: (i, j, 0)),
                  pl.BlockSpec((1, d), lambda i, j: (0, 0)),
                  pl.BlockSpec((1, 1, d), lambda i, j: (i, 0, 0)),
                  pl.BlockSpec((1, 1, d), lambda i, j: (i, 0, 0))]
        + [pl.BlockSpec(w.shape, lambda i, j: (0, 0)) for w in ws],
        out_specs=[pl.BlockSpec((1, tm, w.shape[1]), lambda i, j: (i, j, 0)) for w in ws],
        compiler_params=_params("arbitrary", "arbitrary"),
        name="in_proj",
    )(x, g, sh, sc, *ws)


def _conv3(u, w, bias):
    l = u.shape[0]
    row = lax.broadcasted_iota(jnp.int32, u.shape, 0)
    um = jnp.where(row == 0, 0.0, pltpu.roll(u, 1, 0))
    up = jnp.where(row == l - 1, 0.0, pltpu.roll(u, l - 1, 0))
    return um * w[0:1, :] + u * w[1:2, :] + up * w[2:3, :] + bias


def _dwconv_kernel(u_ref, w_ref, b_ref, o_ref, *, act):
    y = _conv3(u_ref[0], w_ref[...], b_ref[...])
    if act:
        y = _silu(y)
    o_ref[0] = y


def _dwconv(u, w, bias, act):
    b, l, c = u.shape
    cb = 256
    return pl.pallas_call(
        functools.partial(_dwconv_kernel, act=act),
        out_shape=jax.ShapeDtypeStruct((b, l, c), F32),
        grid=(b, c // cb),
        in_specs=[pl.BlockSpec((1, l, cb), lambda i, j: (i, 0, j)),
                  pl.BlockSpec((3, cb), lambda i, j: (0, j)),
                  pl.BlockSpec((1, cb), lambda i, j: (0, j))],
        out_specs=pl.BlockSpec((1, l, cb), lambda i, j: (i, 0, j)),
        compiler_params=_params("arbitrary", "arbitrary"),
        name="dwconv",
    )(u, w, bias.reshape(1, c))


def _split3(x):
    hi = x.astype(BF16)
    r1 = x - hi.astype(F32)
    mid = r1.astype(BF16)
    lo = (r1 - mid.astype(F32)).astype(BF16)
    return hi, mid, lo


def _ssd_chunk(xbc, dt_raw, dtb, arow, st_ref, direction):
    q = SSD_CHUNK
    x = xbc[:, :SSD_INNER]
    gn = SSD_GROUPS * SSD_STATE
    bm = xbc[:, SSD_INNER:SSD_INNER + gn].astype(BF16)
    cm = xbc[:, SSD_INNER + gn:].astype(BF16)
    dtp = _softplus(dt_raw + dtb)
    da = dtp * arow
    ii = lax.broadcasted_iota(jnp.int32, (q, q), 0)
    jj = lax.broadcasted_iota(jnp.int32, (q, q), 1)
    tri = (ii >= jj) if direction == 0 else (ii <= jj)
    tri_b = tri.astype(BF16)
    cs = jnp.dot(jnp.concatenate([tri_b] * 3, axis=1), jnp.concatenate(_split3(da), axis=0),
                 preferred_element_type=F32)
    cs_t = cs.T
    lane = lax.broadcasted_iota(jnp.int32, (LANES, SSD_INNER), 0)
    chan = lax.broadcasted_iota(jnp.int32, (LANES, SSD_INNER), 1)
    head_of = lax.shift_right_logical(chan, int(math.log2(SSD_HEAD_DIM)))
    expand = (lane == direction * SSD_HEADS + head_of).astype(BF16)
    expand3 = jnp.concatenate([expand] * 3, axis=0)
    both = jnp.concatenate([jnp.concatenate(_split3(cs), axis=1), jnp.concatenate(_split3(dtp), axis=1)], axis=0)
    wide = jnp.dot(both, expand3, preferred_element_type=F32)
    acum, dtf = wide[:q], wide[q:]
    tot = acum[q - 1:q, :] if direction == 0 else acum[0:1, :]
    xdt = x * dtf
    ea = jnp.exp(acum)
    xdec = (jnp.exp(tot - acum) * xdt).astype(BF16)
    cd = jnp.exp(tot)
    xdt_b = xdt.astype(BF16)
    hg = SSD_HEADS // SSD_GROUPS
    gw = hg * SSD_HEAD_DIM
    outs = []
    for g in range(SSD_GROUPS):
        bg = bm[:, g * SSD_STATE:(g + 1) * SSD_STATE]
        cg = cm[:, g * SSD_STATE:(g + 1) * SSD_STATE]
        cb = lax.dot_general(cg, bg, (((1,), (1,)), ((), ())), preferred_element_type=F32)
        st = st_ref[g]
        y_g = jnp.dot(cg, st.astype(BF16), preferred_element_type=F32) * ea[:, g * gw:(g + 1) * gw]
        xg = xdt_b[:, g * gw:(g + 1) * gw]
        head_g = lax.shift_right_logical(lax.broadcasted_iota(jnp.int32, xg.shape, 1), int(math.log2(SSD_HEAD_DIM)))
        for k in range(hg):
            ln = direction * SSD_HEADS + g * hg + k
            seg = cs[:, ln:ln + 1] - cs_t[ln:ln + 1, :]
            lmat = jnp.where(tri, jnp.exp(jnp.where(tri, seg, 0.0)), 0.0)
            m = (cb * lmat).astype(BF16)
            y_g = y_g + jnp.dot(m, jnp.where(head_g == k, xg, jnp.zeros_like(xg)), preferred_element_type=F32)
        outs.append(y_g)
        sg = lax.dot_general(bg, xdec[:, g * gw:(g + 1) * gw], (((0,), (0,)), ((), ())),
                             preferred_element_type=F32)
        st_ref[g] = st * cd[:, g * gw:(g + 1) * gw] + sg
    return jnp.concatenate(outs, axis=1)


def _ssd_kernel(xf_ref, dtf_ref, xb_ref, dtb_ref, bias_ref, arow_ref, h0f_ref, h0b_ref,
                yf_ref, yb_ref, hlf_ref, hlb_ref, stf_ref, stb_ref, *, nc):
    c = pl.program_id(1)

    @pl.when(c == 0)
    def _():
        stf_ref[...] = h0f_ref[0]
        stb_ref[...] = h0b_ref[0]

    yf_ref[0] = _ssd_chunk(xf_ref[0], dtf_ref[0], bias_ref[...], arow_ref[...], stf_ref, 0)
    yb_ref[0] = _ssd_chunk(xb_ref[0], dtb_ref[0], bias_ref[...], arow_ref[...], stb_ref, 1)

    @pl.when(c == nc - 1)
    def _():
        hlf_ref[0] = stf_ref[...]
        hlb_ref[0] = stb_ref[...]


def _ssd_scan(xbc, dt, dtb_row, a_row, h0f, h0b):
    b, l, _ = xbc.shape
    nc = l // SSD_CHUNK
    fwd = lambda i, c: (i, c, 0)
    bwd = lambda i, c: (i, nc - 1 - c, 0)
    st_shape = (SSD_GROUPS, SSD_STATE, SSD_INNER // SSD_GROUPS)
    st_spec = pl.BlockSpec((1,) + st_shape, lambda i, c: (i, 0, 0, 0))
    row = pl.BlockSpec((1, LANES), lambda i, c: (0, 0))
    return pl.pallas_call(
        functools.partial(_ssd_kernel, nc=nc),
        out_shape=[jax.ShapeDtypeStruct((b, l, SSD_INNER), F32)] * 2
        + [jax.ShapeDtypeStruct((b,) + st_shape, F32)] * 2,
        grid=(b, nc),
        in_specs=[pl.BlockSpec((1, SSD_CHUNK, SSD_XBC), fwd), pl.BlockSpec((1, SSD_CHUNK, LANES), fwd),
                  pl.BlockSpec((1, SSD_CHUNK, SSD_XBC), bwd), pl.BlockSpec((1, SSD_CHUNK, LANES), bwd),
                  row, row, st_spec, st_spec],
        out_specs=[pl.BlockSpec((1, SSD_CHUNK, SSD_INNER), fwd), pl.BlockSpec((1, SSD_CHUNK, SSD_INNER), bwd),
                   st_spec, st_spec],
        scratch_shapes=[pltpu.VMEM(st_shape, F32), pltpu.VMEM(st_shape, F32)],
        compiler_params=_params("arbitrary", "arbitrary"),
        name="ssd_scan",
    )(xbc, dt, xbc, dt, dtb_row, a_row, h0f, h0b)


def _hy_filter_kernel(z_ref, w1_ref, b1_ref, w2_ref, b2_ref, w3_ref, fr_ref, win_ref, alt_ref,
                      fs_ref, fd_ref, nyq_ref):
    fr = fr_ref[...]
    h = jnp.sin(fr * (jnp.dot(z_ref[...], w1_ref[...], preferred_element_type=F32, precision=HI) + b1_ref[...]))
    h = jnp.sin(fr * (jnp.dot(h, w2_ref[...], preferred_element_type=F32, precision=HI) + b2_ref[...]))
    h = jnp.dot(h, w3_ref[...], preferred_element_type=F32, precision=HI)
    win = win_ref[...]
    row = lax.broadcasted_iota(jnp.int32, win.shape, 0)
    hf = h[:, :HY_WIDTH] * win
    hb = jnp.where(row == 0, 0.0, h[:, HY_WIDTH:] * win)
    nrm = lax.rsqrt(jnp.sum(hf * hf, axis=0, keepdims=True) + jnp.sum(hb * hb, axis=0, keepdims=True) + EPS)
    fs = (hf + hb) * nrm
    fs_ref[...] = fs
    fd_ref[...] = (hb - hf) * nrm
    nyq_ref[...] = jnp.sum(fs * alt_ref[...], axis=0, keepdims=True)


def _hy_filter(l, w1, b1, w2, b2, w3, freq):
    t = np.linspace(0.0, 1.0, l, dtype=np.float32)[:, None]
    w = (np.float32(2.0 * math.pi) * np.arange(l, dtype=np.float32)[:, None] / np.float32(l)).astype(np.float32)
    f = np.linspace(1e-4, HY_BANDS - 1, HY_BANDS, dtype=np.float32)[None, :]
    zf = jnp.asarray(f) * jnp.asarray(w)
    z = jnp.concatenate([jnp.asarray(t), jnp.cos(zf), -jnp.sin(zf)], axis=-1)
    z = jnp.pad(z, ((0, 0), (0, LANES - HY_EMB)))
    hid = w1.shape[1]
    w1p = jnp.pad(w1, ((0, LANES - HY_EMB), (0, LANES - hid)))
    w2p = jnp.pad(w2, ((0, LANES - hid), (0, LANES - hid)))
    w3p = jnp.pad(w3, ((0, LANES - hid), (0, 0)))
    pad_row = lambda v: jnp.pad(v.reshape(1, hid), ((0, 0), (0, LANES - hid)))
    max_decay = math.log(HY_DECAY_TARGET) / HY_SHORT_DECAY_PCT
    min_decay = math.log(HY_DECAY_TARGET) / HY_LONG_DECAY_PCT
    deltas = jnp.abs(jnp.linspace(min_decay, max_decay, HY_WIDTH, dtype=F32))
    win = jnp.exp(-jnp.asarray(t) * deltas)
    alt = jnp.asarray(np.where(np.arange(l) % 2 == 0, 1.0, -1.0).astype(np.float32)[:, None] * np.ones((1, HY_WIDTH), np.float32))
    return pl.pallas_call(
        _hy_filter_kernel,
        out_shape=[jax.ShapeDtypeStruct((l, HY_WIDTH), F32), jax.ShapeDtypeStruct((l, HY_WIDTH), F32),
                   jax.ShapeDtypeStruct((1, HY_WIDTH), F32)],
        compiler_params=pltpu.CompilerParams(vmem_limit_bytes=VMEM_LIMIT),
        name="hy_filter",
    )(z, w1p, pad_row(b1), w2p, pad_row(b2), w3p, pad_row(freq), win, alt)


def _mm_kernel(a_ref, b_ref, o_ref):
    o_ref[...] = jnp.dot(a_ref[...], b_ref[...], preferred_element_type=F32)


def _mm(a, b, tm):
    m, k = a.shape
    n = b.shape[1]
    return pl.pallas_call(
        _mm_kernel,
        out_shape=jax.ShapeDtypeStruct((m, n), F32),
        grid=(m // tm,),
        in_specs=[pl.BlockSpec((tm, k), lambda i: (i, 0)), pl.BlockSpec((k, n), lambda i: (0, 0))],
        out_specs=pl.BlockSpec((tm, n), lambda i: (i, 0)),
        compiler_params=_params("arbitrary"),
        name="mm",
    )(a, b)


def _dft_tables(l):
    n = 2 * l
    k = np.arange(l, dtype=np.int64)[:, None]
    s = np.arange(l, dtype=np.int64)[None, :]
    ang = (2.0 * np.pi / n) * ((k * s) % n).astype(np.float64)
    cos, sin = np.cos(ang), np.sin(ang)
    alt = np.where(np.arange(l) % 2 == 0, 1.0, -1.0)
    sin[0, :] = alt
    fwd = np.concatenate([cos, sin], axis=0)
    wgt = np.full((l, 1), 2.0)
    wgt[0, 0] = 1.0
    sin_i = -sin * wgt
    sin_i[0, :] = alt
    inv = np.concatenate([(cos * wgt).T, sin_i.T], axis=1) / n
    return fwd, inv


def _hy_conv_kernel(u_ref, cw_ref, cb_ref, ff_ref, fi_ref, ka_ref, kb_ref, kc_ref, bias_ref, o_ref,
                    ub_ref, uf_ref, x1_ref, acc_ref, *, nf):
    f = pl.program_id(1)
    w = HY_WIDTH

    @pl.when(f == 0)
    def _():
        part = lambda k: _conv3(u_ref[0, :, k * w:(k + 1) * w], cw_ref[:, k * w:(k + 1) * w],
                                cb_ref[:, k * w:(k + 1) * w])
        u = part(2) * part(0)
        uf_ref[...] = u
        ub_ref[...] = u.astype(BF16)
        x1_ref[...] = part(1)
        acc_ref[...] = jnp.zeros_like(acc_ref)

    uf = jnp.dot(ff_ref[0], ub_ref[...], preferred_element_type=F32)
    fb = uf.shape[0] // 2
    ur, us = uf[:fb], uf[fb:]
    ka, kb, kc = ka_ref[...], kb_ref[...], kc_ref[...]
    p = jnp.concatenate([ur * ka + us * kb, ur * kb - us * kc], axis=0).astype(BF16)
    acc_ref[...] += jnp.dot(fi_ref[0], p, preferred_element_type=F32)

    @pl.when(f == nf - 1)
    def _():
        o_ref[0] = x1_ref[...] * (acc_ref[...] + uf_ref[...] * bias_ref[...])


def _hy_conv(proj, conv_w, conv_b, fs, fd, nyq, bias):
    b, l, _ = proj.shape
    fwd, inv = _dft_tables(l)
    fb = min(512, l)
    nf = l // fb
    ff = jnp.asarray(np.stack([np.concatenate([fwd[i * fb:(i + 1) * fb], fwd[l + i * fb:l + (i + 1) * fb]], axis=0)
                               for i in range(nf)]), dtype=BF16)
    fi = jnp.asarray(np.stack([np.concatenate([inv[:, i * fb:(i + 1) * fb], inv[:, l + i * fb:l + (i + 1) * fb]], axis=1)
                               for i in range(nf)]), dtype=BF16)
    fwd_b = jnp.asarray(fwd, dtype=BF16)
    tm = min(512, l)
    k_r = _mm(fwd_b[:l], fs.astype(BF16), tm)
    k_i = _mm(fwd_b[l:], fd.astype(BF16), tm)
    first = (jnp.arange(l) == 0)[:, None]
    ka = k_r
    kb = jnp.where(first, 0.0, k_i)
    kc = jnp.where(first, -nyq, k_r)
    return pl.pallas_call(
        functools.partial(_hy_conv_kernel, nf=nf),
        out_shape=jax.ShapeDtypeStruct((b, l, HY_WIDTH), F32),
        grid=(b, nf),
        in_specs=[pl.BlockSpec((1, l, 3 * HY_WIDTH), lambda i, f: (i, 0, 0)),
                  pl.BlockSpec((3, 3 * HY_WIDTH), lambda i, f: (0, 0)),
                  pl.BlockSpec((1, 3 * HY_WIDTH), lambda i, f: (0, 0)),
                  pl.BlockSpec((1, 2 * fb, l), lambda i, f: (f, 0, 0)),
                  pl.BlockSpec((1, l, 2 * fb), lambda i, f: (f, 0, 0)),
                  pl.BlockSpec((fb, HY_WIDTH), lambda i, f: (f, 0)),
                  pl.BlockSpec((fb, HY_WIDTH), lambda i, f: (f, 0)),
                  pl.BlockSpec((fb, HY_WIDTH), lambda i, f: (f, 0)),
                  pl.BlockSpec((1, HY_WIDTH), lambda i, f: (0, 0))],
        out_specs=pl.BlockSpec((1, l, HY_WIDTH), lambda i, f: (i, 0, 0)),
        scratch_shapes=[pltpu.VMEM((l, HY_WIDTH), BF16), pltpu.VMEM((l, HY_WIDTH), F32),
                        pltpu.VMEM((l, HY_WIDTH), F32), pltpu.VMEM((l, HY_WIDTH), F32)],
        compiler_params=_params("arbitrary", "arbitrary"),
        name="hy_conv",
    )(proj, conv_w, conv_b.reshape(1, 3 * HY_WIDTH), ff, fi, ka, kb, kc, bias.reshape(1, HY_WIDTH))


def _fnet_kernel(x_ref, cc_ref, sc_ref, m_ref, o_ref, xx_ref):
    r = pl.program_id(1)
    l = x_ref.shape[1]

    @pl.when(r == 0)
    def _():
        xb = x_ref[0].astype(BF16)
        xx_ref[:l, :] = jnp.dot(xb, cc_ref[...], preferred_element_type=F32).astype(BF16)
        xx_ref[l:, :] = jnp.dot(xb, sc_ref[...], preferred_element_type=F32).astype(BF16)

    o_ref[0] = jnp.dot(m_ref[...], xx_ref[...], preferred_element_type=F32)


def _fnet(x):
    b, l, c = x.shape
    gd = FN_GROUP_DIM
    kc = np.arange(gd)[:, None] * np.arange(gd)[None, :]
    ang_c = 2.0 * np.pi * (kc % gd) / gd
    eye = np.eye(FN_GROUPS)
    cblk = jnp.asarray(np.kron(eye, np.cos(ang_c)), dtype=BF16)
    sblk = jnp.asarray(np.kron(eye, np.sin(ang_c)), dtype=BF16)
    kl = (np.arange(l, dtype=np.int64)[:, None] * np.arange(l, dtype=np.int64)[None, :]) % l
    ang_l = 2.0 * np.pi * kl / l
    scale = 1.0 / math.sqrt(l * gd)
    mat = jnp.asarray(np.concatenate([np.cos(ang_l), -np.sin(ang_l)], axis=1) * scale, dtype=BF16)
    tr = min(512, l)
    return pl.pallas_call(
        _fnet_kernel,
        out_shape=jax.ShapeDtypeStruct((b, l, c), F32),
        grid=(b, l // tr),
        in_specs=[pl.BlockSpec((1, l, c), lambda i, r: (i, 0, 0)),
                  pl.BlockSpec((c, c), lambda i, r: (0, 0)),
                  pl.BlockSpec((c, c), lambda i, r: (0, 0)),
                  pl.BlockSpec((tr, 2 * l), lambda i, r: (r, 0))],
        out_specs=pl.BlockSpec((1, tr, c), lambda i, r: (i, r, 0)),
        scratch_shapes=[pltpu.VMEM((2 * l, c), BF16)],
        compiler_params=_params("arbitrary", "arbitrary"),
        name="fnet",
    )(x, cblk, sblk, mat)


def _outproj_kernel(x_ref, ga_ref, yhy_ref, yf_ref, yb_ref, xs_ref, z_ref, yfn_ref, dsk_ref, ng_ref,
                    w1_ref, w2_ref, w3_ref, o_ref):
    y = yf_ref[0] + yb_ref[0] + dsk_ref[...] * xs_ref[0]
    y = y * _silu(z_ref[0])
    gw = SSD_INNER // SSD_GROUPS
    parts = []
    for g in range(SSD_GROUPS):
        yg = y[:, g * gw:(g + 1) * gw]
        parts.append(yg * lax.rsqrt(jnp.mean(yg * yg, axis=-1, keepdims=True) + EPS))
    yn = jnp.concatenate(parts, axis=1) * ng_ref[...]
    out = jnp.dot(yhy_ref[0].astype(BF16), w1_ref[...], preferred_element_type=F32)
    out += jnp.dot(yn.astype(BF16), w2_ref[...], preferred_element_type=F32)
    out += jnp.dot(yfn_ref[0].astype(BF16), w3_ref[...], preferred_element_type=F32)
    o_ref[0] = x_ref[0] + ga_ref[0] * out


def _out_proj(x, ga, y_hy, y_f, y_b, xbc, z, y_fn, dsk_row, ng_row, w1, w2, w3):
    b, l, d = x.shape
    tm = min(512, l)
    tok = lambda n: pl.BlockSpec((1, tm, n), lambda i, j: (i, j, 0))
    const = lambda a: pl.BlockSpec(a.shape, lambda i, j: (0,) * a.ndim)
    return pl.pallas_call(
        _outproj_kernel,
        out_shape=jax.ShapeDtypeStruct((b, l, d), F32),
        grid=(b, l // tm),
        in_specs=[tok(d), pl.BlockSpec((1, 1, d), lambda i, j: (i, 0, 0)), tok(HY_WIDTH), tok(SSD_INNER),
                  tok(SSD_INNER), tok(SSD_INNER), tok(SSD_INNER), tok(FN_WIDTH), const(dsk_row), const(ng_row),
                  const(w1), const(w2), const(w3)],
        out_specs=tok(d),
        compiler_params=_params("arbitrary", "arbitrary"),
        name="out_proj",
    )(x, ga, y_hy, y_f, y_b, xbc, z, y_fn, dsk_row, ng_row, w1, w2, w3)


def _peer_q_kernel(x_ref, g_ref, sh_ref, sc_ref, wq_ref, k1_ref, k2_ref, ht_ref, s1_ref, s2_ref):
    hm = _normmod(x_ref[...], g_ref[...], sh_ref[0], sc_ref[0])
    ht_ref[...] = hm.T.astype(BF16)
    qb = jnp.dot(hm.astype(BF16), wq_ref[...], preferred_element_type=F32).astype(BF16)
    nt = (((1,), (1,)), ((), ()))
    for h in range(PEER_HEADS):
        q1 = qb[:, (2 * h) * PEER_HALF:(2 * h + 1) * PEER_HALF]
        q2 = qb[:, (2 * h + 1) * PEER_HALF:(2 * h + 2) * PEER_HALF]
        s1_ref[h] = lax.dot_general(k1_ref[...], q1, nt, preferred_element_type=F32)
        s2_ref[h] = lax.dot_general(k2_ref[...], q2, nt, preferred_element_type=F32)


def _peer_q(x2, g, sh, sc, wq, k1, k2, l):
    n, d = x2.shape
    tm = min(256, l)
    per = l // tm
    return pl.pallas_call(
        _peer_q_kernel,
        out_shape=[jax.ShapeDtypeStruct((d, n), BF16),
                   jax.ShapeDtypeStruct((PEER_HEADS, N_KEYS, n), F32),
                   jax.ShapeDtypeStruct((PEER_HEADS, N_KEYS, n), F32)],
        grid=(n // tm,),
        in_specs=[pl.BlockSpec((tm, d), lambda i: (i, 0)),
                  pl.BlockSpec((1, d), lambda i: (0, 0)),
                  pl.BlockSpec((1, 1, d), lambda i: (i // per, 0, 0)),
                  pl.BlockSpec((1, 1, d), lambda i: (i // per, 0, 0)),
                  pl.BlockSpec(wq.shape, lambda i: (0, 0)),
                  pl.BlockSpec(k1.shape, lambda i: (0, 0)),
                  pl.BlockSpec(k2.shape, lambda i: (0, 0))],
        out_specs=[pl.BlockSpec((d, tm), lambda i: (0, i)),
                   pl.BlockSpec((PEER_HEADS, N_KEYS, tm), lambda i: (0, 0, i)),
                   pl.BlockSpec((PEER_HEADS, N_KEYS, tm), lambda i: (0, 0, i))],
        compiler_params=_params("arbitrary"),
        name="peer_q",
    )(x2, g, sh, sc, wq, k1, k2)


def _sorting_network(n):
    pairs = []

    def merge(lo, m, r):
        step = 2 * r
        if step < m:
            merge(lo, m, step)
            merge(lo + r, m, step)
            for i in range(lo + r, lo + m - r, step):
                pairs.append((i, i + r))
        else:
            pairs.append((lo, lo + r))

    def sort(lo, m):
        if m > 1:
            half = m // 2
            sort(lo, half)
            sort(lo + half, half)
            merge(lo, m, 1)

    sort(0, n)
    return tuple(pairs)


NEG = -3.0e38
N_TOP = PEER_TOPK + 1


def _top_sorted(rows, n_out):
    rows = list(rows)
    for i, j in _sorting_network(len(rows)):
        rows[i], rows[j] = jnp.maximum(rows[i], rows[j]), jnp.minimum(rows[i], rows[j])
    sub = lax.broadcasted_iota(jnp.int32, rows[0].shape, 0)
    out = []
    for r in range(n_out):
        m = jnp.max(rows[0], axis=0, keepdims=True)
        out.append(m)
        first = jnp.min(jnp.where(rows[0] == m, sub, SUBLANES), axis=0, keepdims=True)
        hit = sub == first
        keep = min(len(rows), n_out - r - 1)
        rows = [jnp.where(hit, rows[k + 1] if k + 1 < len(rows) else NEG, rows[k]) for k in range(keep)]
    return out


def _count_above(vals, x, strict):
    r = jnp.zeros_like(x)
    for j, v in enumerate(vals):
        r = jnp.where((v > x) if strict else (v >= x), float(j + 1), r)
    return r


def _dup_bf16(x):
    hi = pltpu.bitcast(x.astype(BF16).astype(F32), jnp.uint32)
    return hi | (hi >> 16)


def _peer_stats_kernel(s1_ref, s2_ref, r2_ref, a2_ref, ns_ref, cc_ref):
    nrow = N_KEYS // SUBLANES
    t = s1_ref.shape[2]
    sub = lax.broadcasted_iota(jnp.int32, (SUBLANES, t), 0)
    for h in range(PEER_HEADS):
        s1_rows = [s1_ref[h, SUBLANES * k:SUBLANES * (k + 1), :] for k in range(nrow)]
        s2_rows = [s2_ref[h, SUBLANES * k:SUBLANES * (k + 1), :] for k in range(nrow)]
        v1 = _top_sorted(s1_rows, N_TOP)
        v2 = _top_sorted(s2_rows, N_TOP)
        cands = [v1[i] + v2[j] for i in range(N_TOP) for j in range(N_TOP) if (i + 1) * (j + 1) <= N_TOP]
        packed = []
        for k in range(0, len(cands), SUBLANES):
            blk = jnp.full((SUBLANES, t), NEG, F32)
            for s, cv in enumerate(cands[k:k + SUBLANES]):
                blk = jnp.where(sub == s, cv, blk)
            packed.append(blk)
        ids = [sub + SUBLANES * k for k in range(len(packed))]
        big = SUBLANES * len(packed)
        top = []
        for r in range(N_TOP):
            m = packed[0]
            for blk in packed[1:]:
                m = jnp.maximum(m, blk)
            m = jnp.max(m, axis=0, keepdims=True)
            top.append(m)
            sel = jnp.where(packed[0] == m, ids[0], big)
            for blk, idk in zip(packed[1:], ids[1:]):
                sel = jnp.minimum(sel, jnp.where(blk == m, idk, big))
            sel = jnp.min(sel, axis=0, keepdims=True)
            packed = [jnp.where(idk == sel, NEG, blk) for blk, idk in zip(packed, ids)]
        mx = top[0]
        z = jnp.zeros((1, t), F32)
        for r in range(PEER_TOPK):
            z = z + jnp.exp(top[r] - mx)
        tau = 0.5 * (top[PEER_TOPK - 1] + top[PEER_TOPK])
        cscale = 0.5 / z
        for k in range(0, nrow, 2):
            pair = slice(SUBLANES * k, SUBLANES * (k + 2))
            r2 = [_count_above(v2[:PEER_TOPK], s2_rows[k + q], True) for q in range(2)]
            r2_ref[h, pair, :] = jnp.concatenate(r2, axis=0).astype(BF16)
            a2 = [jnp.exp(s2_rows[k + q] - v2[0]) for q in range(2)]
            a2_ref[h, pair, :] = jnp.concatenate(a2, axis=0).astype(BF16)
        for k in range(nrow):
            blk = slice(SUBLANES * k, SUBLANES * (k + 1))
            ns_ref[h, blk, :] = _dup_bf16(_count_above(v2[:PEER_TOPK], tau - s1_rows[k], False))
            cc_ref[h, blk, :] = _dup_bf16(jnp.exp(s1_rows[k] - v1[0]) * cscale)


def _peer_stats(s1t, s2t):
    n = s1t.shape[2]
    t = LANES
    blk = pl.BlockSpec((PEER_HEADS, N_KEYS, t), lambda i: (0, 0, i))
    shp = (PEER_HEADS, N_KEYS, n)
    return pl.pallas_call(
        _peer_stats_kernel,
        out_shape=[jax.ShapeDtypeStruct(shp, BF16), jax.ShapeDtypeStruct(shp, BF16),
                   jax.ShapeDtypeStruct(shp, jnp.uint32), jax.ShapeDtypeStruct(shp, jnp.uint32)],
        grid=(n // t,),
        in_specs=[blk, blk],
        out_specs=[blk] * 4,
        compiler_params=_params("arbitrary"),
        name="peer_stats",
    )(s1t, s2t)


E_TILE = 1024
E_BLOCK = 2 * E_TILE


def _gelu_tanh_x2(x):
    c0 = math.sqrt(2.0 / math.pi)
    return x + x * jnp.tanh(x * (c0 + (c0 * 0.044715) * (x * x)))


ROWS_PER_TILE = E_TILE // N_KEYS


def _rows_bf16(word_row):
    words = word_row | jnp.zeros((SUBLANES, word_row.shape[1]), jnp.uint32)
    tile = pltpu.bitcast(words, BF16)
    return jnp.concatenate([tile] * (N_KEYS // tile.shape[0]), axis=0)


def _peer_mix_kernel(ht_ref, u_ref, vt_ref, r2_ref, a2_ref, nsp_ref, ccp_ref, nsc_ref, ccc_ref, x_ref, ga_ref,
                     gf_ref, o_ref, st0_ref, st1_ref, act0_ref, act1_ref, acc_ref, *, nblk, final_norm):
    m = pl.program_id(1)

    def scores(half, st_ref):
        st_ref[...] = jnp.dot(u_ref[half * E_TILE:(half + 1) * E_TILE, :], ht_ref[...],
                              preferred_element_type=F32).astype(BF16)

    def gate(st_ref, act_ref, ns_ref, cc_ref):
        for e in range(ROWS_PER_TILE):
            g = None
            for h in range(PEER_HEADS):
                nb = _rows_bf16(ns_ref[h, e:e + 1, :])
                cb = _rows_bf16(cc_ref[h, e:e + 1, :])
                a2 = a2_ref[h]
                term = jnp.where(r2_ref[h] < nb, a2, jnp.zeros_like(a2)) * cb
                g = term if g is None else g + term
            rows = slice(e * N_KEYS, (e + 1) * N_KEYS)
            act_ref[rows, :] = _gelu_tanh_x2(st_ref[rows, :]) * g

    def accumulate(half, act_ref):
        acc_ref[...] += jnp.dot(vt_ref[:, half * E_TILE:(half + 1) * E_TILE], act_ref[...],
                                preferred_element_type=F32)

    @pl.when(m == 0)
    def _():
        acc_ref[...] = jnp.zeros_like(acc_ref)
        scores(0, st0_ref)
        gate(st0_ref, act0_ref, nsc_ref, ccc_ref)
        scores(1, st1_ref)

    @pl.when((m > 0) & (m < nblk))
    def _():
        accumulate(0, act0_ref)
        gate(st1_ref, act1_ref, nsp_ref, ccp_ref)
        scores(0, st0_ref)
        accumulate(1, act1_ref)
        gate(st0_ref, act0_ref, nsc_ref, ccc_ref)
        scores(1, st1_ref)

    @pl.when(m == nblk)
    def _():
        accumulate(0, act0_ref)
        gate(st1_ref, act1_ref, nsp_ref, ccp_ref)
        accumulate(1, act1_ref)
        y = x_ref[...] + ga_ref[0] * acc_ref[...].T
        if final_norm:
            y = y * lax.rsqrt(jnp.mean(y * y, axis=-1, keepdims=True) + EPS) * gf_ref[...]
        o_ref[...] = y


def _peer_mix(ht, u, vt, r2, a2, ns, cc, x2, ga, l, g_final=None):
    d, n = ht.shape
    n_exp = u.shape[0]
    t = min(512, l)
    per = l // t
    nblk = n_exp // E_BLOCK
    n_tiles = n_exp // E_TILE
    sblk = pl.BlockSpec((PEER_HEADS, N_KEYS, t), lambda i, m: (0, 0, i))
    prev = pl.BlockSpec((PEER_HEADS, ROWS_PER_TILE, t), lambda i, m: (0, jnp.maximum(2 * m - 1, 0), i))
    cur = pl.BlockSpec((PEER_HEADS, ROWS_PER_TILE, t), lambda i, m: (0, jnp.minimum(2 * m, n_tiles - 1), i))
    return pl.pallas_call(
        functools.partial(_peer_mix_kernel, nblk=nblk, final_norm=g_final is not None),
        out_shape=jax.ShapeDtypeStruct((n, d), F32),
        grid=(n // t, nblk + 1),
        in_specs=[pl.BlockSpec((d, t), lambda i, m: (0, i)),
                  pl.BlockSpec((E_BLOCK, d), lambda i, m: (jnp.minimum(m, nblk - 1), 0)),
                  pl.BlockSpec((d, E_BLOCK), lambda i, m: (0, jnp.maximum(m - 1, 0))),
                  sblk, sblk, prev, prev, cur, cur,
                  pl.BlockSpec((t, d), lambda i, m: (i, 0)),
                  pl.BlockSpec((1, 1, d), lambda i, m: (i // per, 0, 0)),
                  pl.BlockSpec((1, d), lambda i, m: (0, 0))],
        out_specs=pl.BlockSpec((t, d), lambda i, m: (i, 0)),
        scratch_shapes=[pltpu.VMEM((E_TILE, t), BF16), pltpu.VMEM((E_TILE, t), BF16),
                        pltpu.VMEM((E_TILE, t), BF16), pltpu.VMEM((E_TILE, t), BF16),
                        pltpu.VMEM((d, t), F32)],
        compiler_params=_params("arbitrary", "arbitrary"),
        name="peer_mix",
    )(ht, u, vt, r2, a2, ns, cc, ns, cc, x2, ga, jnp.ones((1, d), F32) if g_final is None else g_final)


def _pad_lanes(a):
    return jnp.pad(a, ((0, 0), (0, LANES - a.shape[1])))


def _token_mix_inputs(p, proj_hy, l):
    fs, fd, nyq = _hy_filter(l, p['hf_w1'], p['hf_b1'], p['hf_w2'], p['hf_b2'], p['hf_w3'], p['hf_freq'])
    return _hy_conv(proj_hy, p['hy_conv_w'], p['hy_conv_b'], fs, fd, nyq, p['hy_bias'])


def _peer(x, p, sh, sc, ga, g_final=None):
    b, l, d = x.shape
    x2 = x.reshape(b * l, d)
    ht, s1t, s2t = _peer_q(x2, p['g_norm2'], sh, sc, p['wq'], p['k1'], p['k2'], l)
    r2, a2, ns, cc = _peer_stats(s1t, s2t)
    out = _peer_mix(ht, p['u'], p['vt'], r2, a2, ns, cc, x2, ga, l, g_final)
    return out.reshape(b, l, d)


def _layer(xl, xc, mod_l, mod_c, p, ctx_out, g_final=None):
    b = xl.shape[0]
    d = xl.shape[2]
    sh1, sc1, ga1, sh2, sc2, ga2 = [m.reshape(b, 1, d) for m in jnp.split(mod_l, 6, axis=-1)]
    csh1, csc1, cga1, csh2, csc2, cga2 = [jnp.broadcast_to(m.reshape(1, 1, d), (b, 1, d))
                                          for m in jnp.split(mod_c, 6, axis=-1)]
    w_in = p['w_in']
    w_hy, w_z, w_xbc, w_fn = (w_in[:, OFF_HY:OFF_Z], w_in[:, OFF_Z:OFF_XBC], w_in[:, OFF_XBC:OFF_DT],
                              w_in[:, OFF_FN:D_IN_PROJ])
    w_dt = _pad_lanes(w_in[:, OFF_DT:OFF_FN])
    g1 = p['g_norm1']
    pl_hy, pl_z, pl_xbc, pl_dt, pl_fn = _in_proj(xl, g1, sh1, sc1, [w_hy, w_z, w_xbc, w_dt, w_fn])
    if ctx_out:
        pc_hy, pc_z, pc_xbc, pc_dt, pc_fn = _in_proj(xc, g1, csh1, csc1, [w_hy, w_z, w_xbc, w_dt, w_fn])
    else:
        pc_xbc, pc_dt = _in_proj(xc, g1, csh1, csc1, [w_xbc, w_dt])

    xbc_c = _dwconv(pc_xbc, p['ssd_conv_w'], p['ssd_conv_b'], act=True)
    xbc_l = _dwconv(pl_xbc, p['ssd_conv_w'], p['ssd_conv_b'], act=True)
    dtb, arow = p['dtb_row'], p['a_row']
    h0 = jnp.zeros((b, SSD_GROUPS, SSD_STATE, SSD_INNER // SSD_GROUPS), F32)
    yc_f, yc_b, hc_f, hc_b = _ssd_scan(xbc_c, pc_dt, dtb, arow, h0, h0)
    yl_f, yl_b, _, _ = _ssd_scan(xbc_l, pl_dt, dtb, arow, hc_f, hc_b)

    def token_mix(x, ga, proj_hy, y_f, y_b, xbc, z, proj_fn):
        y_hy = _token_mix_inputs(p, proj_hy, x.shape[1])
        y_fn = _fnet(proj_fn)
        return _out_proj(x, ga, y_hy, y_f, y_b, xbc, z, y_fn, p['dsk_row'], p['ng_row'], p['wo1'], p['wo2'], p['wo3'])

    xl = token_mix(xl, ga1, pl_hy, yl_f, yl_b, xbc_l, pl_z, pl_fn)
    xl = _peer(xl, p, sh2, sc2, ga2, g_final)
    if ctx_out:
        xc = token_mix(xc, cga1, pc_hy, yc_f, yc_b, xbc_c, pc_z, pc_fn)
        xc = _peer(xc, p, csh2, csc2, cga2)
    return xl, xc


def kernel(x, c, ctx, c_ctx, w_ada, b_ada, g_norm1, g_norm2, w_in, hy_conv_w, hy_conv_b, hf_w1, hf_b1, hf_w2, hf_b2, hf_w3, hf_freq, hy_bias, ssd_conv_w, ssd_conv_b, ssd_dt_bias, ssd_a_log, ssd_d, ssd_norm_g, w_out, peer_wq, peer_k1, peer_k2, peer_u, peer_v, g_final):
    depth = w_ada.shape[0]
    b, l, d = x.shape
    rows = -(-(b + 1) // SUBLANES) * SUBLANES
    cc = jnp.concatenate([c, c_ctx[None, :], jnp.zeros((rows - b - 1, d), F32)], axis=0)
    mods = _ada_mod(cc, w_ada, b_ada)
    xl, xc = x, ctx
    for i in range(depth):
        wo = w_out[i].astype(BF16)
        p = {
            'g_norm1': g_norm1[i].reshape(1, d), 'g_norm2': g_norm2[i].reshape(1, d),
            'w_in': w_in[i].astype(BF16),
            'hy_conv_w': hy_conv_w[i], 'hy_conv_b': hy_conv_b[i],
            'hf_w1': hf_w1[i], 'hf_b1': hf_b1[i], 'hf_w2': hf_w2[i], 'hf_b2': hf_b2[i], 'hf_w3': hf_w3[i],
            'hf_freq': hf_freq[i], 'hy_bias': hy_bias[i],
            'ssd_conv_w': ssd_conv_w[i], 'ssd_conv_b': ssd_conv_b[i],
            'dtb_row': _pad_lanes(ssd_dt_bias[i].reshape(1, 2 * SSD_HEADS)),
            'a_row': _pad_lanes(-jnp.exp(ssd_a_log[i].astype(F32)).reshape(1, 2 * SSD_HEADS)),
            'dsk_row': jnp.repeat(ssd_d[i].astype(F32), SSD_HEAD_DIM).reshape(1, SSD_INNER),
            'ng_row': ssd_norm_g[i].reshape(1, SSD_INNER),
            'wo1': wo[:HY_WIDTH], 'wo2': wo[HY_WIDTH:HY_WIDTH + SSD_INNER], 'wo3': wo[HY_WIDTH + SSD_INNER:],
            'wq': peer_wq[i].astype(BF16), 'k1': peer_k1[i].astype(BF16), 'k2': peer_k2[i].astype(BF16),
            'u': peer_u[i].astype(BF16), 'vt': peer_v[i].astype(BF16).T,
        }
        last = i == depth - 1
        xl, xc = _layer(xl, xc, mods[i, :b], mods[i, b], p, not last, g_final.reshape(1, d) if last else None)
    return xl
```

```python
import functools
import math

import jax
import jax.numpy as jnp
import numpy as np
from jax import lax
from jax.experimental import pallas as pl
from jax.experimental.pallas import tpu as pltpu

F32 = jnp.float32
BF16 = jnp.bfloat16
F8 = jnp.float8_e4m3fn
HI = lax.Precision.HIGHEST

EPS = 1e-6
LANES = 128
SUBLANES = 8
VMEM_LIMIT = 56 * 1024 * 1024

HY_WIDTH = 256
HY_EMB = 33
HY_BANDS = (HY_EMB - 1) // 2
HY_DECAY_TARGET = 1e-2
HY_SHORT_DECAY_PCT = 0.3
HY_LONG_DECAY_PCT = 1.5
SSD_HEADS = 8
SSD_HEAD_DIM = 64
SSD_INNER = SSD_HEADS * SSD_HEAD_DIM
SSD_GROUPS = 2
SSD_STATE = 128
SSD_CHUNK = 128
SSD_XBC = SSD_INNER + 2 * SSD_GROUPS * SSD_STATE
FN_WIDTH = 256
FN_GROUPS = 4
FN_GROUP_DIM = FN_WIDTH // FN_GROUPS
PEER_HEADS = 8
PEER_TOPK = 16
N_KEYS = 128
PEER_HALF = 128
OFF_HY = 0
OFF_Z = OFF_HY + 3 * HY_WIDTH
OFF_XBC = OFF_Z + SSD_INNER
OFF_DT = OFF_XBC + SSD_XBC
OFF_FN = OFF_DT + 2 * SSD_HEADS
D_IN_PROJ = OFF_FN + FN_WIDTH


def _params(*sem):
    return pltpu.CompilerParams(dimension_semantics=sem, vmem_limit_bytes=VMEM_LIMIT)


def _silu(x):
    return x * jax.nn.sigmoid(x)


def _softplus(x):
    return jnp.maximum(x, 0.0) + jnp.log1p(jnp.exp(-jnp.abs(x)))


def _gelu_tanh(x):
    return 0.5 * x * (1.0 + jnp.tanh(math.sqrt(2.0 / math.pi) * (x + 0.044715 * (x * x * x))))


def _ada_kernel(c_ref, w_ref, b_ref, o_ref):
    s = _silu(c_ref[...])
    o_ref[0] = jnp.dot(s, w_ref[0], preferred_element_type=F32, precision=HI) + b_ref[0]


def _ada_mod(cc, w_ada, b_ada):
    depth, d, n = w_ada.shape
    r = cc.shape[0]
    tn = 1536
    return pl.pallas_call(
        _ada_kernel,
        out_shape=jax.ShapeDtypeStruct((depth, r, n), F32),
        grid=(depth, n // tn),
        in_specs=[pl.BlockSpec((r, d), lambda l, j: (0, 0)),
                  pl.BlockSpec((1, d, tn), lambda l, j: (l, 0, j)),
                  pl.BlockSpec((1, 1, tn), lambda l, j: (l, 0, j))],
        out_specs=pl.BlockSpec((1, r, tn), lambda l, j: (l, 0, j)),
        compiler_params=_params("arbitrary", "arbitrary"),
        name="ada_mod",
    )(cc, w_ada, b_ada.reshape(depth, 1, n))


def _normmod(x, g, sh, sc):
    y = x * lax.rsqrt(jnp.mean(x * x, axis=-1, keepdims=True) + EPS) * g
    return y * (1.0 + sc) + sh


def _inproj_kernel(x_ref, g_ref, sh_ref, sc_ref, *rest, n_w):
    w_refs, o_refs = rest[:n_w], rest[n_w:]
    hb = _normmod(x_ref[0], g_ref[...], sh_ref[0], sc_ref[0]).astype(BF16)
    for w_ref, o_ref in zip(w_refs, o_refs):
        o_ref[0] = jnp.dot(hb, w_ref[...], preferred_element_type=F32).astype(o_ref.dtype)


def _in_proj(x, g, sh, sc, ws):
    b, l, d = x.shape
    tm = min(512, l)
    return pl.pallas_call(
        functools.partial(_inproj_kernel, n_w=len(ws)),
        out_shape=[jax.ShapeDtypeStruct((b, l, w.shape[1]), F32) for w in ws],
        grid=(b, l // tm),
        in_specs=[pl.BlockSpec((1, tm, d), lambda i, j: (i, j, 0)),
                  pl.BlockSpec((1, d), lambda i, j: (0, 0)),
                  pl.BlockSpec((1, 1, d), lambda i, j: (i, 0, 0)),
                  pl.BlockSpec((1, 1, d), lambda i, j: (i, 0, 0))]
        + [pl.BlockSpec(w.shape, lambda i, j: (0, 0)) for w in ws],
        out_specs=[pl.BlockSpec((1, tm, w.shape[1]), lambda i, j: (i, j, 0)) for w in ws],
        compiler_params=_params("arbitrary", "arbitrary"),
        name="in_proj",
    )(x, g, sh, sc, *ws)


def _conv3(u, w, bias):
    l = u.shape[0]
    row = lax.broadcasted_iota(jnp.int32, u.shape, 0)
    um = jnp.where(row == 0, 0.0, pltpu.roll(u, 1, 0))
    up = jnp.where(row == l - 1, 0.0, pltpu.roll(u, l - 1, 0))
    return um * w[0:1, :] + u * w[1:2, :] + up * w[2:3, :] + bias


def _dwconv_kernel(u_ref, w_ref, b_ref, o_ref, *, act):
    y = _conv3(u_ref[0], w_ref[...], b_ref[...])
    if act:
        y = _silu(y)
    o_ref[0] = y


def _dwconv(u, w, bias, act):
    b, l, c = u.shape
    cb = 256
    return pl.pallas_call(
        functools.partial(_dwconv_kernel, act=act),
        out_shape=jax.ShapeDtypeStruct((b, l, c), F32),
        grid=(b, c // cb),
        in_specs=[pl.BlockSpec((1, l, cb), lambda i, j: (i, 0, j)),
                  pl.BlockSpec((3, cb), lambda i, j: (0, j)),
                  pl.BlockSpec((1, cb), lambda i, j: (0, j))],
        out_specs=pl.BlockSpec((1, l, cb), lambda i, j: (i, 0, j)),
        compiler_params=_params("arbitrary", "arbitrary"),
        name="dwconv",
    )(u, w, bias.reshape(1, c))


def _split3(x):
    hi = x.astype(BF16)
    r1 = x - hi.astype(F32)
    mid = r1.astype(BF16)
    lo = (r1 - mid.astype(F32)).astype(BF16)
    return hi, mid, lo


def _ssd_chunk(xbc, dt_raw, dtb, arow, st_ref, direction):
    q = SSD_CHUNK
    x = xbc[:, :SSD_INNER]
    gn = SSD_GROUPS * SSD_STATE
    bm = xbc[:, SSD_INNER:SSD_INNER + gn].astype(BF16)
    cm = xbc[:, SSD_INNER + gn:].astype(BF16)
    dtp = _softplus(dt_raw + dtb)
    da = dtp * arow
    ii = lax.broadcasted_iota(jnp.int32, (q, q), 0)
    jj = lax.broadcasted_iota(jnp.int32, (q, q), 1)
    tri = (ii >= jj) if direction == 0 else (ii <= jj)
    tri_b = tri.astype(BF16)
    cs = jnp.dot(jnp.concatenate([tri_b] * 3, axis=1), jnp.concatenate(_split3(da), axis=0),
                 preferred_element_type=F32)
    cs_t = cs.T
    lane = lax.broadcasted_iota(jnp.int32, (LANES, SSD_INNER), 0)
    chan = lax.broadcasted_iota(jnp.int32, (LANES, SSD_INNER), 1)
    head_of = lax.shift_right_logical(chan, int(math.log2(SSD_HEAD_DIM)))
    expand = (lane == direction * SSD_HEADS + head_of).astype(BF16)
    expand3 = jnp.concatenate([expand] * 3, axis=0)
    both = jnp.concatenate([jnp.concatenate(_split3(cs), axis=1), jnp.concatenate(_split3(dtp), axis=1)], axis=0)
    wide = jnp.dot(both, expand3, preferred_element_type=F32)
    acum, dtf = wide[:q], wide[q:]
    tot = acum[q - 1:q, :] if direction == 0 else acum[0:1, :]
    xdt = x * dtf
    ea = jnp.exp(acum)
    xdec = (jnp.exp(tot - acum) * xdt).astype(BF16)
    cd = jnp.exp(tot)
    xdt_b = xdt.astype(BF16)
    hg = SSD_HEADS // SSD_GROUPS
    gw = hg * SSD_HEAD_DIM
    outs = []
    for g in range(SSD_GROUPS):
        bg = bm[:, g * SSD_STATE:(g + 1) * SSD_STATE]
        cg = cm[:, g * SSD_STATE:(g + 1) * SSD_STATE]
        cb = lax.dot_general(cg, bg, (((1,), (1,)), ((), ())), preferred_element_type=F32)
        st = st_ref[g]
        y_g = jnp.dot(cg, st.astype(BF16), preferred_element_type=F32) * ea[:, g * gw:(g + 1) * gw]
        xg = xdt_b[:, g * gw:(g + 1) * gw]
        head_g = lax.shift_right_logical(lax.broadcasted_iota(jnp.int32, xg.shape, 1), int(math.log2(SSD_HEAD_DIM)))
        for k in range(hg):
            ln = direction * SSD_HEADS + g * hg + k
            seg = cs[:, ln:ln + 1] - cs_t[ln:ln + 1, :]
            lmat = jnp.where(tri, jnp.exp(jnp.where(tri, seg, 0.0)), 0.0)
            m = (cb * lmat).astype(BF16)
            y_g = y_g + jnp.dot(m, jnp.where(head_g == k, xg, jnp.zeros_like(xg)), preferred_element_type=F32)
        outs.append(y_g)
        sg = lax.dot_general(bg, xdec[:, g * gw:(g + 1) * gw], (((0,), (0,)), ((), ())),
                             preferred_element_type=F32)
        st_ref[g] = st * cd[:, g * gw:(g + 1) * gw] + sg
    return jnp.concatenate(outs, axis=1)


def _ssd_kernel(xf_ref, dtf_ref, xb_ref, dtb_ref, bias_ref, arow_ref, h0f_ref, h0b_ref,
                yf_ref, yb_ref, hlf_ref, hlb_ref, stf_ref, stb_ref, *, nc):
    c = pl.program_id(1)

    @pl.when(c == 0)
    def _():
        stf_ref[...] = h0f_ref[0]
        stb_ref[...] = h0b_ref[0]

    yf_ref[0] = _ssd_chunk(xf_ref[0], dtf_ref[0], bias_ref[...], arow_ref[...], stf_ref, 0)
    yb_ref[0] = _ssd_chunk(xb_ref[0], dtb_ref[0], bias_ref[...], arow_ref[...], stb_ref, 1)

    @pl.when(c == nc - 1)
    def _():
        hlf_ref[0] = stf_ref[...]
        hlb_ref[0] = stb_ref[...]


def _ssd_scan(xbc, dt, dtb_row, a_row, h0f, h0b):
    b, l, _ = xbc.shape
    nc = l // SSD_CHUNK
    fwd = lambda i, c: (i, c, 0)
    bwd = lambda i, c: (i, nc - 1 - c, 0)
    st_shape = (SSD_GROUPS, SSD_STATE, SSD_INNER // SSD_GROUPS)
    st_spec = pl.BlockSpec((1,) + st_shape, lambda i, c: (i, 0, 0, 0))
    row = pl.BlockSpec((1, LANES), lambda i, c: (0, 0))
    return pl.pallas_call(
        functools.partial(_ssd_kernel, nc=nc),
        out_shape=[jax.ShapeDtypeStruct((b, l, SSD_INNER), F32)] * 2
        + [jax.ShapeDtypeStruct((b,) + st_shape, F32)] * 2,
        grid=(b, nc),
        in_specs=[pl.BlockSpec((1, SSD_CHUNK, SSD_XBC), fwd), pl.BlockSpec((1, SSD_CHUNK, LANES), fwd),
                  pl.BlockSpec((1, SSD_CHUNK, SSD_XBC), bwd), pl.BlockSpec((1, SSD_CHUNK, LANES), bwd),
                  row, row, st_spec, st_spec],
        out_specs=[pl.BlockSpec((1, SSD_CHUNK, SSD_INNER), fwd), pl.BlockSpec((1, SSD_CHUNK, SSD_INNER), bwd),
                   st_spec, st_spec],
        scratch_shapes=[pltpu.VMEM(st_shape, F32), pltpu.VMEM(st_shape, F32)],
        compiler_params=_params("arbitrary", "arbitrary"),
        name="ssd_scan",
    )(xbc, dt, xbc, dt, dtb_row, a_row, h0f, h0b)


def _hy_filter_kernel(z_ref, w1_ref, b1_ref, w2_ref, b2_ref, w3_ref, fr_ref, win_ref, alt_ref,
                      fs_ref, fd_ref, nyq_ref):
    fr = fr_ref[...]
    h = jnp.sin(fr * (jnp.dot(z_ref[...], w1_ref[...], preferred_element_type=F32, precision=HI) + b1_ref[...]))
    h = jnp.sin(fr * (jnp.dot(h, w2_ref[...], preferred_element_type=F32, precision=HI) + b2_ref[...]))
    h = jnp.dot(h, w3_ref[...], preferred_element_type=F32, precision=HI)
    win = win_ref[...]
    row = lax.broadcasted_iota(jnp.int32, win.shape, 0)
    hf = h[:, :HY_WIDTH] * win
    hb = jnp.where(row == 0, 0.0, h[:, HY_WIDTH:] * win)
    nrm = lax.rsqrt(jnp.sum(hf * hf, axis=0, keepdims=True) + jnp.sum(hb * hb, axis=0, keepdims=True) + EPS)
    fs = (hf + hb) * nrm
    fs_ref[...] = fs
    fd_ref[...] = (hb - hf) * nrm
    nyq_ref[...] = jnp.sum(fs * alt_ref[...], axis=0, keepdims=True)


def _hy_filter(l, w1, b1, w2, b2, w3, freq):
    t = np.linspace(0.0, 1.0, l, dtype=np.float32)[:, None]
    w = (np.float32(2.0 * math.pi) * np.arange(l, dtype=np.float32)[:, None] / np.float32(l)).astype(np.float32)
    f = np.linspace(1e-4, HY_BANDS - 1, HY_BANDS, dtype=np.float32)[None, :]
    zf = jnp.asarray(f) * jnp.asarray(w)
    z = jnp.concatenate([jnp.asarray(t), jnp.cos(zf), -jnp.sin(zf)], axis=-1)
    z = jnp.pad(z, ((0, 0), (0, LANES - HY_EMB)))
    hid = w1.shape[1]
    w1p = jnp.pad(w1, ((0, LANES - HY_EMB), (0, LANES - hid)))
    w2p = jnp.pad(w2, ((0, LANES - hid), (0, LANES - hid)))
    w3p = jnp.pad(w3, ((0, LANES - hid), (0, 0)))
    pad_row = lambda v: jnp.pad(v.reshape(1, hid), ((0, 0), (0, LANES - hid)))
    max_decay = math.log(HY_DECAY_TARGET) / HY_SHORT_DECAY_PCT
    min_decay = math.log(HY_DECAY_TARGET) / HY_LONG_DECAY_PCT
    deltas = jnp.abs(jnp.linspace(min_decay, max_decay, HY_WIDTH, dtype=F32))
    win = jnp.exp(-jnp.asarray(t) * deltas)
    alt = jnp.asarray(np.where(np.arange(l) % 2 == 0, 1.0, -1.0).astype(np.float32)[:, None] * np.ones((1, HY_WIDTH), np.float32))
    return pl.pallas_call(
        _hy_filter_kernel,
        out_shape=[jax.ShapeDtypeStruct((l, HY_WIDTH), F32), jax.ShapeDtypeStruct((l, HY_WIDTH), F32),
                   jax.ShapeDtypeStruct((1, HY_WIDTH), F32)],
        compiler_params=pltpu.CompilerParams(vmem_limit_bytes=VMEM_LIMIT),
        name="hy_filter",
    )(z, w1p, pad_row(b1), w2p, pad_row(b2), w3p, pad_row(freq), win, alt)


def _mm_kernel(a_ref, b_ref, o_ref):
    o_ref[...] = jnp.dot(a_ref[...], b_ref[...], preferred_element_type=F32)


def _mm(a, b, tm):
    m, k = a.shape
    n = b.shape[1]
    return pl.pallas_call(
        _mm_kernel,
        out_shape=jax.ShapeDtypeStruct((m, n), F32),
        grid=(m // tm,),
        in_specs=[pl.BlockSpec((tm, k), lambda i: (i, 0)), pl.BlockSpec((k, n), lambda i: (0, 0))],
        out_specs=pl.BlockSpec((tm, n), lambda i: (i, 0)),
        compiler_params=_params("arbitrary"),
        name="mm",
    )(a, b)


def _dft_tables(l):
    n = 2 * l
    k = np.arange(l, dtype=np.int64)[:, None]
    s = np.arange(l, dtype=np.int64)[None, :]
    ang = (2.0 * np.pi / n) * ((k * s) % n).astype(np.float64)
    cos, sin = np.cos(ang), np.sin(ang)
    alt = np.where(np.arange(l) % 2 == 0, 1.0, -1.0)
    sin[0, :] = alt
    fwd = np.concatenate([cos, sin], axis=0)
    wgt = np.full((l, 1), 2.0)
    wgt[0, 0] = 1.0
    sin_i = -sin * wgt
    sin_i[0, :] = alt
    inv = np.concatenate([(cos * wgt).T, sin_i.T], axis=1) / n
    return fwd, inv


def _hy_conv_kernel(u_ref, cw_ref, cb_ref, ff_ref, fi_ref, ka_ref, kb_ref, kc_ref, bias_ref, o_ref,
                    ub_ref, uf_ref, x1_ref, acc_ref, *, nf):
    f = pl.program_id(1)
    w = HY_WIDTH

    @pl.when(f == 0)
    def _():
        part = lambda k: _conv3(u_ref[0, :, k * w:(k + 1) * w], cw_ref[:, k * w:(k + 1) * w],
                                cb_ref[:, k * w:(k + 1) * w])
        u = part(2) * part(0)
        uf_ref[...] = u
        ub_ref[...] = u.astype(BF16)
        x1_ref[...] = part(1)
        acc_ref[...] = jnp.zeros_like(acc_ref)

    uf = jnp.dot(ff_ref[0], ub_ref[...], preferred_element_type=F32)
    fb = uf.shape[0] // 2
    ur, us = uf[:fb], uf[fb:]
    ka, kb, kc = ka_ref[...], kb_ref[...], kc_ref[...]
    p = jnp.concatenate([ur * ka + us * kb, ur * kb - us * kc], axis=0).astype(BF16)
    acc_ref[...] += jnp.dot(fi_ref[0], p, preferred_element_type=F32)

    @pl.when(f == nf - 1)
    def _():
        o_ref[0] = x1_ref[...] * (acc_ref[...] + uf_ref[...] * bias_ref[...])


def _hy_conv(proj, conv_w, conv_b, fs, fd, nyq, bias):
    b, l, _ = proj.shape
    fwd, inv = _dft_tables(l)
    fb = min(512, l)
    nf = l // fb
    ff = jnp.asarray(np.stack([np.concatenate([fwd[i * fb:(i + 1) * fb], fwd[l + i * fb:l + (i + 1) * fb]], axis=0)
                               for i in range(nf)]), dtype=BF16)
    fi = jnp.asarray(np.stack([np.concatenate([inv[:, i * fb:(i + 1) * fb], inv[:, l + i * fb:l + (i + 1) * fb]], axis=1)
                               for i in range(nf)]), dtype=BF16)
    fwd_b = jnp.asarray(fwd, dtype=BF16)
    tm = min(512, l)
    k_r = _mm(fwd_b[:l], fs.astype(BF16), tm)
    k_i = _mm(fwd_b[l:], fd.astype(BF16), tm)
    first = (jnp.arange(l) == 0)[:, None]
    ka = k_r
    kb = jnp.where(first, 0.0, k_i)
    kc = jnp.where(first, -nyq, k_r)
    return pl.pallas_call(
        functools.partial(_hy_conv_kernel, nf=nf),
        out_shape=jax.ShapeDtypeStruct((b, l, HY_WIDTH), F32),
        grid=(b, nf),
        in_specs=[pl.BlockSpec((1, l, 3 * HY_WIDTH), lambda i, f: (i, 0, 0)),
                  pl.BlockSpec((3, 3 * HY_WIDTH), lambda i, f: (0, 0)),
                  pl.BlockSpec((1, 3 * HY_WIDTH), lambda i, f: (0, 0)),
                  pl.BlockSpec((1, 2 * fb, l), lambda i, f: (f, 0, 0)),
                  pl.BlockSpec((1, l, 2 * fb), lambda i, f: (f, 0, 0)),
                  pl.BlockSpec((fb, HY_WIDTH), lambda i, f: (f, 0)),
                  pl.BlockSpec((fb, HY_WIDTH), lambda i, f: (f, 0)),
                  pl.BlockSpec((fb, HY_WIDTH), lambda i, f: (f, 0)),
                  pl.BlockSpec((1, HY_WIDTH), lambda i, f: (0, 0))],
        out_specs=pl.BlockSpec((1, l, HY_WIDTH), lambda i, f: (i, 0, 0)),
        scratch_shapes=[pltpu.VMEM((l, HY_WIDTH), BF16), pltpu.VMEM((l, HY_WIDTH), F32),
                        pltpu.VMEM((l, HY_WIDTH), F32), pltpu.VMEM((l, HY_WIDTH), F32)],
        compiler_params=_params("arbitrary", "arbitrary"),
        name="hy_conv",
    )(proj, conv_w, conv_b.reshape(1, 3 * HY_WIDTH), ff, fi, ka, kb, kc, bias.reshape(1, HY_WIDTH))


def _fnet_kernel(x_ref, cc_ref, sc_ref, m_ref, o_ref, xx_ref):
    r = pl.program_id(1)
    l = x_ref.shape[1]

    @pl.when(r == 0)
    def _():
        xb = x_ref[0].astype(BF16)
        xx_ref[:l, :] = jnp.dot(xb, cc_ref[...], preferred_element_type=F32).astype(BF16)
        xx_ref[l:, :] = jnp.dot(xb, sc_ref[...], preferred_element_type=F32).astype(BF16)

    o_ref[0] = jnp.dot(m_ref[...], xx_ref[...], preferred_element_type=F32)


def _fnet(x):
    b, l, c = x.shape
    gd = FN_GROUP_DIM
    kc = np.arange(gd)[:, None] * np.arange(gd)[None, :]
    ang_c = 2.0 * np.pi * (kc % gd) / gd
    eye = np.eye(FN_GROUPS)
    cblk = jnp.asarray(np.kron(eye, np.cos(ang_c)), dtype=BF16)
    sblk = jnp.asarray(np.kron(eye, np.sin(ang_c)), dtype=BF16)
    kl = (np.arange(l, dtype=np.int64)[:, None] * np.arange(l, dtype=np.int64)[None, :]) % l
    ang_l = 2.0 * np.pi * kl / l
    scale = 1.0 / math.sqrt(l * gd)
    mat = jnp.asarray(np.concatenate([np.cos(ang_l), -np.sin(ang_l)], axis=1) * scale, dtype=BF16)
    tr = min(512, l)
    return pl.pallas_call(
        _fnet_kernel,
        out_shape=jax.ShapeDtypeStruct((b, l, c), F32),
        grid=(b, l // tr),
        in_specs=[pl.BlockSpec((1, l, c), lambda i, r: (i, 0, 0)),
                  pl.BlockSpec((c, c), lambda i, r: (0, 0)),
                  pl.BlockSpec((c, c), lambda i, r: (0, 0)),
                  pl.BlockSpec((tr, 2 * l), lambda i, r: (r, 0))],
        out_specs=pl.BlockSpec((1, tr, c), lambda i, r: (i, r, 0)),
        scratch_shapes=[pltpu.VMEM((2 * l, c), BF16)],
        compiler_params=_params("arbitrary", "arbitrary"),
        name="fnet",
    )(x, cblk, sblk, mat)


def _outproj_kernel(x_ref, ga_ref, yhy_ref, yf_ref, yb_ref, xs_ref, z_ref, yfn_ref, dsk_ref, ng_ref,
                    w1_ref, w2_ref, w3_ref, o_ref):
    y = yf_ref[0] + yb_ref[0] + dsk_ref[...] * xs_ref[0]
    y = y * _silu(z_ref[0])
    gw = SSD_INNER // SSD_GROUPS
    parts = []
    for g in range(SSD_GROUPS):
        yg = y[:, g * gw:(g + 1) * gw]
        parts.append(yg * lax.rsqrt(jnp.mean(yg * yg, axis=-1, keepdims=True) + EPS))
    yn = jnp.concatenate(parts, axis=1) * ng_ref[...]
    out = jnp.dot(yhy_ref[0].astype(BF16), w1_ref[...], preferred_element_type=F32)
    out += jnp.dot(yn.astype(BF16), w2_ref[...], preferred_element_type=F32)
    out += jnp.dot(yfn_ref[0].astype(BF16), w3_ref[...], preferred_element_type=F32)
    o_ref[0] = x_ref[0] + ga_ref[0] * out


def _out_proj(x, ga, y_hy, y_f, y_b, xbc, z, y_fn, dsk_row, ng_row, w1, w2, w3):
    b, l, d = x.shape
    tm = min(512, l)
    tok = lambda n: pl.BlockSpec((1, tm, n), lambda i, j: (i, j, 0))
    const = lambda a: pl.BlockSpec(a.shape, lambda i, j: (0,) * a.ndim)
    return pl.pallas_call(
        _outproj_kernel,
        out_shape=jax.ShapeDtypeStruct((b, l, d), F32),
        grid=(b, l // tm),
        in_specs=[tok(d), pl.BlockSpec((1, 1, d), lambda i, j: (i, 0, 0)), tok(HY_WIDTH), tok(SSD_INNER),
                  tok(SSD_INNER), tok(SSD_INNER), tok(SSD_INNER), tok(FN_WIDTH), const(dsk_row), const(ng_row),
                  const(w1), const(w2), const(w3)],
        out_specs=tok(d),
        compiler_params=_params("arbitrary", "arbitrary"),
        name="out_proj",
    )(x, ga, y_hy, y_f, y_b, xbc, z, y_fn, dsk_row, ng_row, w1, w2, w3)


def _peer_q_kernel(x_ref, g_ref, sh_ref, sc_ref, wq_ref, k1_ref, k2_ref, ht_ref, s1_ref, s2_ref):
    hm = _normmod(x_ref[...], g_ref[...], sh_ref[0], sc_ref[0])
    ht_ref[...] = hm.T.astype(F8)
    qb = jnp.dot(hm.astype(BF16), wq_ref[...], preferred_element_type=F32).astype(BF16)
    nt = (((1,), (1,)), ((), ()))
    for h in range(PEER_HEADS):
        q1 = qb[:, (2 * h) * PEER_HALF:(2 * h + 1) * PEER_HALF]
        q2 = qb[:, (2 * h + 1) * PEER_HALF:(2 * h + 2) * PEER_HALF]
        s1_ref[h] = lax.dot_general(k1_ref[...], q1, nt, preferred_element_type=F32)
        s2_ref[h] = lax.dot_general(k2_ref[...], q2, nt, preferred_element_type=F32)


def _peer_q(x2, g, sh, sc, wq, k1, k2, l):
    n, d = x2.shape
    tm = min(256, l)
    per = l // tm
    return pl.pallas_call(
        _peer_q_kernel,
        out_shape=[jax.ShapeDtypeStruct((d, n), F8),
                   jax.ShapeDtypeStruct((PEER_HEADS, N_KEYS, n), F32),
                   jax.ShapeDtypeStruct((PEER_HEADS, N_KEYS, n), F32)],
        grid=(n // tm,),
        in_specs=[pl.BlockSpec((tm, d), lambda i: (i, 0)),
                  pl.BlockSpec((1, d), lambda i: (0, 0)),
                  pl.BlockSpec((1, 1, d), lambda i: (i // per, 0, 0)),
                  pl.BlockSpec((1, 1, d), lambda i: (i // per, 0, 0)),
                  pl.BlockSpec(wq.shape, lambda i: (0, 0)),
                  pl.BlockSpec(k1.shape, lambda i: (0, 0)),
                  pl.BlockSpec(k2.shape, lambda i: (0, 0))],
        out_specs=[pl.BlockSpec((d, tm), lambda i: (0, i)),
                   pl.BlockSpec((PEER_HEADS, N_KEYS, tm), lambda i: (0, 0, i)),
                   pl.BlockSpec((PEER_HEADS, N_KEYS, tm), lambda i: (0, 0, i))],
        compiler_params=_params("arbitrary"),
        name="peer_q",
    )(x2, g, sh, sc, wq, k1, k2)


def _sorting_network(n):
    pairs = []

    def merge(lo, m, r):
        step = 2 * r
        if step < m:
            merge(lo, m, step)
            merge(lo + r, m, step)
            for i in range(lo + r, lo + m - r, step):
                pairs.append((i, i + r))
        else:
            pairs.append((lo, lo + r))

    def sort(lo, m):
        if m > 1:
            half = m // 2
            sort(lo, half)
            sort(lo + half, half)
            merge(lo, m, 1)

    sort(0, n)
    return tuple(pairs)


NEG = -3.0e38
N_TOP = PEER_TOPK + 1


def _top_sorted(rows, n_out):
    rows = list(rows)
    for i, j in _sorting_network(len(rows)):
        rows[i], rows[j] = jnp.maximum(rows[i], rows[j]), jnp.minimum(rows[i], rows[j])
    sub = lax.broadcasted_iota(jnp.int32, rows[0].shape, 0)
    out = []
    for r in range(n_out):
        m = jnp.max(rows[0], axis=0, keepdims=True)
        out.append(m)
        first = jnp.min(jnp.where(rows[0] == m, sub, SUBLANES), axis=0, keepdims=True)
        hit = sub == first
        keep = min(len(rows), n_out - r - 1)
        rows = [jnp.where(hit, rows[k + 1] if k + 1 < len(rows) else NEG, rows[k]) for k in range(keep)]
    return out


def _count_above(vals, x, strict):
    r = jnp.zeros_like(x)
    for j, v in enumerate(vals):
        r = jnp.where((v > x) if strict else (v >= x), float(j + 1), r)
    return r


def _dup_bf16(x):
    hi = pltpu.bitcast(x.astype(BF16).astype(F32), jnp.uint32)
    return hi | (hi >> 16)


def _peer_stats_kernel(s1_ref, s2_ref, r2_ref, a2_ref, ns_ref, cc_ref):
    nrow = N_KEYS // SUBLANES
    t = s1_ref.shape[2]
    sub = lax.broadcasted_iota(jnp.int32, (SUBLANES, t), 0)
    for h in range(PEER_HEADS):
        s1_rows = [s1_ref[h, SUBLANES * k:SUBLANES * (k + 1), :] for k in range(nrow)]
        s2_rows = [s2_ref[h, SUBLANES * k:SUBLANES * (k + 1), :] for k in range(nrow)]
        v1 = _top_sorted(s1_rows, N_TOP)
        v2 = _top_sorted(s2_rows, N_TOP)
        cands = [v1[i] + v2[j] for i in range(N_TOP) for j in range(N_TOP) if (i + 1) * (j + 1) <= N_TOP]
        packed = []
        for k in range(0, len(cands), SUBLANES):
            blk = jnp.full((SUBLANES, t), NEG, F32)
            for s, cv in enumerate(cands[k:k + SUBLANES]):
                blk = jnp.where(sub == s, cv, blk)
            packed.append(blk)
        ids = [sub + SUBLANES * k for k in range(len(packed))]
        big = SUBLANES * len(packed)
        top = []
        for r in range(N_TOP):
            m = packed[0]
            for blk in packed[1:]:
                m = jnp.maximum(m, blk)
            m = jnp.max(m, axis=0, keepdims=True)
            top.append(m)
            sel = jnp.where(packed[0] == m, ids[0], big)
            for blk, idk in zip(packed[1:], ids[1:]):
                sel = jnp.minimum(sel, jnp.where(blk == m, idk, big))
            sel = jnp.min(sel, axis=0, keepdims=True)
            packed = [jnp.where(idk == sel, NEG, blk) for blk, idk in zip(packed, ids)]
        mx = top[0]
        z = jnp.zeros((1, t), F32)
        for r in range(PEER_TOPK):
            z = z + jnp.exp(top[r] - mx)
        tau = 0.5 * (top[PEER_TOPK - 1] + top[PEER_TOPK])
        cscale = 0.5 / z
        for k in range(0, nrow, 2):
            pair = slice(SUBLANES * k, SUBLANES * (k + 2))
            r2 = [_count_above(v2[:PEER_TOPK], s2_rows[k + q], True) for q in range(2)]
            r2_ref[h, pair, :] = jnp.concatenate(r2, axis=0).astype(BF16)
            a2 = [jnp.exp(s2_rows[k + q] - v2[0]) for q in range(2)]
            a2_ref[h, pair, :] = jnp.concatenate(a2, axis=0).astype(BF16)
        for k in range(nrow):
            blk = slice(SUBLANES * k, SUBLANES * (k + 1))
            ns_ref[h, blk, :] = _dup_bf16(_count_above(v2[:PEER_TOPK], tau - s1_rows[k], False))
            cc_ref[h, blk, :] = _dup_bf16(jnp.exp(s1_rows[k] - v1[0]) * cscale)


def _peer_stats(s1t, s2t):
    n = s1t.shape[2]
    t = LANES
    blk = pl.BlockSpec((PEER_HEADS, N_KEYS, t), lambda i: (0, 0, i))
    shp = (PEER_HEADS, N_KEYS, n)
    return pl.pallas_call(
        _peer_stats_kernel,
        out_shape=[jax.ShapeDtypeStruct(shp, BF16), jax.ShapeDtypeStruct(shp, BF16),
                   jax.ShapeDtypeStruct(shp, jnp.uint32), jax.ShapeDtypeStruct(shp, jnp.uint32)],
        grid=(n // t,),
        in_specs=[blk, blk],
        out_specs=[blk] * 4,
        compiler_params=_params("arbitrary"),
        name="peer_stats",
    )(s1t, s2t)


E_TILE = 1024
E_BLOCK = 2 * E_TILE
PEER_U_SCALE = 32.0
PEER_ACT_SCALE = 32.0


def _gelu_tanh_x2(x):
    c0 = math.sqrt(2.0 / math.pi)
    return x + x * jnp.tanh(x * (c0 + (c0 * 0.044715) * (x * x)))


ROWS_PER_TILE = E_TILE // N_KEYS


def _rows_bf16(word_row):
    words = word_row | jnp.zeros((SUBLANES, word_row.shape[1]), jnp.uint32)
    tile = pltpu.bitcast(words, BF16)
    return jnp.concatenate([tile] * (N_KEYS // tile.shape[0]), axis=0)


def _peer_mix_kernel(ht_ref, u_ref, vt_ref, r2_ref, a2_ref, nsp_ref, ccp_ref, nsc_ref, ccc_ref, x_ref, ga_ref,
                     gf_ref, o_ref, st0_ref, st1_ref, act0_ref, act1_ref, acc_ref, *, nblk, final_norm):
    m = pl.program_id(1)

    def scores(half, st_ref):
        st_ref[...] = (jnp.dot(u_ref[half * E_TILE:(half + 1) * E_TILE, :], ht_ref[...],
                               preferred_element_type=F32) * (1.0 / PEER_U_SCALE)).astype(BF16)

    def gate(st_ref, act_ref, ns_ref, cc_ref):
        for e in range(ROWS_PER_TILE):
            g = None
            for h in range(PEER_HEADS):
                nb = _rows_bf16(ns_ref[h, e:e + 1, :])
                cb = _rows_bf16(cc_ref[h, e:e + 1, :])
                a2 = a2_ref[h]
                term = jnp.where(r2_ref[h] < nb, a2, jnp.zeros_like(a2)) * cb
                g = term if g is None else g + term
            rows = slice(e * N_KEYS, (e + 1) * N_KEYS)
            act_ref[rows, :] = (_gelu_tanh_x2(st_ref[rows, :]) * g * PEER_ACT_SCALE).astype(F8)

    def accumulate(half, act_ref):
        acc_ref[...] += jnp.dot(vt_ref[:, half * E_TILE:(half + 1) * E_TILE], act_ref[...],
                                preferred_element_type=F32)

    @pl.when(m == 0)
    def _():
        acc_ref[...] = jnp.zeros_like(acc_ref)
        scores(0, st0_ref)
        gate(st0_ref, act0_ref, nsc_ref, ccc_ref)
        scores(1, st1_ref)

    @pl.when((m > 0) & (m < nblk))
    def _():
        accumulate(0, act0_ref)
        gate(st1_ref, act1_ref, nsp_ref, ccp_ref)
        scores(0, st0_ref)
        accumulate(1, act1_ref)
        gate(st0_ref, act0_ref, nsc_ref, ccc_ref)
        scores(1, st1_ref)

    @pl.when(m == nblk)
    def _():
        accumulate(0, act0_ref)
        gate(st1_ref, act1_ref, nsp_ref, ccp_ref)
        accumulate(1, act1_ref)
        y = x_ref[...] + ga_ref[0] * (acc_ref[...].T * (1.0 / PEER_ACT_SCALE))
        if final_norm:
            y = y * lax.rsqrt(jnp.mean(y * y, axis=-1, keepdims=True) + EPS) * gf_ref[...]
        o_ref[...] = y


def _peer_mix(ht, u, vt, r2, a2, ns, cc, x2, ga, l, g_final=None):
    d, n = ht.shape
    n_exp = u.shape[0]
    t = min(512, l)
    per = l // t
    nblk = n_exp // E_BLOCK
    n_tiles = n_exp // E_TILE
    sblk = pl.BlockSpec((PEER_HEADS, N_KEYS, t), lambda i, m: (0, 0, i))
    prev = pl.BlockSpec((PEER_HEADS, ROWS_PER_TILE, t), lambda i, m: (0, jnp.maximum(2 * m - 1, 0), i))
    cur = pl.BlockSpec((PEER_HEADS, ROWS_PER_TILE, t), lambda i, m: (0, jnp.minimum(2 * m, n_tiles - 1), i))
    return pl.pallas_call(
        functools.partial(_peer_mix_kernel, nblk=nblk, final_norm=g_final is not None),
        out_shape=jax.ShapeDtypeStruct((n, d), F32),
        grid=(n // t, nblk + 1),
        in_specs=[pl.BlockSpec((d, t), lambda i, m: (0, i)),
                  pl.BlockSpec((E_BLOCK, d), lambda i, m: (jnp.minimum(m, nblk - 1), 0)),
                  pl.BlockSpec((d, E_BLOCK), lambda i, m: (0, jnp.maximum(m - 1, 0))),
                  sblk, sblk, prev, prev, cur, cur,
                  pl.BlockSpec((t, d), lambda i, m: (i, 0)),
                  pl.BlockSpec((1, 1, d), lambda i, m: (i // per, 0, 0)),
                  pl.BlockSpec((1, d), lambda i, m: (0, 0))],
        out_specs=pl.BlockSpec((t, d), lambda i, m: (i, 0)),
        scratch_shapes=[pltpu.VMEM((E_TILE, t), BF16), pltpu.VMEM((E_TILE, t), BF16),
                        pltpu.VMEM((E_TILE, t), F8), pltpu.VMEM((E_TILE, t), F8),
                        pltpu.VMEM((d, t), F32)],
        compiler_params=_params("arbitrary", "arbitrary"),
        name="peer_mix",
    )(ht, u, vt, r2, a2, ns, cc, ns, cc, x2, ga, jnp.ones((1, d), F32) if g_final is None else g_final)


def _pad_lanes(a):
    return jnp.pad(a, ((0, 0), (0, LANES - a.shape[1])))


def _token_mix_inputs(p, proj_hy, l):
    fs, fd, nyq = _hy_filter(l, p['hf_w1'], p['hf_b1'], p['hf_w2'], p['hf_b2'], p['hf_w3'], p['hf_freq'])
    return _hy_conv(proj_hy, p['hy_conv_w'], p['hy_conv_b'], fs, fd, nyq, p['hy_bias'])


def _peer(x, p, sh, sc, ga, g_final=None):
    b, l, d = x.shape
    x2 = x.reshape(b * l, d)
    ht, s1t, s2t = _peer_q(x2, p['g_norm2'], sh, sc, p['wq'], p['k1'], p['k2'], l)
    r2, a2, ns, cc = _peer_stats(s1t, s2t)
    out = _peer_mix(ht, p['u'], p['vt'], r2, a2, ns, cc, x2, ga, l, g_final)
    return out.reshape(b, l, d)


def _layer(xl, xc, mod_l, mod_c, p, ctx_out, g_final=None):
    b = xl.shape[0]
    d = xl.shape[2]
    sh1, sc1, ga1, sh2, sc2, ga2 = [m.reshape(b, 1, d) for m in jnp.split(mod_l, 6, axis=-1)]
    csh1, csc1, cga1, csh2, csc2, cga2 = [jnp.broadcast_to(m.reshape(1, 1, d), (b, 1, d))
                                          for m in jnp.split(mod_c, 6, axis=-1)]
    w_in = p['w_in']
    w_hy, w_z, w_xbc, w_fn = (w_in[:, OFF_HY:OFF_Z], w_in[:, OFF_Z:OFF_XBC], w_in[:, OFF_XBC:OFF_DT],
                              w_in[:, OFF_FN:D_IN_PROJ])
    w_dt = _pad_lanes(w_in[:, OFF_DT:OFF_FN])
    g1 = p['g_norm1']
    pl_hy, pl_z, pl_xbc, pl_dt, pl_fn = _in_proj(xl, g1, sh1, sc1, [w_hy, w_z, w_xbc, w_dt, w_fn])
    if ctx_out:
        pc_hy, pc_z, pc_xbc, pc_dt, pc_fn = _in_proj(xc, g1, csh1, csc1, [w_hy, w_z, w_xbc, w_dt, w_fn])
    else:
        pc_xbc, pc_dt = _in_proj(xc, g1, csh1, csc1, [w_xbc, w_dt])

    xbc_c = _dwconv(pc_xbc, p['ssd_conv_w'], p['ssd_conv_b'], act=True)
    xbc_l = _dwconv(pl_xbc, p['ssd_conv_w'], p['ssd_conv_b'], act=True)
    dtb, arow = p['dtb_row'], p['a_row']
    h0 = jnp.zeros((b, SSD_GROUPS, SSD_STATE, SSD_INNER // SSD_GROUPS), F32)
    yc_f, yc_b, hc_f, hc_b = _ssd_scan(xbc_c, pc_dt, dtb, arow, h0, h0)
    yl_f, yl_b, _, _ = _ssd_scan(xbc_l, pl_dt, dtb, arow, hc_f, hc_b)

    def token_mix(x, ga, proj_hy, y_f, y_b, xbc, z, proj_fn):
        y_hy = _token_mix_inputs(p, proj_hy, x.shape[1])
        y_fn = _fnet(proj_fn)
        return _out_proj(x, ga, y_hy, y_f, y_b, xbc, z, y_fn, p['dsk_row'], p['ng_row'], p['wo1'], p['wo2'], p['wo3'])

    xl = token_mix(xl, ga1, pl_hy, yl_f, yl_b, xbc_l, pl_z, pl_fn)
    xl = _peer(xl, p, sh2, sc2, ga2, g_final)
    if ctx_out:
        xc = token_mix(xc, cga1, pc_hy, yc_f, yc_b, xbc_c, pc_z, pc_fn)
        xc = _peer(xc, p, csh2, csc2, cga2)
    return xl, xc


def kernel(x, c, ctx, c_ctx, w_ada, b_ada, g_norm1, g_norm2, w_in, hy_conv_w, hy_conv_b, hf_w1, hf_b1, hf_w2, hf_b2, hf_w3, hf_freq, hy_bias, ssd_conv_w, ssd_conv_b, ssd_dt_bias, ssd_a_log, ssd_d, ssd_norm_g, w_out, peer_wq, peer_k1, peer_k2, peer_u, peer_v, g_final):
    depth = w_ada.shape[0]
    b, l, d = x.shape
    rows = -(-(b + 1) // SUBLANES) * SUBLANES
    cc = jnp.concatenate([c, c_ctx[None, :], jnp.zeros((rows - b - 1, d), F32)], axis=0)
    mods = _ada_mod(cc, w_ada, b_ada)
    xl, xc = x, ctx
    for i in range(depth):
        wo = w_out[i].astype(BF16)
        p = {
            'g_norm1': g_norm1[i].reshape(1, d), 'g_norm2': g_norm2[i].reshape(1, d),
            'w_in': w_in[i].astype(BF16),
            'hy_conv_w': hy_conv_w[i], 'hy_conv_b': hy_conv_b[i],
            'hf_w1': hf_w1[i], 'hf_b1': hf_b1[i], 'hf_w2': hf_w2[i], 'hf_b2': hf_b2[i], 'hf_w3': hf_w3[i],
            'hf_freq': hf_freq[i], 'hy_bias': hy_bias[i],
            'ssd_conv_w': ssd_conv_w[i], 'ssd_conv_b': ssd_conv_b[i],
            'dtb_row': _pad_lanes(ssd_dt_bias[i].reshape(1, 2 * SSD_HEADS)),
            'a_row': _pad_lanes(-jnp.exp(ssd_a_log[i].astype(F32)).reshape(1, 2 * SSD_HEADS)),
            'dsk_row': jnp.repeat(ssd_d[i].astype(F32), SSD_HEAD_DIM).reshape(1, SSD_INNER),
            'ng_row': ssd_norm_g[i].reshape(1, SSD_INNER),
            'wo1': wo[:HY_WIDTH], 'wo2': wo[HY_WIDTH:HY_WIDTH + SSD_INNER], 'wo3': wo[HY_WIDTH + SSD_INNER:],
            'wq': peer_wq[i].astype(BF16), 'k1': peer_k1[i].astype(BF16), 'k2': peer_k2[i].astype(BF16),
            'u': (peer_u[i] * PEER_U_SCALE).astype(F8), 'vt': peer_v[i].astype(F8).T,
        }
        last = i == depth - 1
        xl, xc = _layer(xl, xc, mods[i, :b], mods[i, b], p, not last, g_final.reshape(1, d) if last else None)
    return xl
```

```python
import functools
import math

import jax
import jax.numpy as jnp
import numpy as np
from jax import lax
from jax.experimental import pallas as pl
from jax.experimental.pallas import tpu as pltpu

F32 = jnp.float32
BF16 = jnp.bfloat16
F8 = jnp.float8_e4m3fn
HI = lax.Precision.HIGHEST

EPS = 1e-6
LANES = 128
SUBLANES = 8
VMEM_LIMIT = 56 * 1024 * 1024

HY_WIDTH = 256
HY_EMB = 33
HY_BANDS = (HY_EMB - 1) // 2
HY_DECAY_TARGET = 1e-2
HY_SHORT_DECAY_PCT = 0.3
HY_LONG_DECAY_PCT = 1.5
SSD_HEADS = 8
SSD_HEAD_DIM = 64
SSD_INNER = SSD_HEADS * SSD_HEAD_DIM
SSD_GROUPS = 2
SSD_STATE = 128
SSD_CHUNK = 128
SSD_XBC = SSD_INNER + 2 * SSD_GROUPS * SSD_STATE
FN_WIDTH = 256
FN_GROUPS = 4
FN_GROUP_DIM = FN_WIDTH // FN_GROUPS
PEER_HEADS = 8
PEER_TOPK = 16
N_KEYS = 128
PEER_HALF = 128
OFF_HY = 0
OFF_Z = OFF_HY + 3 * HY_WIDTH
OFF_XBC = OFF_Z + SSD_INNER
OFF_DT = OFF_XBC + SSD_XBC
OFF_FN = OFF_DT + 2 * SSD_HEADS
D_IN_PROJ = OFF_FN + FN_WIDTH


def _params(*sem):
    return pltpu.CompilerParams(dimension_semantics=sem, vmem_limit_bytes=VMEM_LIMIT)


def _silu(x):
    return x * jax.nn.sigmoid(x)


def _softplus(x):
    return jnp.maximum(x, 0.0) + jnp.log1p(jnp.exp(-jnp.abs(x)))


def _gelu_tanh(x):
    return 0.5 * x * (1.0 + jnp.tanh(math.sqrt(2.0 / math.pi) * (x + 0.044715 * (x * x * x))))


def _ada_kernel(c_ref, w_ref, b_ref, o_ref):
    s = _silu(c_ref[...])
    o_ref[0] = jnp.dot(s, w_ref[0], preferred_element_type=F32, precision=HI) + b_ref[0]


def _ada_mod(cc, w_ada, b_ada):
    depth, d, n = w_ada.shape
    r = cc.shape[0]
    tn = 1536
    return pl.pallas_call(
        _ada_kernel,
        out_shape=jax.ShapeDtypeStruct((depth, r, n), F32),
        grid=(depth, n // tn),
        in_specs=[pl.BlockSpec((r, d), lambda l, j: (0, 0)),
                  pl.BlockSpec((1, d, tn), lambda l, j: (l, 0, j)),
                  pl.BlockSpec((1, 1, tn), lambda l, j: (l, 0, j))],
        out_specs=pl.BlockSpec((1, r, tn), lambda l, j: (l, 0, j)),
        compiler_params=_params("arbitrary", "arbitrary"),
        name="ada_mod",
    )(cc, w_ada, b_ada.reshape(depth, 1, n))


def _normmod(x, g, sh, sc):
    y = x * lax.rsqrt(jnp.mean(x * x, axis=-1, keepdims=True) + EPS) * g
    return y * (1.0 + sc) + sh


def _inproj_kernel(x_ref, g_ref, sh_ref, sc_ref, *rest, n_w):
    w_refs, o_refs = rest[:n_w], rest[n_w:]
    hb = _normmod(x_ref[0], g_ref[...], sh_ref[0], sc_ref[0]).astype(BF16)
    for w_ref, o_ref in zip(w_refs, o_refs):
        o_ref[0] = jnp.dot(hb, w_ref[...], preferred_element_type=F32).astype(o_ref.dtype)


def _in_proj(x, g, sh, sc, ws):
    b, l, d = x.shape
    tm = min(512, l)
    return pl.pallas_call(
        functools.partial(_inproj_kernel, n_w=len(ws)),
        out_shape=[jax.ShapeDtypeStruct((b, l, w.shape[1]), F32) for w in ws],
        grid=(b, l // tm),
        in_specs=[pl.BlockSpec((1, tm, d), lambda i, j: (i, j, 0)),
                  pl.BlockSpec((1, d), lambda i, j: (0, 0)),
                  pl.BlockSpec((1, 1, d), lambda i, j: (i, 0, 0)),
                  pl.BlockSpec((1, 1, d), lambda i, j: (i, 0, 0))]
        + [pl.BlockSpec(w.shape, lambda i, j: (0, 0)) for w in ws],
        out_specs=[pl.BlockSpec((1, tm, w.shape[1]), lambda i, j: (i, j, 0)) for w in ws],
        compiler_params=_params("arbitrary", "arbitrary"),
        name="in_proj",
    )(x, g, sh, sc, *ws)


def _conv3(u, w, bias):
    l = u.shape[0]
    row = lax.broadcasted_iota(jnp.int32, u.shape, 0)
    um = jnp.where(row == 0, 0.0, pltpu.roll(u, 1, 0))
    up = jnp.where(row == l - 1, 0.0, pltpu.roll(u, l - 1, 0))
    return um * w[0:1, :] + u * w[1:2, :] + up * w[2:3, :] + bias


def _dwconv_kernel(u_ref, w_ref, b_ref, o_ref, *, act):
    y = _conv3(u_ref[0], w_ref[...], b_ref[...])
    if act:
        y = _silu(y)
    o_ref[0] = y


def _dwconv(u, w, bias, act):
    b, l, c = u.shape
    cb = 256
    return pl.pallas_call(
        functools.partial(_dwconv_kernel, act=act),
        out_shape=jax.ShapeDtypeStruct((b, l, c), F32),
        grid=(b, c // cb),
        in_specs=[pl.BlockSpec((1, l, cb), lambda i, j: (i, 0, j)),
                  pl.BlockSpec((3, cb), lambda i, j: (0, j)),
                  pl.BlockSpec((1, cb), lambda i, j: (0, j))],
        out_specs=pl.BlockSpec((1, l, cb), lambda i, j: (i, 0, j)),
        compiler_params=_params("arbitrary", "arbitrary"),
        name="dwconv",
    )(u, w, bias.reshape(1, c))


def _split3(x):
    hi = x.astype(BF16)
    r1 = x - hi.astype(F32)
    mid = r1.astype(BF16)
    lo = (r1 - mid.astype(F32)).astype(BF16)
    return hi, mid, lo


def _ssd_chunk(xbc, dt_raw, dtb, arow, st_ref, direction):
    q = SSD_CHUNK
    x = xbc[:, :SSD_INNER]
    gn = SSD_GROUPS * SSD_STATE
    bm = xbc[:, SSD_INNER:SSD_INNER + gn].astype(BF16)
    cm = xbc[:, SSD_INNER + gn:].astype(BF16)
    dtp = _softplus(dt_raw + dtb)
    da = dtp * arow
    ii = lax.broadcasted_iota(jnp.int32, (q, q), 0)
    jj = lax.broadcasted_iota(jnp.int32, (q, q), 1)
    tri = (ii >= jj) if direction == 0 else (ii <= jj)
    tri_b = tri.astype(BF16)
    cs = jnp.dot(jnp.concatenate([tri_b] * 3, axis=1), jnp.concatenate(_split3(da), axis=0),
                 preferred_element_type=F32)
    cs_t = cs.T
    lane = lax.broadcasted_iota(jnp.int32, (LANES, SSD_INNER), 0)
    chan = lax.broadcasted_iota(jnp.int32, (LANES, SSD_INNER), 1)
    head_of = lax.shift_right_logical(chan, int(math.log2(SSD_HEAD_DIM)))
    expand = (lane == direction * SSD_HEADS + head_of).astype(BF16)
    expand3 = jnp.concatenate([expand] * 3, axis=0)
    both = jnp.concatenate([jnp.concatenate(_split3(cs), axis=1), jnp.concatenate(_split3(dtp), axis=1)], axis=0)
    wide = jnp.dot(both, expand3, preferred_element_type=F32)
    acum, dtf = wide[:q], wide[q:]
    tot = acum[q - 1:q, :] if direction == 0 else acum[0:1, :]
    xdt = x * dtf
    ea = jnp.exp(acum)
    xdec = (jnp.exp(tot - acum) * xdt).astype(BF16)
    cd = jnp.exp(tot)
    xdt_b = xdt.astype(BF16)
    hg = SSD_HEADS // SSD_GROUPS
    gw = hg * SSD_HEAD_DIM
    outs = []
    for g in range(SSD_GROUPS):
        bg = bm[:, g * SSD_STATE:(g + 1) * SSD_STATE]
        cg = cm[:, g * SSD_STATE:(g + 1) * SSD_STATE]
        cb = lax.dot_general(cg, bg, (((1,), (1,)), ((), ())), preferred_element_type=F32)
        st = st_ref[g]
        y_g = jnp.dot(cg, st.astype(BF16), preferred_element_type=F32) * ea[:, g * gw:(g + 1) * gw]
        xg = xdt_b[:, g * gw:(g + 1) * gw]
        head_g = lax.shift_right_logical(lax.broadcasted_iota(jnp.int32, xg.shape, 1), int(math.log2(SSD_HEAD_DIM)))
        for k in range(hg):
            ln = direction * SSD_HEADS + g * hg + k
            seg = cs[:, ln:ln + 1] - cs_t[ln:ln + 1, :]
            lmat = jnp.where(tri, jnp.exp(jnp.where(tri, seg, 0.0)), 0.0)
            m = (cb * lmat).astype(BF16)
            y_g = y_g + jnp.dot(m, jnp.where(head_g == k, xg, jnp.zeros_like(xg)), preferred_element_type=F32)
        outs.append(y_g)
        sg = lax.dot_general(bg, xdec[:, g * gw:(g + 1) * gw], (((0,), (0,)), ((), ())),
                             preferred_element_type=F32)
        st_ref[g] = st * cd[:, g * gw:(g + 1) * gw] + sg
    return jnp.concatenate(outs, axis=1)


def _ssd_kernel(xf_ref, dtf_ref, xb_ref, dtb_ref, bias_ref, arow_ref, h0f_ref, h0b_ref,
                yf_ref, yb_ref, hlf_ref, hlb_ref, stf_ref, stb_ref, *, nc):
    c = pl.program_id(1)

    @pl.when(c == 0)
    def _():
        stf_ref[...] = h0f_ref[0]
        stb_ref[...] = h0b_ref[0]

    yf_ref[0] = _ssd_chunk(xf_ref[0], dtf_ref[0], bias_ref[...], arow_ref[...], stf_ref, 0)
    yb_ref[0] = _ssd_chunk(xb_ref[0], dtb_ref[0], bias_ref[...], arow_ref[...], stb_ref, 1)

    @pl.when(c == nc - 1)
    def _():
        hlf_ref[0] = stf_ref[...]
        hlb_ref[0] = stb_ref[...]


def _ssd_scan(xbc, dt, dtb_row, a_row, h0f, h0b):
    b, l, _ = xbc.shape
    nc = l // SSD_CHUNK
    fwd = lambda i, c: (i, c, 0)
    bwd = lambda i, c: (i, nc - 1 - c, 0)
    st_shape = (SSD_GROUPS, SSD_STATE, SSD_INNER // SSD_GROUPS)
    st_spec = pl.BlockSpec((1,) + st_shape, lambda i, c: (i, 0, 0, 0))
    row = pl.BlockSpec((1, LANES), lambda i, c: (0, 0))
    return pl.pallas_call(
        functools.partial(_ssd_kernel, nc=nc),
        out_shape=[jax.ShapeDtypeStruct((b, l, SSD_INNER), F32)] * 2
        + [jax.ShapeDtypeStruct((b,) + st_shape, F32)] * 2,
        grid=(b, nc),
        in_specs=[pl.BlockSpec((1, SSD_CHUNK, SSD_XBC), fwd), pl.BlockSpec((1, SSD_CHUNK, LANES), fwd),
                  pl.BlockSpec((1, SSD_CHUNK, SSD_XBC), bwd), pl.BlockSpec((1, SSD_CHUNK, LANES), bwd),
                  row, row, st_spec, st_spec],
        out_specs=[pl.BlockSpec((1, SSD_CHUNK, SSD_INNER), fwd), pl.BlockSpec((1, SSD_CHUNK, SSD_INNER), bwd),
                   st_spec, st_spec],
        scratch_shapes=[pltpu.VMEM(st_shape, F32), pltpu.VMEM(st_shape, F32)],
        compiler_params=_params("arbitrary", "arbitrary"),
        name="ssd_scan",
    )(xbc, dt, xbc, dt, dtb_row, a_row, h0f, h0b)


def _hy_filter_kernel(z_ref, w1_ref, b1_ref, w2_ref, b2_ref, w3_ref, fr_ref, win_ref, alt_ref,
                      fs_ref, fd_ref, nyq_ref):
    fr = fr_ref[...]
    h = jnp.sin(fr * (jnp.dot(z_ref[...], w1_ref[...], preferred_element_type=F32, precision=HI) + b1_ref[...]))
    h = jnp.sin(fr * (jnp.dot(h, w2_ref[...], preferred_element_type=F32, precision=HI) + b2_ref[...]))
    h = jnp.dot(h, w3_ref[...], preferred_element_type=F32, precision=HI)
    win = win_ref[...]
    row = lax.broadcasted_iota(jnp.int32, win.shape, 0)
    hf = h[:, :HY_WIDTH] * win
    hb = jnp.where(row == 0, 0.0, h[:, HY_WIDTH:] * win)
    nrm = lax.rsqrt(jnp.sum(hf * hf, axis=0, keepdims=True) + jnp.sum(hb * hb, axis=0, keepdims=True) + EPS)
    fs = (hf + hb) * nrm
    fs_ref[...] = fs
    fd_ref[...] = (hb - hf) * nrm
    nyq_ref[...] = jnp.sum(fs * alt_ref[...], axis=0, keepdims=True)


def _hy_filter(l, w1, b1, w2, b2, w3, freq):
    t = np.linspace(0.0, 1.0, l, dtype=np.float32)[:, None]
    w = (np.float32(2.0 * math.pi) * np.arange(l, dtype=np.float32)[:, None] / np.float32(l)).astype(np.float32)
    f = np.linspace(1e-4, HY_BANDS - 1, HY_BANDS, dtype=np.float32)[None, :]
    zf = jnp.asarray(f) * jnp.asarray(w)
    z = jnp.concatenate([jnp.asarray(t), jnp.cos(zf), -jnp.sin(zf)], axis=-1)
    z = jnp.pad(z, ((0, 0), (0, LANES - HY_EMB)))
    hid = w1.shape[1]
    w1p = jnp.pad(w1, ((0, LANES - HY_EMB), (0, LANES - hid)))
    w2p = jnp.pad(w2, ((0, LANES - hid), (0, LANES - hid)))
    w3p = jnp.pad(w3, ((0, LANES - hid), (0, 0)))
    pad_row = lambda v: jnp.pad(v.reshape(1, hid), ((0, 0), (0, LANES - hid)))
    max_decay = math.log(HY_DECAY_TARGET) / HY_SHORT_DECAY_PCT
    min_decay = math.log(HY_DECAY_TARGET) / HY_LONG_DECAY_PCT
    deltas = jnp.abs(jnp.linspace(min_decay, max_decay, HY_WIDTH, dtype=F32))
    win = jnp.exp(-jnp.asarray(t) * deltas)
    alt = jnp.asarray(np.where(np.arange(l) % 2 == 0, 1.0, -1.0).astype(np.float32)[:, None] * np.ones((1, HY_WIDTH), np.float32))
    return pl.pallas_call(
        _hy_filter_kernel,
        out_shape=[jax.ShapeDtypeStruct((l, HY_WIDTH), F32), jax.ShapeDtypeStruct((l, HY_WIDTH), F32),
                   jax.ShapeDtypeStruct((1, HY_WIDTH), F32)],
        compiler_params=pltpu.CompilerParams(vmem_limit_bytes=VMEM_LIMIT),
        name="hy_filter",
    )(z, w1p, pad_row(b1), w2p, pad_row(b2), w3p, pad_row(freq), win, alt)


def _mm_kernel(a_ref, b_ref, o_ref):
    o_ref[...] = jnp.dot(a_ref[...], b_ref[...], preferred_element_type=F32)


def _mm(a, b, tm):
    m, k = a.shape
    n = b.shape[1]
    return pl.pallas_call(
        _mm_kernel,
        out_shape=jax.ShapeDtypeStruct((m, n), F32),
        grid=(m // tm,),
        in_specs=[pl.BlockSpec((tm, k), lambda i: (i, 0)), pl.BlockSpec((k, n), lambda i: (0, 0))],
        out_specs=pl.BlockSpec((tm, n), lambda i: (i, 0)),
        compiler_params=_params("arbitrary"),
        name="mm",
    )(a, b)


def _dft_tables(l):
    n = 2 * l
    k = np.arange(l, dtype=np.int64)[:, None]
    s = np.arange(l, dtype=np.int64)[None, :]
    ang = (2.0 * np.pi / n) * ((k * s) % n).astype(np.float64)
    cos, sin = np.cos(ang), np.sin(ang)
    alt = np.where(np.arange(l) % 2 == 0, 1.0, -1.0)
    sin[0, :] = alt
    fwd = np.concatenate([cos, sin], axis=0)
    wgt = np.full((l, 1), 2.0)
    wgt[0, 0] = 1.0
    sin_i = -sin * wgt
    sin_i[0, :] = alt
    inv = np.concatenate([(cos * wgt).T, sin_i.T], axis=1) / n
    return fwd, inv


def _hy_conv_kernel(u_ref, cw_ref, cb_ref, ff_ref, fi_ref, ka_ref, kb_ref, kc_ref, bias_ref, o_ref,
                    ub_ref, uf_ref, x1_ref, acc_ref, *, nf):
    f = pl.program_id(1)
    w = HY_WIDTH

    @pl.when(f == 0)
    def _():
        part = lambda k: _conv3(u_ref[0, :, k * w:(k + 1) * w], cw_ref[:, k * w:(k + 1) * w],
                                cb_ref[:, k * w:(k + 1) * w])
        u = part(2) * part(0)
        uf_ref[...] = u
        ub_ref[...] = u.astype(BF16)
        x1_ref[...] = part(1)
        acc_ref[...] = jnp.zeros_like(acc_ref)

    uf = jnp.dot(ff_ref[0], ub_ref[...], preferred_element_type=F32)
    fb = uf.shape[0] // 2
    ur, us = uf[:fb], uf[fb:]
    ka, kb, kc = ka_ref[...], kb_ref[...], kc_ref[...]
    p = jnp.concatenate([ur * ka + us * kb, ur * kb - us * kc], axis=0).astype(BF16)
    acc_ref[...] += jnp.dot(fi_ref[0], p, preferred_element_type=F32)

    @pl.when(f == nf - 1)
    def _():
        o_ref[0] = x1_ref[...] * (acc_ref[...] + uf_ref[...] * bias_ref[...])


def _hy_conv(proj, conv_w, conv_b, fs, fd, nyq, bias):
    b, l, _ = proj.shape
    fwd, inv = _dft_tables(l)
    fb = min(512, l)
    nf = l // fb
    ff = jnp.asarray(np.stack([np.concatenate([fwd[i * fb:(i + 1) * fb], fwd[l + i * fb:l + (i + 1) * fb]], axis=0)
                               for i in range(nf)]), dtype=BF16)
    fi = jnp.asarray(np.stack([np.concatenate([inv[:, i * fb:(i + 1) * fb], inv[:, l + i * fb:l + (i + 1) * fb]], axis=1)
                               for i in range(nf)]), dtype=BF16)
    fwd_b = jnp.asarray(fwd, dtype=BF16)
    tm = min(512, l)
    k_r = _mm(fwd_b[:l], fs.astype(BF16), tm)
    k_i = _mm(fwd_b[l:], fd.astype(BF16), tm)
    first = (jnp.arange(l) == 0)[:, None]
    ka = k_r
    kb = jnp.where(first, 0.0, k_i)
    kc = jnp.where(first, -nyq, k_r)
    return pl.pallas_call(
        functools.partial(_hy_conv_kernel, nf=nf),
        out_shape=jax.ShapeDtypeStruct((b, l, HY_WIDTH), F32),
        grid=(b, nf),
        in_specs=[pl.BlockSpec((1, l, 3 * HY_WIDTH), lambda i, f: (i, 0, 0)),
                  pl.BlockSpec((3, 3 * HY_WIDTH), lambda i, f: (0, 0)),
                  pl.BlockSpec((1, 3 * HY_WIDTH), lambda i, f: (0, 0)),
                  pl.BlockSpec((1, 2 * fb, l), lambda i, f: (f, 0, 0)),
                  pl.BlockSpec((1, l, 2 * fb), lambda i, f: (f, 0, 0)),
                  pl.BlockSpec((fb, HY_WIDTH), lambda i, f: (f, 0)),
                  pl.BlockSpec((fb, HY_WIDTH), lambda i, f: (f, 0)),
                  pl.BlockSpec((fb, HY_WIDTH), lambda i, f: (f, 0)),
                  pl.BlockSpec((1, HY_WIDTH), lambda i, f: (0, 0))],
        out_specs=pl.BlockSpec((1, l, HY_WIDTH), lambda i, f: (i, 0, 0)),
        scratch_shapes=[pltpu.VMEM((l, HY_WIDTH), BF16), pltpu.VMEM((l, HY_WIDTH), F32),
                        pltpu.VMEM((l, HY_WIDTH), F32), pltpu.VMEM((l, HY_WIDTH), F32)],
        compiler_params=_params("arbitrary", "arbitrary"),
        name="hy_conv",
    )(proj, conv_w, conv_b.reshape(1, 3 * HY_WIDTH), ff, fi, ka, kb, kc, bias.reshape(1, HY_WIDTH))


def _fnet_kernel(x_ref, cc_ref, sc_ref, m_ref, o_ref, xx_ref):
    r = pl.program_id(1)
    l = x_ref.shape[1]

    @pl.when(r == 0)
    def _():
        xb = x_ref[0].astype(BF16)
        xx_ref[:l, :] = jnp.dot(xb, cc_ref[...], preferred_element_type=F32).astype(BF16)
        xx_ref[l:, :] = jnp.dot(xb, sc_ref[...], preferred_element_type=F32).astype(BF16)

    o_ref[0] = jnp.dot(m_ref[...], xx_ref[...], preferred_element_type=F32)


def _fnet(x):
    b, l, c = x.shape
    gd = FN_GROUP_DIM
    kc = np.arange(gd)[:, None] * np.arange(gd)[None, :]
    ang_c = 2.0 * np.pi * (kc % gd) / gd
    eye = np.eye(FN_GROUPS)
    cblk = jnp.asarray(np.kron(eye, np.cos(ang_c)), dtype=BF16)
    sblk = jnp.asarray(np.kron(eye, np.sin(ang_c)), dtype=BF16)
    kl = (np.arange(l, dtype=np.int64)[:, None] * np.arange(l, dtype=np.int64)[None, :]) % l
    ang_l = 2.0 * np.pi * kl / l
    scale = 1.0 / math.sqrt(l * gd)
    mat = jnp.asarray(np.concatenate([np.cos(ang_l), -np.sin(ang_l)], axis=1) * scale, dtype=BF16)
    tr = min(512, l)
    return pl.pallas_call(
        _fnet_kernel,
        out_shape=jax.ShapeDtypeStruct((b, l, c), F32),
        grid=(b, l // tr),
        in_specs=[pl.BlockSpec((1, l, c), lambda i, r: (i, 0, 0)),
                  pl.BlockSpec((c, c), lambda i, r: (0, 0)),
                  pl.BlockSpec((c, c), lambda i, r: (0, 0)),
                  pl.BlockSpec((tr, 2 * l), lambda i, r: (r, 0))],
        out_specs=pl.BlockSpec((1, tr, c), lambda i, r: (i, r, 0)),
        scratch_shapes=[pltpu.VMEM((2 * l, c), BF16)],
        compiler_params=_params("arbitrary", "arbitrary"),
        name="fnet",
    )(x, cblk, sblk, mat)


def _outproj_kernel(x_ref, ga_ref, yhy_ref, yf_ref, yb_ref, xs_ref, z_ref, yfn_ref, dsk_ref, ng_ref,
                    w1_ref, w2_ref, w3_ref, o_ref):
    y = yf_ref[0] + yb_ref[0] + dsk_ref[...] * xs_ref[0]
    y = y * _silu(z_ref[0])
    gw = SSD_INNER // SSD_GROUPS
    parts = []
    for g in range(SSD_GROUPS):
        yg = y[:, g * gw:(g + 1) * gw]
        parts.append(yg * lax.rsqrt(jnp.mean(yg * yg, axis=-1, keepdims=True) + EPS))
    yn = jnp.concatenate(parts, axis=1) * ng_ref[...]
    out = jnp.dot(yhy_ref[0].astype(BF16), w1_ref[...], preferred_element_type=F32)
    out += jnp.dot(yn.astype(BF16), w2_ref[...], preferred_element_type=F32)
    out += jnp.dot(yfn_ref[0].astype(BF16), w3_ref[...], preferred_element_type=F32)
    o_ref[0] = x_ref[0] + ga_ref[0] * out


def _out_proj(x, ga, y_hy, y_f, y_b, xbc, z, y_fn, dsk_row, ng_row, w1, w2, w3):
    b, l, d = x.shape
    tm = min(512, l)
    tok = lambda n: pl.BlockSpec((1, tm, n), lambda i, j: (i, j, 0))
    const = lambda a: pl.BlockSpec(a.shape, lambda i, j: (0,) * a.ndim)
    return pl.pallas_call(
        _outproj_kernel,
        out_shape=jax.ShapeDtypeStruct((b, l, d), F32),
        grid=(b, l // tm),
        in_specs=[tok(d), pl.BlockSpec((1, 1, d), lambda i, j: (i, 0, 0)), tok(HY_WIDTH), tok(SSD_INNER),
                  tok(SSD_INNER), tok(SSD_INNER), tok(SSD_INNER), tok(FN_WIDTH), const(dsk_row), const(ng_row),
                  const(w1), const(w2), const(w3)],
        out_specs=tok(d),
        compiler_params=_params("arbitrary", "arbitrary"),
        name="out_proj",
    )(x, ga, y_hy, y_f, y_b, xbc, z, y_fn, dsk_row, ng_row, w1, w2, w3)


def _peer_scores(x_ref, g_ref, sh_ref, sc_ref, wq_ref, k1_ref, k2_ref, ht_ref, s1_ref, s2_ref):
    hm = _normmod(x_ref[...], g_ref[...], sh_ref[0], sc_ref[0])
    ht_ref[...] = hm.T.astype(F8)
    qb = jnp.dot(hm.astype(BF16), wq_ref[...], preferred_element_type=F32).astype(BF16)
    nt = (((1,), (1,)), ((), ()))
    for h in range(PEER_HEADS):
        q1 = qb[:, (2 * h) * PEER_HALF:(2 * h + 1) * PEER_HALF]
        q2 = qb[:, (2 * h + 1) * PEER_HALF:(2 * h + 2) * PEER_HALF]
        s1_ref[h] = lax.dot_general(k1_ref[...], q1, nt, preferred_element_type=F32)
        s2_ref[h] = lax.dot_general(k2_ref[...], q2, nt, preferred_element_type=F32)


def _sorting_network(n):
    pairs = []

    def merge(lo, m, r):
        step = 2 * r
        if step < m:
            merge(lo, m, step)
            merge(lo + r, m, step)
            for i in range(lo + r, lo + m - r, step):
                pairs.append((i, i + r))
        else:
            pairs.append((lo, lo + r))

    def sort(lo, m):
        if m > 1:
            half = m // 2
            sort(lo, half)
            sort(lo + half, half)
            merge(lo, m, 1)

    sort(0, n)
    return tuple(pairs)


NEG = -3.0e38
N_TOP = PEER_TOPK + 1


def _top_sorted(rows, n_out):
    rows = list(rows)
    for i, j in _sorting_network(len(rows)):
        rows[i], rows[j] = jnp.maximum(rows[i], rows[j]), jnp.minimum(rows[i], rows[j])
    sub = lax.broadcasted_iota(jnp.int32, rows[0].shape, 0)
    out = []
    for r in range(n_out):
        m = jnp.max(rows[0], axis=0, keepdims=True)
        out.append(m)
        first = jnp.min(jnp.where(rows[0] == m, sub, SUBLANES), axis=0, keepdims=True)
        hit = sub == first
        keep = min(len(rows), n_out - r - 1)
        rows = [jnp.where(hit, rows[k + 1] if k + 1 < len(rows) else NEG, rows[k]) for k in range(keep)]
    return out


def _count_above(vals, x, strict):
    r = jnp.zeros_like(x)
    for j, v in enumerate(vals):
        r = jnp.where((v > x) if strict else (v >= x), float(j + 1), r)
    return r


def _dup_bf16(x):
    hi = pltpu.bitcast(x.astype(BF16).astype(F32), jnp.uint32)
    return hi | (hi >> 16)


def _peer_stats_kernel(s1_ref, s2_ref, r2_ref, a2_ref, ns_ref, cc_ref):
    nrow = N_KEYS // SUBLANES
    t = s1_ref.shape[2]
    sub = lax.broadcasted_iota(jnp.int32, (SUBLANES, t), 0)
    for h in range(PEER_HEADS):
        s1_rows = [s1_ref[h, SUBLANES * k:SUBLANES * (k + 1), :] for k in range(nrow)]
        s2_rows = [s2_ref[h, SUBLANES * k:SUBLANES * (k + 1), :] for k in range(nrow)]
        v1 = _top_sorted(s1_rows, N_TOP)
        v2 = _top_sorted(s2_rows, N_TOP)
        cands = [v1[i] + v2[j] for i in range(N_TOP) for j in range(N_TOP) if (i + 1) * (j + 1) <= N_TOP]
        packed = []
        for k in range(0, len(cands), SUBLANES):
            blk = jnp.full((SUBLANES, t), NEG, F32)
            for s, cv in enumerate(cands[k:k + SUBLANES]):
                blk = jnp.where(sub == s, cv, blk)
            packed.append(blk)
        ids = [sub + SUBLANES * k for k in range(len(packed))]
        big = SUBLANES * len(packed)
        top = []
        for r in range(N_TOP):
            m = packed[0]
            for blk in packed[1:]:
                m = jnp.maximum(m, blk)
            m = jnp.max(m, axis=0, keepdims=True)
            top.append(m)
            sel = jnp.where(packed[0] == m, ids[0], big)
            for blk, idk in zip(packed[1:], ids[1:]):
                sel = jnp.minimum(sel, jnp.where(blk == m, idk, big))
            sel = jnp.min(sel, axis=0, keepdims=True)
            packed = [jnp.where(idk == sel, NEG, blk) for blk, idk in zip(packed, ids)]
        mx = top[0]
        z = jnp.zeros((1, t), F32)
        for r in range(PEER_TOPK):
            z = z + jnp.exp(top[r] - mx)
        tau = 0.5 * (top[PEER_TOPK - 1] + top[PEER_TOPK])
        cscale = 0.5 / z
        for k in range(0, nrow, 2):
            pair = slice(SUBLANES * k, SUBLANES * (k + 2))
            r2 = [_count_above(v2[:PEER_TOPK], s2_rows[k + q], True) for q in range(2)]
            r2_ref[h, pair, :] = jnp.concatenate(r2, axis=0).astype(BF16)
            a2 = [jnp.exp(s2_rows[k + q] - v2[0]) for q in range(2)]
            a2_ref[h, pair, :] = jnp.concatenate(a2, axis=0).astype(BF16)
        for k in range(nrow):
            blk = slice(SUBLANES * k, SUBLANES * (k + 1))
            ns_ref[h, blk, :] = _dup_bf16(_count_above(v2[:PEER_TOPK], tau - s1_rows[k], False))
            cc_ref[h, blk, :] = _dup_bf16(jnp.exp(s1_rows[k] - v1[0]) * cscale)


def _peer_select_kernel(x_ref, g_ref, sh_ref, sc_ref, wq_ref, k1_ref, k2_ref,
                        ht_ref, r2_ref, a2_ref, ns_ref, cc_ref, s1_ref, s2_ref):
    _peer_scores(x_ref, g_ref, sh_ref, sc_ref, wq_ref, k1_ref, k2_ref, ht_ref, s1_ref, s2_ref)
    _peer_stats_kernel(s1_ref, s2_ref, r2_ref, a2_ref, ns_ref, cc_ref)


def _peer_select(x2, g, sh, sc, wq, k1, k2, l):
    n, d = x2.shape
    tm = min(256, l)
    per = l // tm
    shp = (PEER_HEADS, N_KEYS, n)
    blk = pl.BlockSpec((PEER_HEADS, N_KEYS, tm), lambda i: (0, 0, i))
    return pl.pallas_call(
        _peer_select_kernel,
        out_shape=[jax.ShapeDtypeStruct((d, n), F8),
                   jax.ShapeDtypeStruct(shp, BF16), jax.ShapeDtypeStruct(shp, BF16),
                   jax.ShapeDtypeStruct(shp, jnp.uint32), jax.ShapeDtypeStruct(shp, jnp.uint32)],
        grid=(n // tm,),
        in_specs=[pl.BlockSpec((tm, d), lambda i: (i, 0)),
                  pl.BlockSpec((1, d), lambda i: (0, 0)),
                  pl.BlockSpec((1, 1, d), lambda i: (i // per, 0, 0)),
                  pl.BlockSpec((1, 1, d), lambda i: (i // per, 0, 0)),
                  pl.BlockSpec(wq.shape, lambda i: (0, 0)),
                  pl.BlockSpec(k1.shape, lambda i: (0, 0)),
                  pl.BlockSpec(k2.shape, lambda i: (0, 0))],
        out_specs=[pl.BlockSpec((d, tm), lambda i: (0, i)), blk, blk, blk, blk],
        scratch_shapes=[pltpu.VMEM((PEER_HEADS, N_KEYS, tm), F32), pltpu.VMEM((PEER_HEADS, N_KEYS, tm), F32)],
        compiler_params=_params("arbitrary"),
        name="peer_select",
    )(x2, g, sh, sc, wq, k1, k2)


E_TILE = 1024
E_BLOCK = 2 * E_TILE
PEER_SCALE = 32.0


def _gelu_tanh_x2_scaled(s):
    c0 = math.sqrt(2.0 / math.pi) / PEER_SCALE
    c1 = math.sqrt(2.0 / math.pi) * 0.044715 / PEER_SCALE ** 3
    return s + s * jnp.tanh(s * (c0 + c1 * (s * s)))


ROWS_PER_TILE = E_TILE // N_KEYS


def _rows_bf16(word_row):
    words = word_row | jnp.zeros((SUBLANES, word_row.shape[1]), jnp.uint32)
    tile = pltpu.bitcast(words, BF16)
    return jnp.concatenate([tile] * (N_KEYS // tile.shape[0]), axis=0)


def _peer_mix_kernel(ht_ref, u_ref, vt_ref, r2_ref, a2_ref, nsp_ref, ccp_ref, nsc_ref, ccc_ref, x_ref, ga_ref,
                     gf_ref, o_ref, st0_ref, st1_ref, act0_ref, act1_ref, acc_ref, *, nblk, final_norm):
    m = pl.program_id(1)

    def scores(half, st_ref):
        st_ref[...] = jnp.dot(u_ref[half * E_TILE:(half + 1) * E_TILE, :], ht_ref[...],
                              preferred_element_type=F32).astype(BF16)

    def gate(st_ref, act_ref, ns_ref, cc_ref):
        for e in range(ROWS_PER_TILE):
            g = None
            for h in range(PEER_HEADS):
                nb = _rows_bf16(ns_ref[h, e:e + 1, :])
                cb = _rows_bf16(cc_ref[h, e:e + 1, :])
                a2 = a2_ref[h]
                term = jnp.where(r2_ref[h] < nb, a2, jnp.zeros_like(a2)) * cb
                g = term if g is None else g + term
            rows = slice(e * N_KEYS, (e + 1) * N_KEYS)
            act_ref[rows, :] = (_gelu_tanh_x2_scaled(st_ref[rows, :]) * g).astype(F8)

    def accumulate(half, act_ref):
        acc_ref[...] += jnp.dot(vt_ref[:, half * E_TILE:(half + 1) * E_TILE], act_ref[...],
                                preferred_element_type=F32)

    @pl.when(m == 0)
    def _():
        acc_ref[...] = jnp.zeros_like(acc_ref)
        scores(0, st0_ref)
        gate(st0_ref, act0_ref, nsc_ref, ccc_ref)
        scores(1, st1_ref)

    @pl.when((m > 0) & (m < nblk))
    def _():
        accumulate(0, act0_ref)
        gate(st1_ref, act1_ref, nsp_ref, ccp_ref)
        scores(0, st0_ref)
        accumulate(1, act1_ref)
        gate(st0_ref, act0_ref, nsc_ref, ccc_ref)
        scores(1, st1_ref)

    @pl.when(m == nblk)
    def _():
        accumulate(0, act0_ref)
        gate(st1_ref, act1_ref, nsp_ref, ccp_ref)
        accumulate(1, act1_ref)
        y = x_ref[...] + ga_ref[0] * (acc_ref[...].T * (1.0 / PEER_SCALE))
        if final_norm:
            y = y * lax.rsqrt(jnp.mean(y * y, axis=-1, keepdims=True) + EPS) * gf_ref[...]
        o_ref[...] = y


def _peer_mix(ht, u, vt, r2, a2, ns, cc, x2, ga, l, g_final=None):
    d, n = ht.shape
    n_exp = u.shape[0]
    t = min(512, l)
    per = l // t
    nblk = n_exp // E_BLOCK
    n_tiles = n_exp // E_TILE
    sblk = pl.BlockSpec((PEER_HEADS, N_KEYS, t), lambda i, m: (0, 0, i))
    prev = pl.BlockSpec((PEER_HEADS, ROWS_PER_TILE, t), lambda i, m: (0, jnp.maximum(2 * m - 1, 0), i))
    cur = pl.BlockSpec((PEER_HEADS, ROWS_PER_TILE, t), lambda i, m: (0, jnp.minimum(2 * m, n_tiles - 1), i))
    return pl.pallas_call(
        functools.partial(_peer_mix_kernel, nblk=nblk, final_norm=g_final is not None),
        out_shape=jax.ShapeDtypeStruct((n, d), F32),
        grid=(n // t, nblk + 1),
        in_specs=[pl.BlockSpec((d, t), lambda i, m: (0, i)),
                  pl.BlockSpec((E_BLOCK, d), lambda i, m: (jnp.minimum(m, nblk - 1), 0)),
                  pl.BlockSpec((d, E_BLOCK), lambda i, m: (0, jnp.maximum(m - 1, 0))),
                  sblk, sblk, prev, prev, cur, cur,
                  pl.BlockSpec((t, d), lambda i, m: (i, 0)),
                  pl.BlockSpec((1, 1, d), lambda i, m: (i // per, 0, 0)),
                  pl.BlockSpec((1, d), lambda i, m: (0, 0))],
        out_specs=pl.BlockSpec((t, d), lambda i, m: (i, 0)),
        scratch_shapes=[pltpu.VMEM((E_TILE, t), BF16), pltpu.VMEM((E_TILE, t), BF16),
                        pltpu.VMEM((E_TILE, t), F8), pltpu.VMEM((E_TILE, t), F8),
                        pltpu.VMEM((d, t), F32)],
        compiler_params=_params("arbitrary", "arbitrary"),
        name="peer_mix",
    )(ht, u, vt, r2, a2, ns, cc, ns, cc, x2, ga, jnp.ones((1, d), F32) if g_final is None else g_final)


def _pad_lanes(a):
    return jnp.pad(a, ((0, 0), (0, LANES - a.shape[1])))


def _token_mix_inputs(p, proj_hy, l):
    fs, fd, nyq = _hy_filter(l, p['hf_w1'], p['hf_b1'], p['hf_w2'], p['hf_b2'], p['hf_w3'], p['hf_freq'])
    return _hy_conv(proj_hy, p['hy_conv_w'], p['hy_conv_b'], fs, fd, nyq, p['hy_bias'])


def _peer(x, p, sh, sc, ga, g_final=None):
    b, l, d = x.shape
    x2 = x.reshape(b * l, d)
    ht, r2, a2, ns, cc = _peer_select(x2, p['g_norm2'], sh, sc, p['wq'], p['k1'], p['k2'], l)
    out = _peer_mix(ht, p['u'], p['vt'], r2, a2, ns, cc, x2, ga, l, g_final)
    return out.reshape(b, l, d)


def _layer(xl, xc, mod_l, mod_c, p, ctx_out, g_final=None):
    b = xl.shape[0]
    d = xl.shape[2]
    sh1, sc1, ga1, sh2, sc2, ga2 = [m.reshape(b, 1, d) for m in jnp.split(mod_l, 6, axis=-1)]
    csh1, csc1, cga1, csh2, csc2, cga2 = [jnp.broadcast_to(m.reshape(1, 1, d), (b, 1, d))
                                          for m in jnp.split(mod_c, 6, axis=-1)]
    w_in = p['w_in']
    w_hy, w_z, w_xbc, w_fn = (w_in[:, OFF_HY:OFF_Z], w_in[:, OFF_Z:OFF_XBC], w_in[:, OFF_XBC:OFF_DT],
                              w_in[:, OFF_FN:D_IN_PROJ])
    w_dt = _pad_lanes(w_in[:, OFF_DT:OFF_FN])
    g1 = p['g_norm1']
    pl_hy, pl_z, pl_xbc, pl_dt, pl_fn = _in_proj(xl, g1, sh1, sc1, [w_hy, w_z, w_xbc, w_dt, w_fn])
    if ctx_out:
        pc_hy, pc_z, pc_xbc, pc_dt, pc_fn = _in_proj(xc, g1, csh1, csc1, [w_hy, w_z, w_xbc, w_dt, w_fn])
    else:
        pc_xbc, pc_dt = _in_proj(xc, g1, csh1, csc1, [w_xbc, w_dt])

    xbc_c = _dwconv(pc_xbc, p['ssd_conv_w'], p['ssd_conv_b'], act=True)
    xbc_l = _dwconv(pl_xbc, p['ssd_conv_w'], p['ssd_conv_b'], act=True)
    dtb, arow = p['dtb_row'], p['a_row']
    h0 = jnp.zeros((b, SSD_GROUPS, SSD_STATE, SSD_INNER // SSD_GROUPS), F32)
    yc_f, yc_b, hc_f, hc_b = _ssd_scan(xbc_c, pc_dt, dtb, arow, h0, h0)
    yl_f, yl_b, _, _ = _ssd_scan(xbc_l, pl_dt, dtb, arow, hc_f, hc_b)

    def token_mix(x, ga, proj_hy, y_f, y_b, xbc, z, proj_fn):
        y_hy = _token_mix_inputs(p, proj_hy, x.shape[1])
        y_fn = _fnet(proj_fn)
        return _out_proj(x, ga, y_hy, y_f, y_b, xbc, z, y_fn, p['dsk_row'], p['ng_row'], p['wo1'], p['wo2'], p['wo3'])

    xl = token_mix(xl, ga1, pl_hy, yl_f, yl_b, xbc_l, pl_z, pl_fn)
    xl = _peer(xl, p, sh2, sc2, ga2, g_final)
    if ctx_out:
        xc = token_mix(xc, cga1, pc_hy, yc_f, yc_b, xbc_c, pc_z, pc_fn)
        xc = _peer(xc, p, csh2, csc2, cga2)
    return xl, xc


def kernel(x, c, ctx, c_ctx, w_ada, b_ada, g_norm1, g_norm2, w_in, hy_conv_w, hy_conv_b, hf_w1, hf_b1, hf_w2, hf_b2, hf_w3, hf_freq, hy_bias, ssd_conv_w, ssd_conv_b, ssd_dt_bias, ssd_a_log, ssd_d, ssd_norm_g, w_out, peer_wq, peer_k1, peer_k2, peer_u, peer_v, g_final):
    depth = w_ada.shape[0]
    b, l, d = x.shape
    rows = -(-(b + 1) // SUBLANES) * SUBLANES
    cc = jnp.concatenate([c, c_ctx[None, :], jnp.zeros((rows - b - 1, d), F32)], axis=0)
    mods = _ada_mod(cc, w_ada, b_ada)
    xl, xc = x, ctx
    for i in range(depth):
        wo = w_out[i].astype(BF16)
        p = {
            'g_norm1': g_norm1[i].reshape(1, d), 'g_norm2': g_norm2[i].reshape(1, d),
            'w_in': w_in[i].astype(BF16),
            'hy_conv_w': hy_conv_w[i], 'hy_conv_b': hy_conv_b[i],
            'hf_w1': hf_w1[i], 'hf_b1': hf_b1[i], 'hf_w2': hf_w2[i], 'hf_b2': hf_b2[i], 'hf_w3': hf_w3[i],
            'hf_freq': hf_freq[i], 'hy_bias': hy_bias[i],
            'ssd_conv_w': ssd_conv_w[i], 'ssd_conv_b': ssd_conv_b[i],
            'dtb_row': _pad_lanes(ssd_dt_bias[i].reshape(1, 2 * SSD_HEADS)),
            'a_row': _pad_lanes(-jnp.exp(ssd_a_log[i].astype(F32)).reshape(1, 2 * SSD_HEADS)),
            'dsk_row': jnp.repeat(ssd_d[i].astype(F32), SSD_HEAD_DIM).reshape(1, SSD_INNER),
            'ng_row': ssd_norm_g[i].reshape(1, SSD_INNER),
            'wo1': wo[:HY_WIDTH], 'wo2': wo[HY_WIDTH:HY_WIDTH + SSD_INNER], 'wo3': wo[HY_WIDTH + SSD_INNER:],
            'wq': peer_wq[i].astype(BF16), 'k1': peer_k1[i].astype(BF16), 'k2': peer_k2[i].astype(BF16),
            'u': (peer_u[i] * PEER_SCALE).astype(F8), 'vt': peer_v[i].astype(F8).T,
        }
        last = i == depth - 1
        xl, xc = _layer(xl, xc, mods[i, :b], mods[i, b], p, not last, g_final.reshape(1, d) if last else None)
    return xl
```

```python
import functools
import math

import jax
import jax.numpy as jnp
import numpy as np
from jax import lax
from jax.experimental import pallas as pl
from jax.experimental.pallas import tpu as pltpu

F32 = jnp.float32
BF16 = jnp.bfloat16
F8 = jnp.float8_e4m3fn
HI = lax.Precision.HIGHEST

EPS = 1e-6
LANES = 128
SUBLANES = 8
VMEM_LIMIT = 56 * 1024 * 1024

HY_WIDTH = 256
HY_EMB = 33
HY_BANDS = (HY_EMB - 1) // 2
HY_DECAY_TARGET = 1e-2
HY_SHORT_DECAY_PCT = 0.3
HY_LONG_DECAY_PCT = 1.5
SSD_HEADS = 8
SSD_HEAD_DIM = 64
SSD_INNER = SSD_HEADS * SSD_HEAD_DIM
SSD_GROUPS = 2
SSD_STATE = 128
SSD_CHUNK = 128
SSD_XBC = SSD_INNER + 2 * SSD_GROUPS * SSD_STATE
FN_WIDTH = 256
FN_GROUPS = 4
FN_GROUP_DIM = FN_WIDTH // FN_GROUPS
PEER_HEADS = 8
PEER_TOPK = 16
N_KEYS = 128
PEER_HALF = 128
OFF_HY = 0
OFF_Z = OFF_HY + 3 * HY_WIDTH
OFF_XBC = OFF_Z + SSD_INNER
OFF_DT = OFF_XBC + SSD_XBC
OFF_FN = OFF_DT + 2 * SSD_HEADS
D_IN_PROJ = OFF_FN + FN_WIDTH


def _params(*sem):
    return pltpu.CompilerParams(dimension_semantics=sem, vmem_limit_bytes=VMEM_LIMIT)


def _silu(x):
    return x * jax.nn.sigmoid(x)


def _softplus(x):
    return jnp.maximum(x, 0.0) + jnp.log1p(jnp.exp(-jnp.abs(x)))


def _gelu_tanh(x):
    return 0.5 * x * (1.0 + jnp.tanh(math.sqrt(2.0 / math.pi) * (x + 0.044715 * (x * x * x))))


def _ada_kernel(c_ref, w_ref, b_ref, o_ref):
    s = _silu(c_ref[...])
    o_ref[0] = jnp.dot(s, w_ref[0], preferred_element_type=F32, precision=HI) + b_ref[0]


def _ada_mod(cc, w_ada, b_ada):
    depth, d, n = w_ada.shape
    r = cc.shape[0]
    tn = 1536
    return pl.pallas_call(
        _ada_kernel,
        out_shape=jax.ShapeDtypeStruct((depth, r, n), F32),
        grid=(depth, n // tn),
        in_specs=[pl.BlockSpec((r, d), lambda l, j: (0, 0)),
                  pl.BlockSpec((1, d, tn), lambda l, j: (l, 0, j)),
                  pl.BlockSpec((1, 1, tn), lambda l, j: (l, 0, j))],
        out_specs=pl.BlockSpec((1, r, tn), lambda l, j: (l, 0, j)),
        compiler_params=_params("arbitrary", "arbitrary"),
        name="ada_mod",
    )(cc, w_ada, b_ada.reshape(depth, 1, n))


def _normmod(x, g, sh, sc):
    y = x * lax.rsqrt(jnp.mean(x * x, axis=-1, keepdims=True) + EPS) * g
    return y * (1.0 + sc) + sh


def _inproj_kernel(x_ref, g_ref, sh_ref, sc_ref, *rest, n_w):
    w_refs, o_refs = rest[:n_w], rest[n_w:]
    hb = _normmod(x_ref[0], g_ref[...], sh_ref[0], sc_ref[0]).astype(BF16)
    for w_ref, o_ref in zip(w_refs, o_refs):
        o_ref[0] = jnp.dot(hb, w_ref[...], preferred_element_type=F32).astype(o_ref.dtype)


def _in_proj(x, g, sh, sc, ws, dtypes):
    b, l, d = x.shape
    tm = min(512, l)
    return pl.pallas_call(
        functools.partial(_inproj_kernel, n_w=len(ws)),
        out_shape=[jax.ShapeDtypeStruct((b, l, w.shape[1]), dt) for w, dt in zip(ws, dtypes)],
        grid=(b, l // tm),
        in_specs=[pl.BlockSpec((1, tm, d), lambda i, j: (i, j, 0)),
                  pl.BlockSpec((1, d), lambda i, j: (0, 0)),
                  pl.BlockSpec((1, 1, d), lambda i, j: (i, 0, 0)),
                  pl.BlockSpec((1, 1, d), lambda i, j: (i, 0, 0))]
        + [pl.BlockSpec(w.shape, lambda i, j: (0, 0)) for w in ws],
        out_specs=[pl.BlockSpec((1, tm, w.shape[1]), lambda i, j: (i, j, 0)) for w in ws],
        compiler_params=_params("arbitrary", "arbitrary"),
        name="in_proj",
    )(x, g, sh, sc, *ws)


def _conv3(u, w, bias):
    l = u.shape[0]
    row = lax.broadcasted_iota(jnp.int32, u.shape, 0)
    um = jnp.where(row == 0, 0.0, pltpu.roll(u, 1, 0))
    up = jnp.where(row == l - 1, 0.0, pltpu.roll(u, l - 1, 0))
    return um * w[0:1, :] + u * w[1:2, :] + up * w[2:3, :] + bias


def _dwconv_kernel(u_ref, w_ref, b_ref, o_ref, *, act):
    y = _conv3(u_ref[0], w_ref[...], b_ref[...])
    if act:
        y = _silu(y)
    o_ref[0] = y


def _dwconv(u, w, bias, act):
    b, l, c = u.shape
    cb = 256
    return pl.pallas_call(
        functools.partial(_dwconv_kernel, act=act),
        out_shape=jax.ShapeDtypeStruct((b, l, c), F32),
        grid=(b, c // cb),
        in_specs=[pl.BlockSpec((1, l, cb), lambda i, j: (i, 0, j)),
                  pl.BlockSpec((3, cb), lambda i, j: (0, j)),
                  pl.BlockSpec((1, cb), lambda i, j: (0, j))],
        out_specs=pl.BlockSpec((1, l, cb), lambda i, j: (i, 0, j)),
        compiler_params=_params("arbitrary", "arbitrary"),
        name="dwconv",
    )(u, w, bias.reshape(1, c))


def _split3(x):
    hi = x.astype(BF16)
    r1 = x - hi.astype(F32)
    mid = r1.astype(BF16)
    lo = (r1 - mid.astype(F32)).astype(BF16)
    return hi, mid, lo


def _ssd_chunk(xbc, dt_raw, dtb, arow, st_ref, direction):
    q = SSD_CHUNK
    x = xbc[:, :SSD_INNER]
    gn = SSD_GROUPS * SSD_STATE
    bm = xbc[:, SSD_INNER:SSD_INNER + gn].astype(BF16)
    cm = xbc[:, SSD_INNER + gn:].astype(BF16)
    dtp = _softplus(dt_raw + dtb)
    da = dtp * arow
    ii = lax.broadcasted_iota(jnp.int32, (q, q), 0)
    jj = lax.broadcasted_iota(jnp.int32, (q, q), 1)
    tri = (ii >= jj) if direction == 0 else (ii <= jj)
    tri_b = tri.astype(BF16)
    cs = jnp.dot(jnp.concatenate([tri_b] * 3, axis=1), jnp.concatenate(_split3(da), axis=0),
                 preferred_element_type=F32)
    cs_t = cs.T
    lane = lax.broadcasted_iota(jnp.int32, (LANES, SSD_INNER), 0)
    chan = lax.broadcasted_iota(jnp.int32, (LANES, SSD_INNER), 1)
    head_of = lax.shift_right_logical(chan, int(math.log2(SSD_HEAD_DIM)))
    expand = (lane == direction * SSD_HEADS + head_of).astype(BF16)
    expand3 = jnp.concatenate([expand] * 3, axis=0)
    both = jnp.concatenate([jnp.concatenate(_split3(cs), axis=1), jnp.concatenate(_split3(dtp), axis=1)], axis=0)
    wide = jnp.dot(both, expand3, preferred_element_type=F32)
    acum, dtf = wide[:q], wide[q:]
    tot = acum[q - 1:q, :] if direction == 0 else acum[0:1, :]
    xdt = x * dtf
    ea = jnp.exp(acum)
    xdec = (jnp.exp(tot - acum) * xdt).astype(BF16)
    cd = jnp.exp(tot)
    xdt_b = xdt.astype(BF16)
    hg = SSD_HEADS // SSD_GROUPS
    gw = hg * SSD_HEAD_DIM
    outs = []
    for g in range(SSD_GROUPS):
        bg = bm[:, g * SSD_STATE:(g + 1) * SSD_STATE]
        cg = cm[:, g * SSD_STATE:(g + 1) * SSD_STATE]
        cb = lax.dot_general(cg, bg, (((1,), (1,)), ((), ())), preferred_element_type=F32)
        st = st_ref[g]
        y_g = jnp.dot(cg, st.astype(BF16), preferred_element_type=F32) * ea[:, g * gw:(g + 1) * gw]
        xg = xdt_b[:, g * gw:(g + 1) * gw]
        head_g = lax.shift_right_logical(lax.broadcasted_iota(jnp.int32, xg.shape, 1), int(math.log2(SSD_HEAD_DIM)))
        for k in range(hg):
            ln = direction * SSD_HEADS + g * hg + k
            seg = cs[:, ln:ln + 1] - cs_t[ln:ln + 1, :]
            lmat = jnp.where(tri, jnp.exp(jnp.where(tri, seg, 0.0)), 0.0)
            m = (cb * lmat).astype(BF16)
            y_g = y_g + jnp.dot(m, jnp.where(head_g == k, xg, jnp.zeros_like(xg)), preferred_element_type=F32)
        outs.append(y_g)
        sg = lax.dot_general(bg, xdec[:, g * gw:(g + 1) * gw], (((0,), (0,)), ((), ())),
                             preferred_element_type=F32)
        st_ref[g] = st * cd[:, g * gw:(g + 1) * gw] + sg
    return jnp.concatenate(outs, axis=1)


def _ssd_kernel(xf_ref, dtf_ref, xb_ref, dtb_ref, bias_ref, arow_ref, h0f_ref, h0b_ref,
                yf_ref, yb_ref, hlf_ref, hlb_ref, stf_ref, stb_ref, *, nc):
    c = pl.program_id(1)

    @pl.when(c == 0)
    def _():
        stf_ref[...] = h0f_ref[0]
        stb_ref[...] = h0b_ref[0]

    yf_ref[0] = _ssd_chunk(xf_ref[0], dtf_ref[0], bias_ref[...], arow_ref[...], stf_ref, 0)
    yb_ref[0] = _ssd_chunk(xb_ref[0], dtb_ref[0], bias_ref[...], arow_ref[...], stb_ref, 1)

    @pl.when(c == nc - 1)
    def _():
        hlf_ref[0] = stf_ref[...]
        hlb_ref[0] = stb_ref[...]


def _ssd_scan(xbc, dt, dtb_row, a_row, h0f, h0b):
    b, l, _ = xbc.shape
    nc = l // SSD_CHUNK
    fwd = lambda i, c: (i, c, 0)
    bwd = lambda i, c: (i, nc - 1 - c, 0)
    st_shape = (SSD_GROUPS, SSD_STATE, SSD_INNER // SSD_GROUPS)
    st_spec = pl.BlockSpec((1,) + st_shape, lambda i, c: (i, 0, 0, 0))
    row = pl.BlockSpec((1, LANES), lambda i, c: (0, 0))
    return pl.pallas_call(
        functools.partial(_ssd_kernel, nc=nc),
        out_shape=[jax.ShapeDtypeStruct((b, l, SSD_INNER), F32)] * 2
        + [jax.ShapeDtypeStruct((b,) + st_shape, F32)] * 2,
        grid=(b, nc),
        in_specs=[pl.BlockSpec((1, SSD_CHUNK, SSD_XBC), fwd), pl.BlockSpec((1, SSD_CHUNK, LANES), fwd),
                  pl.BlockSpec((1, SSD_CHUNK, SSD_XBC), bwd), pl.BlockSpec((1, SSD_CHUNK, LANES), bwd),
                  row, row, st_spec, st_spec],
        out_specs=[pl.BlockSpec((1, SSD_CHUNK, SSD_INNER), fwd), pl.BlockSpec((1, SSD_CHUNK, SSD_INNER), bwd),
                   st_spec, st_spec],
        scratch_shapes=[pltpu.VMEM(st_shape, F32), pltpu.VMEM(st_shape, F32)],
        compiler_params=_params("arbitrary", "arbitrary"),
        name="ssd_scan",
    )(xbc, dt, xbc, dt, dtb_row, a_row, h0f, h0b)


def _hy_filter_kernel(z_ref, w1_ref, b1_ref, w2_ref, b2_ref, w3_ref, fr_ref, win_ref, alt_ref,
                      fs_ref, fd_ref, nyq_ref):
    fr = fr_ref[...]
    h = jnp.sin(fr * (jnp.dot(z_ref[...], w1_ref[...], preferred_element_type=F32, precision=HI) + b1_ref[...]))
    h = jnp.sin(fr * (jnp.dot(h, w2_ref[...], preferred_element_type=F32, precision=HI) + b2_ref[...]))
    h = jnp.dot(h, w3_ref[...], preferred_element_type=F32, precision=HI)
    win = win_ref[...]
    row = lax.broadcasted_iota(jnp.int32, win.shape, 0)
    hf = h[:, :HY_WIDTH] * win
    hb = jnp.where(row == 0, 0.0, h[:, HY_WIDTH:] * win)
    nrm = lax.rsqrt(jnp.sum(hf * hf, axis=0, keepdims=True) + jnp.sum(hb * hb, axis=0, keepdims=True) + EPS)
    fs = (hf + hb) * nrm
    fs_ref[...] = fs
    fd_ref[...] = (hb - hf) * nrm
    nyq_ref[...] = jnp.sum(fs * alt_ref[...], axis=0, keepdims=True)


def _hy_filter(l, w1, b1, w2, b2, w3, freq):
    t = np.linspace(0.0, 1.0, l, dtype=np.float32)[:, None]
    w = (np.float32(2.0 * math.pi) * np.arange(l, dtype=np.float32)[:, None] / np.float32(l)).astype(np.float32)
    f = np.linspace(1e-4, HY_BANDS - 1, HY_BANDS, dtype=np.float32)[None, :]
    zf = jnp.asarray(f) * jnp.asarray(w)
    z = jnp.concatenate([jnp.asarray(t), jnp.cos(zf), -jnp.sin(zf)], axis=-1)
    z = jnp.pad(z, ((0, 0), (0, LANES - HY_EMB)))
    hid = w1.shape[1]
    w1p = jnp.pad(w1, ((0, LANES - HY_EMB), (0, LANES - hid)))
    w2p = jnp.pad(w2, ((0, LANES - hid), (0, LANES - hid)))
    w3p = jnp.pad(w3, ((0, LANES - hid), (0, 0)))
    pad_row = lambda v: jnp.pad(v.reshape(1, hid), ((0, 0), (0, LANES - hid)))
    max_decay = math.log(HY_DECAY_TARGET) / HY_SHORT_DECAY_PCT
    min_decay = math.log(HY_DECAY_TARGET) / HY_LONG_DECAY_PCT
    deltas = jnp.abs(jnp.linspace(min_decay, max_decay, HY_WIDTH, dtype=F32))
    win = jnp.exp(-jnp.asarray(t) * deltas)
    alt = jnp.asarray(np.where(np.arange(l) % 2 == 0, 1.0, -1.0).astype(np.float32)[:, None] * np.ones((1, HY_WIDTH), np.float32))
    return pl.pallas_call(
        _hy_filter_kernel,
        out_shape=[jax.ShapeDtypeStruct((l, HY_WIDTH), F32), jax.ShapeDtypeStruct((l, HY_WIDTH), F32),
                   jax.ShapeDtypeStruct((1, HY_WIDTH), F32)],
        compiler_params=pltpu.CompilerParams(vmem_limit_bytes=VMEM_LIMIT),
        name="hy_filter",
    )(z, w1p, pad_row(b1), w2p, pad_row(b2), w3p, pad_row(freq), win, alt)


def _mm_kernel(a_ref, b_ref, o_ref):
    o_ref[...] = jnp.dot(a_ref[...], b_ref[...], preferred_element_type=F32)


def _mm(a, b, tm):
    m, k = a.shape
    n = b.shape[1]
    return pl.pallas_call(
        _mm_kernel,
        out_shape=jax.ShapeDtypeStruct((m, n), F32),
        grid=(m // tm,),
        in_specs=[pl.BlockSpec((tm, k), lambda i: (i, 0)), pl.BlockSpec((k, n), lambda i: (0, 0))],
        out_specs=pl.BlockSpec((tm, n), lambda i: (i, 0)),
        compiler_params=_params("arbitrary"),
        name="mm",
    )(a, b)


def _dft_tables(l):
    n = 2 * l
    k = np.arange(l, dtype=np.int64)[:, None]
    s = np.arange(l, dtype=np.int64)[None, :]
    ang = (2.0 * np.pi / n) * ((k * s) % n).astype(np.float64)
    cos, sin = np.cos(ang), np.sin(ang)
    alt = np.where(np.arange(l) % 2 == 0, 1.0, -1.0)
    sin[0, :] = alt
    fwd = np.concatenate([cos, sin], axis=0)
    wgt = np.full((l, 1), 2.0)
    wgt[0, 0] = 1.0
    sin_i = -sin * wgt
    sin_i[0, :] = alt
    inv = np.concatenate([(cos * wgt).T, sin_i.T], axis=1) / n
    return fwd, inv


def _batch_rows(b):
    return 2 if b % 2 == 0 else 1


def _hy_conv_kernel(u_ref, cw_ref, cb_ref, ff_ref, fi_ref, ka_ref, kb_ref, kc_ref, bias_ref, o_ref,
                    ub_ref, uf_ref, x1_ref, acc_ref, *, nf):
    f = pl.program_id(1)
    w = HY_WIDTH
    rows = u_ref.shape[0]

    @pl.when(f == 0)
    def _():
        for r in range(rows):
            part = lambda k: _conv3(u_ref[r, :, k * w:(k + 1) * w].astype(F32), cw_ref[:, k * w:(k + 1) * w],
                                    cb_ref[:, k * w:(k + 1) * w])
            u = part(2) * part(0)
            cols = slice(r * w, (r + 1) * w)
            uf_ref[:, cols] = u
            ub_ref[:, cols] = u.astype(BF16)
            x1_ref[:, cols] = part(1)
        acc_ref[...] = jnp.zeros_like(acc_ref)

    uf = jnp.dot(ff_ref[0], ub_ref[...], preferred_element_type=F32)
    fb = uf.shape[0] // 2
    ur, us = uf[:fb], uf[fb:]
    ka, kb, kc = [jnp.concatenate([k_ref[...]] * rows, axis=1) for k_ref in (ka_ref, kb_ref, kc_ref)]
    p = jnp.concatenate([ur * ka + us * kb, ur * kb - us * kc], axis=0).astype(BF16)
    acc_ref[...] += jnp.dot(fi_ref[0], p, preferred_element_type=F32)

    @pl.when(f == nf - 1)
    def _():
        for r in range(rows):
            cols = slice(r * w, (r + 1) * w)
            o_ref[r] = x1_ref[:, cols] * (acc_ref[:, cols] + uf_ref[:, cols] * bias_ref[...])


def _hy_conv(proj, conv_w, conv_b, fs, fd, nyq, bias):
    b, l, _ = proj.shape
    rows = _batch_rows(b)
    fwd, inv = _dft_tables(l)
    fb = min(256, l)
    nf = l // fb
    ff = jnp.asarray(np.stack([np.concatenate([fwd[i * fb:(i + 1) * fb], fwd[l + i * fb:l + (i + 1) * fb]], axis=0)
                               for i in range(nf)]), dtype=BF16)
    fi = jnp.asarray(np.stack([np.concatenate([inv[:, i * fb:(i + 1) * fb], inv[:, l + i * fb:l + (i + 1) * fb]], axis=1)
                               for i in range(nf)]), dtype=BF16)
    fwd_b = jnp.asarray(fwd, dtype=BF16)
    tm = min(512, l)
    k_r = _mm(fwd_b[:l], fs.astype(BF16), tm)
    k_i = _mm(fwd_b[l:], fd.astype(BF16), tm)
    first = (jnp.arange(l) == 0)[:, None]
    ka = k_r
    kb = jnp.where(first, 0.0, k_i)
    kc = jnp.where(first, -nyq, k_r)
    return pl.pallas_call(
        functools.partial(_hy_conv_kernel, nf=nf),
        out_shape=jax.ShapeDtypeStruct((b, l, HY_WIDTH), F32),
        grid=(b // rows, nf),
        in_specs=[pl.BlockSpec((rows, l, 3 * HY_WIDTH), lambda i, f: (i, 0, 0)),
                  pl.BlockSpec((3, 3 * HY_WIDTH), lambda i, f: (0, 0)),
                  pl.BlockSpec((1, 3 * HY_WIDTH), lambda i, f: (0, 0)),
                  pl.BlockSpec((1, 2 * fb, l), lambda i, f: (f, 0, 0)),
                  pl.BlockSpec((1, l, 2 * fb), lambda i, f: (f, 0, 0)),
                  pl.BlockSpec((fb, HY_WIDTH), lambda i, f: (f, 0)),
                  pl.BlockSpec((fb, HY_WIDTH), lambda i, f: (f, 0)),
                  pl.BlockSpec((fb, HY_WIDTH), lambda i, f: (f, 0)),
                  pl.BlockSpec((1, HY_WIDTH), lambda i, f: (0, 0))],
        out_specs=pl.BlockSpec((rows, l, HY_WIDTH), lambda i, f: (i, 0, 0)),
        scratch_shapes=[pltpu.VMEM((l, rows * HY_WIDTH), BF16), pltpu.VMEM((l, rows * HY_WIDTH), F32),
                        pltpu.VMEM((l, rows * HY_WIDTH), F32), pltpu.VMEM((l, rows * HY_WIDTH), F32)],
        compiler_params=_params("arbitrary", "arbitrary"),
        name="hy_conv",
    )(proj, conv_w, conv_b.reshape(1, 3 * HY_WIDTH), ff, fi, ka, kb, kc, bias.reshape(1, HY_WIDTH))


def _fnet_kernel(x_ref, cc_ref, sc_ref, m_ref, o_ref, xx_ref):
    j = pl.program_id(1)
    rows, l, c = x_ref.shape

    @pl.when(j == 0)
    def _():
        for r in range(rows):
            xb = x_ref[r].astype(BF16)
            cols = slice(r * c, (r + 1) * c)
            xx_ref[:l, cols] = jnp.dot(xb, cc_ref[...], preferred_element_type=F32).astype(BF16)
            xx_ref[l:, cols] = jnp.dot(xb, sc_ref[...], preferred_element_type=F32).astype(BF16)

    y = jnp.dot(m_ref[...], xx_ref[...], preferred_element_type=F32)
    for r in range(rows):
        o_ref[r] = y[:, r * c:(r + 1) * c]


def _fnet(x):
    b, l, c = x.shape
    rows = _batch_rows(b)
    gd = FN_GROUP_DIM
    kc = np.arange(gd)[:, None] * np.arange(gd)[None, :]
    ang_c = 2.0 * np.pi * (kc % gd) / gd
    eye = np.eye(FN_GROUPS)
    cblk = jnp.asarray(np.kron(eye, np.cos(ang_c)), dtype=BF16)
    sblk = jnp.asarray(np.kron(eye, np.sin(ang_c)), dtype=BF16)
    kl = (np.arange(l, dtype=np.int64)[:, None] * np.arange(l, dtype=np.int64)[None, :]) % l
    ang_l = 2.0 * np.pi * kl / l
    scale = 1.0 / math.sqrt(l * gd)
    mat = jnp.asarray(np.concatenate([np.cos(ang_l), -np.sin(ang_l)], axis=1) * scale, dtype=BF16)
    tr = min(512, l)
    return pl.pallas_call(
        _fnet_kernel,
        out_shape=jax.ShapeDtypeStruct((b, l, c), F32),
        grid=(b // rows, l // tr),
        in_specs=[pl.BlockSpec((rows, l, c), lambda i, r: (i, 0, 0)),
                  pl.BlockSpec((c, c), lambda i, r: (0, 0)),
                  pl.BlockSpec((c, c), lambda i, r: (0, 0)),
                  pl.BlockSpec((tr, 2 * l), lambda i, r: (r, 0))],
        out_specs=pl.BlockSpec((rows, tr, c), lambda i, r: (i, r, 0)),
        scratch_shapes=[pltpu.VMEM((2 * l, rows * c), BF16)],
        compiler_params=_params("arbitrary", "arbitrary"),
        name="fnet",
    )(x, cblk, sblk, mat)


def _outproj_kernel(x_ref, ga_ref, yhy_ref, yf_ref, yb_ref, xs_ref, z_ref, yfn_ref, dsk_ref, ng_ref,
                    w1_ref, w2_ref, w3_ref, o_ref):
    y = yf_ref[0] + yb_ref[0] + dsk_ref[...] * xs_ref[0]
    y = y * _silu(z_ref[0])
    gw = SSD_INNER // SSD_GROUPS
    parts = []
    for g in range(SSD_GROUPS):
        yg = y[:, g * gw:(g + 1) * gw]
        parts.append(yg * lax.rsqrt(jnp.mean(yg * yg, axis=-1, keepdims=True) + EPS))
    yn = jnp.concatenate(parts, axis=1) * ng_ref[...]
    out = jnp.dot(yhy_ref[0].astype(BF16), w1_ref[...], preferred_element_type=F32)
    out += jnp.dot(yn.astype(BF16), w2_ref[...], preferred_element_type=F32)
    out += jnp.dot(yfn_ref[0].astype(BF16), w3_ref[...], preferred_element_type=F32)
    o_ref[0] = x_ref[0] + ga_ref[0] * out


def _out_proj(x, ga, y_hy, y_f, y_b, xbc, z, y_fn, dsk_row, ng_row, w1, w2, w3):
    b, l, d = x.shape
    tm = min(512, l)
    tok = lambda n: pl.BlockSpec((1, tm, n), lambda i, j: (i, j, 0))
    const = lambda a: pl.BlockSpec(a.shape, lambda i, j: (0,) * a.ndim)
    return pl.pallas_call(
        _outproj_kernel,
        out_shape=jax.ShapeDtypeStruct((b, l, d), F32),
        grid=(b, l // tm),
        in_specs=[tok(d), pl.BlockSpec((1, 1, d), lambda i, j: (i, 0, 0)), tok(HY_WIDTH), tok(SSD_INNER),
                  tok(SSD_INNER), tok(SSD_INNER), tok(SSD_INNER), tok(FN_WIDTH), const(dsk_row), const(ng_row),
                  const(w1), const(w2), const(w3)],
        out_specs=tok(d),
        compiler_params=_params("arbitrary", "arbitrary"),
        name="out_proj",
    )(x, ga, y_hy, y_f, y_b, xbc, z, y_fn, dsk_row, ng_row, w1, w2, w3)


def _peer_scores(x_ref, g_ref, sh_ref, sc_ref, wq_ref, k1_ref, k2_ref, ht_ref, s1_ref, s2_ref):
    hm = _normmod(x_ref[...], g_ref[...], sh_ref[0], sc_ref[0])
    ht_ref[...] = hm.T.astype(F8)
    qb = jnp.dot(hm.astype(BF16), wq_ref[...], preferred_element_type=F32).astype(BF16)
    nt = (((1,), (1,)), ((), ()))
    for h in range(PEER_HEADS):
        q1 = qb[:, (2 * h) * PEER_HALF:(2 * h + 1) * PEER_HALF]
        q2 = qb[:, (2 * h + 1) * PEER_HALF:(2 * h + 2) * PEER_HALF]
        s1_ref[h] = lax.dot_general(k1_ref[...], q1, nt, preferred_element_type=F32)
        s2_ref[h] = lax.dot_general(k2_ref[...], q2, nt, preferred_element_type=F32)


def _sorting_network(n):
    pairs = []

    def merge(lo, m, r):
        step = 2 * r
        if step < m:
            merge(lo, m, step)
            merge(lo + r, m, step)
            for i in range(lo + r, lo + m - r, step):
                pairs.append((i, i + r))
        else:
            pairs.append((lo, lo + r))

    def sort(lo, m):
        if m > 1:
            half = m // 2
            sort(lo, half)
            sort(lo + half, half)
            merge(lo, m, 1)

    sort(0, n)
    return tuple(pairs)


NEG = -3.0e38
N_TOP = PEER_TOPK + 1


def _top_sorted(rows, n_out):
    rows = list(rows)
    for i, j in _sorting_network(len(rows)):
        rows[i], rows[j] = jnp.maximum(rows[i], rows[j]), jnp.minimum(rows[i], rows[j])
    sub = lax.broadcasted_iota(jnp.int32, rows[0].shape, 0)
    out = []
    for r in range(n_out):
        m = jnp.max(rows[0], axis=0, keepdims=True)
        out.append(m)
        first = jnp.min(jnp.where(rows[0] == m, sub, SUBLANES), axis=0, keepdims=True)
        hit = sub == first
        keep = min(len(rows), n_out - r - 1)
        rows = [jnp.where(hit, rows[k + 1] if k + 1 < len(rows) else NEG, rows[k]) for k in range(keep)]
    return out


def _count_above(vals, x, strict):
    r = jnp.zeros_like(x)
    for j, v in enumerate(vals):
        r = jnp.where((v > x) if strict else (v >= x), float(j + 1), r)
    return r


def _dup_bf16(x):
    hi = pltpu.bitcast(x.astype(BF16).astype(F32), jnp.uint32)
    return hi | (hi >> 16)


def _peer_stats_kernel(s1_ref, s2_ref, r2_ref, a2_ref, ns_ref, cc_ref):
    nrow = N_KEYS // SUBLANES
    t = s1_ref.shape[2]
    sub = lax.broadcasted_iota(jnp.int32, (SUBLANES, t), 0)
    for h in range(PEER_HEADS):
        s1_rows = [s1_ref[h, SUBLANES * k:SUBLANES * (k + 1), :] for k in range(nrow)]
        s2_rows = [s2_ref[h, SUBLANES * k:SUBLANES * (k + 1), :] for k in range(nrow)]
        v1 = _top_sorted(s1_rows, N_TOP)
        v2 = _top_sorted(s2_rows, N_TOP)
        cands = [v1[i] + v2[j] for i in range(N_TOP) for j in range(N_TOP) if (i + 1) * (j + 1) <= N_TOP]
        packed = []
        for k in range(0, len(cands), SUBLANES):
            blk = jnp.full((SUBLANES, t), NEG, F32)
            for s, cv in enumerate(cands[k:k + SUBLANES]):
                blk = jnp.where(sub == s, cv, blk)
            packed.append(blk)
        ids = [sub + SUBLANES * k for k in range(len(packed))]
        big = SUBLANES * len(packed)
        top = []
        for r in range(N_TOP):
            m = packed[0]
            for blk in packed[1:]:
                m = jnp.maximum(m, blk)
            m = jnp.max(m, axis=0, keepdims=True)
            top.append(m)
            sel = jnp.where(packed[0] == m, ids[0], big)
            for blk, idk in zip(packed[1:], ids[1:]):
                sel = jnp.minimum(sel, jnp.where(blk == m, idk, big))
            sel = jnp.min(sel, axis=0, keepdims=True)
            packed = [jnp.where(idk == sel, NEG, blk) for blk, idk in zip(packed, ids)]
        mx = top[0]
        z = jnp.zeros((1, t), F32)
        for r in range(PEER_TOPK):
            z = z + jnp.exp(top[r] - mx)
        tau = 0.5 * (top[PEER_TOPK - 1] + top[PEER_TOPK])
        cscale = 0.5 / z
        for k in range(0, nrow, 2):
            pair = slice(SUBLANES * k, SUBLANES * (k + 2))
            r2 = [_count_above(v2[:PEER_TOPK], s2_rows[k + q], True) for q in range(2)]
            r2_ref[h, pair, :] = jnp.concatenate(r2, axis=0).astype(BF16)
            a2 = [jnp.exp(s2_rows[k + q] - v2[0]) for q in range(2)]
            a2_ref[h, pair, :] = jnp.concatenate(a2, axis=0).astype(BF16)
        for k in range(nrow):
            blk = slice(SUBLANES * k, SUBLANES * (k + 1))
            ns_ref[h, blk, :] = _dup_bf16(_count_above(v2[:PEER_TOPK], tau - s1_rows[k], False))
            cc_ref[h, blk, :] = _dup_bf16(jnp.exp(s1_rows[k] - v1[0]) * cscale)


def _peer_select_kernel(x_ref, g_ref, sh_ref, sc_ref, wq_ref, k1_ref, k2_ref,
                        ht_ref, r2_ref, a2_ref, ns_ref, cc_ref, s1_ref, s2_ref):
    _peer_scores(x_ref, g_ref, sh_ref, sc_ref, wq_ref, k1_ref, k2_ref, ht_ref, s1_ref, s2_ref)
    _peer_stats_kernel(s1_ref, s2_ref, r2_ref, a2_ref, ns_ref, cc_ref)


def _peer_select(x2, g, sh, sc, wq, k1, k2, l):
    n, d = x2.shape
    tm = min(256, l)
    per = l // tm
    shp = (PEER_HEADS, N_KEYS, n)
    blk = pl.BlockSpec((PEER_HEADS, N_KEYS, tm), lambda i: (0, 0, i))
    return pl.pallas_call(
        _peer_select_kernel,
        out_shape=[jax.ShapeDtypeStruct((d, n), F8),
                   jax.ShapeDtypeStruct(shp, BF16), jax.ShapeDtypeStruct(shp, BF16),
                   jax.ShapeDtypeStruct(shp, jnp.uint32), jax.ShapeDtypeStruct(shp, jnp.uint32)],
        grid=(n // tm,),
        in_specs=[pl.BlockSpec((tm, d), lambda i: (i, 0)),
                  pl.BlockSpec((1, d), lambda i: (0, 0)),
                  pl.BlockSpec((1, 1, d), lambda i: (i // per, 0, 0)),
                  pl.BlockSpec((1, 1, d), lambda i: (i // per, 0, 0)),
                  pl.BlockSpec(wq.shape, lambda i: (0, 0)),
                  pl.BlockSpec(k1.shape, lambda i: (0, 0)),
                  pl.BlockSpec(k2.shape, lambda i: (0, 0))],
        out_specs=[pl.BlockSpec((d, tm), lambda i: (0, i)), blk, blk, blk, blk],
        scratch_shapes=[pltpu.VMEM((PEER_HEADS, N_KEYS, tm), F32), pltpu.VMEM((PEER_HEADS, N_KEYS, tm), F32)],
        compiler_params=_params("arbitrary"),
        name="peer_select",
    )(x2, g, sh, sc, wq, k1, k2)


E_TILE = 1024
E_BLOCK = 2 * E_TILE
PEER_SCALE = 32.0


def _gelu_tanh_x2_scaled(s):
    c0 = math.sqrt(2.0 / math.pi) / PEER_SCALE
    c1 = math.sqrt(2.0 / math.pi) * 0.044715 / PEER_SCALE ** 3
    return s + s * jnp.tanh(s * (c0 + c1 * (s * s)))


ROWS_PER_TILE = E_TILE // N_KEYS


def _rows_bf16(word_row):
    words = word_row | jnp.zeros((SUBLANES, word_row.shape[1]), jnp.uint32)
    tile = pltpu.bitcast(words, BF16)
    return jnp.concatenate([tile] * (N_KEYS // tile.shape[0]), axis=0)


def _peer_mix_kernel(ht_ref, u_ref, vt_ref, r2_ref, a2_ref, nsp_ref, ccp_ref, nsc_ref, ccc_ref, x_ref, ga_ref,
                     gf_ref, o_ref, st0_ref, st1_ref, act0_ref, act1_ref, acc_ref, *, nblk, final_norm):
    m = pl.program_id(1)

    def scores(half, st_ref):
        st_ref[...] = jnp.dot(u_ref[half * E_TILE:(half + 1) * E_TILE, :], ht_ref[...],
                              preferred_element_type=F32).astype(BF16)

    def gate(st_ref, act_ref, ns_ref, cc_ref):
        for e in range(ROWS_PER_TILE):
            g = None
            for h in range(PEER_HEADS):
                nb = _rows_bf16(ns_ref[h, e:e + 1, :])
                cb = _rows_bf16(cc_ref[h, e:e + 1, :])
                a2 = a2_ref[h]
                term = jnp.where(r2_ref[h] < nb, a2, jnp.zeros_like(a2)) * cb
                g = term if g is None else g + term
            rows = slice(e * N_KEYS, (e + 1) * N_KEYS)
            act_ref[rows, :] = (_gelu_tanh_x2_scaled(st_ref[rows, :]) * g).astype(F8)

    def accumulate(half, act_ref):
        acc_ref[...] += jnp.dot(vt_ref[:, half * E_TILE:(half + 1) * E_TILE], act_ref[...],
                                preferred_element_type=F32)

    @pl.when(m == 0)
    def _():
        acc_ref[...] = jnp.zeros_like(acc_ref)
        scores(0, st0_ref)
        gate(st0_ref, act0_ref, nsc_ref, ccc_ref)
        scores(1, st1_ref)

    @pl.when((m > 0) & (m < nblk))
    def _():
        accumulate(0, act0_ref)
        gate(st1_ref, act1_ref, nsp_ref, ccp_ref)
        scores(0, st0_ref)
        accumulate(1, act1_ref)
        gate(st0_ref, act0_ref, nsc_ref, ccc_ref)
        scores(1, st1_ref)

    @pl.when(m == nblk)
    def _():
        accumulate(0, act0_ref)
        gate(st1_ref, act1_ref, nsp_ref, ccp_ref)
        accumulate(1, act1_ref)
        y = x_ref[...] + ga_ref[0] * (acc_ref[...].T * (1.0 / PEER_SCALE))
        if final_norm:
            y = y * lax.rsqrt(jnp.mean(y * y, axis=-1, keepdims=True) + EPS) * gf_ref[...]
        o_ref[...] = y


def _peer_mix(ht, u, vt, r2, a2, ns, cc, x2, ga, l, g_final=None):
    d, n = ht.shape
    n_exp = u.shape[0]
    t = min(512, l)
    per = l // t
    nblk = n_exp // E_BLOCK
    n_tiles = n_exp // E_TILE
    sblk = pl.BlockSpec((PEER_HEADS, N_KEYS, t), lambda i, m: (0, 0, i))
    prev = pl.BlockSpec((PEER_HEADS, ROWS_PER_TILE, t), lambda i, m: (0, jnp.maximum(2 * m - 1, 0), i))
    cur = pl.BlockSpec((PEER_HEADS, ROWS_PER_TILE, t), lambda i, m: (0, jnp.minimum(2 * m, n_tiles - 1), i))
    return pl.pallas_call(
        functools.partial(_peer_mix_kernel, nblk=nblk, final_norm=g_final is not None),
        out_shape=jax.ShapeDtypeStruct((n, d), F32),
        grid=(n // t, nblk + 1),
        in_specs=[pl.BlockSpec((d, t), lambda i, m: (0, i)),
                  pl.BlockSpec((E_BLOCK, d), lambda i, m: (jnp.minimum(m, nblk - 1), 0)),
                  pl.BlockSpec((d, E_BLOCK), lambda i, m: (0, jnp.maximum(m - 1, 0))),
                  sblk, sblk, prev, prev, cur, cur,
                  pl.BlockSpec((t, d), lambda i, m: (i, 0)),
                  pl.BlockSpec((1, 1, d), lambda i, m: (i // per, 0, 0)),
                  pl.BlockSpec((1, d), lambda i, m: (0, 0))],
        out_specs=pl.BlockSpec((t, d), lambda i, m: (i, 0)),
        scratch_shapes=[pltpu.VMEM((E_TILE, t), BF16), pltpu.VMEM((E_TILE, t), BF16),
                        pltpu.VMEM((E_TILE, t), F8), pltpu.VMEM((E_TILE, t), F8),
                        pltpu.VMEM((d, t), F32)],
        compiler_params=_params("arbitrary", "arbitrary"),
        name="peer_mix",
    )(ht, u, vt, r2, a2, ns, cc, ns, cc, x2, ga, jnp.ones((1, d), F32) if g_final is None else g_final)


def _pad_lanes(a):
    return jnp.pad(a, ((0, 0), (0, LANES - a.shape[1])))


def _token_mix_inputs(p, proj_hy, l):
    fs, fd, nyq = _hy_filter(l, p['hf_w1'], p['hf_b1'], p['hf_w2'], p['hf_b2'], p['hf_w3'], p['hf_freq'])
    return _hy_conv(proj_hy, p['hy_conv_w'], p['hy_conv_b'], fs, fd, nyq, p['hy_bias'])


def _peer(x, p, sh, sc, ga, g_final=None):
    b, l, d = x.shape
    x2 = x.reshape(b * l, d)
    ht, r2, a2, ns, cc = _peer_select(x2, p['g_norm2'], sh, sc, p['wq'], p['k1'], p['k2'], l)
    out = _peer_mix(ht, p['u'], p['vt'], r2, a2, ns, cc, x2, ga, l, g_final)
    return out.reshape(b, l, d)


def _layer(xl, xc, mod_l, mod_c, p, ctx_out, g_final=None):
    b = xl.shape[0]
    d = xl.shape[2]
    sh1, sc1, ga1, sh2, sc2, ga2 = [m.reshape(b, 1, d) for m in jnp.split(mod_l, 6, axis=-1)]
    csh1, csc1, cga1, csh2, csc2, cga2 = [jnp.broadcast_to(m.reshape(1, 1, d), (b, 1, d))
                                          for m in jnp.split(mod_c, 6, axis=-1)]
    w_in = p['w_in']
    w_hy, w_z, w_xbc, w_fn = (w_in[:, OFF_HY:OFF_Z], w_in[:, OFF_Z:OFF_XBC], w_in[:, OFF_XBC:OFF_DT],
                              w_in[:, OFF_FN:D_IN_PROJ])
    w_dt = _pad_lanes(w_in[:, OFF_DT:OFF_FN])
    g1 = p['g_norm1']
    all_w, all_dt = [w_hy, w_z, w_xbc, w_dt, w_fn], [BF16, F32, F32, F32, BF16]
    pl_hy, pl_z, pl_xbc, pl_dt, pl_fn = _in_proj(xl, g1, sh1, sc1, all_w, all_dt)
    if ctx_out:
        pc_hy, pc_z, pc_xbc, pc_dt, pc_fn = _in_proj(xc, g1, csh1, csc1, all_w, all_dt)
    else:
        pc_xbc, pc_dt = _in_proj(xc, g1, csh1, csc1, [w_xbc, w_dt], [F32, F32])

    xbc_c = _dwconv(pc_xbc, p['ssd_conv_w'], p['ssd_conv_b'], act=True)
    xbc_l = _dwconv(pl_xbc, p['ssd_conv_w'], p['ssd_conv_b'], act=True)
    dtb, arow = p['dtb_row'], p['a_row']
    h0 = jnp.zeros((b, SSD_GROUPS, SSD_STATE, SSD_INNER // SSD_GROUPS), F32)
    yc_f, yc_b, hc_f, hc_b = _ssd_scan(xbc_c, pc_dt, dtb, arow, h0, h0)
    yl_f, yl_b, _, _ = _ssd_scan(xbc_l, pl_dt, dtb, arow, hc_f, hc_b)

    def token_mix(x, ga, proj_hy, y_f, y_b, xbc, z, proj_fn):
        y_hy = _token_mix_inputs(p, proj_hy, x.shape[1])
        y_fn = _fnet(proj_fn)
        return _out_proj(x, ga, y_hy, y_f, y_b, xbc, z, y_fn, p['dsk_row'], p['ng_row'], p['wo1'], p['wo2'], p['wo3'])

    xl = token_mix(xl, ga1, pl_hy, yl_f, yl_b, xbc_l, pl_z, pl_fn)
    xl = _peer(xl, p, sh2, sc2, ga2, g_final)
    if ctx_out:
        xc = token_mix(xc, cga1, pc_hy, yc_f, yc_b, xbc_c, pc_z, pc_fn)
        xc = _peer(xc, p, csh2, csc2, cga2)
    return xl, xc


def kernel(x, c, ctx, c_ctx, w_ada, b_ada, g_norm1, g_norm2, w_in, hy_conv_w, hy_conv_b, hf_w1, hf_b1, hf_w2, hf_b2, hf_w3, hf_freq, hy_bias, ssd_conv_w, ssd_conv_b, ssd_dt_bias, ssd_a_log, ssd_d, ssd_norm_g, w_out, peer_wq, peer_k1, peer_k2, peer_u, peer_v, g_final):
    depth = w_ada.shape[0]
    b, l, d = x.shape
    rows = -(-(b + 1) // SUBLANES) * SUBLANES
    cc = jnp.concatenate([c, c_ctx[None, :], jnp.zeros((rows - b - 1, d), F32)], axis=0)
    mods = _ada_mod(cc, w_ada, b_ada)
    xl, xc = x, ctx
    for i in range(depth):
        wo = w_out[i].astype(BF16)
        p = {
            'g_norm1': g_norm1[i].reshape(1, d), 'g_norm2': g_norm2[i].reshape(1, d),
            'w_in': w_in[i].astype(BF16),
            'hy_conv_w': hy_conv_w[i], 'hy_conv_b': hy_conv_b[i],
            'hf_w1': hf_w1[i], 'hf_b1': hf_b1[i], 'hf_w2': hf_w2[i], 'hf_b2': hf_b2[i], 'hf_w3': hf_w3[i],
            'hf_freq': hf_freq[i], 'hy_bias': hy_bias[i],
            'ssd_conv_w': ssd_conv_w[i], 'ssd_conv_b': ssd_conv_b[i],
            'dtb_row': _pad_lanes(ssd_dt_bias[i].reshape(1, 2 * SSD_HEADS)),
            'a_row': _pad_lanes(-jnp.exp(ssd_a_log[i].astype(F32)).reshape(1, 2 * SSD_HEADS)),
            'dsk_row': jnp.repeat(ssd_d[i].astype(F32), SSD_HEAD_DIM).reshape(1, SSD_INNER),
            'ng_row': ssd_norm_g[i].reshape(1, SSD_INNER),
            'wo1': wo[:HY_WIDTH], 'wo2': wo[HY_WIDTH:HY_WIDTH + SSD_INNER], 'wo3': wo[HY_WIDTH + SSD_INNER:],
            'wq': peer_wq[i].astype(BF16), 'k1': peer_k1[i].astype(BF16), 'k2': peer_k2[i].astype(BF16),
            'u': (peer_u[i] * PEER_SCALE).astype(F8), 'vt': peer_v[i].astype(F8).T,
        }
        last = i == depth - 1
        xl, xc = _layer(xl, xc, mods[i, :b], mods[i, b], p, not last, g_final.reshape(1, d) if last else None)
    return xl
```

```python
import functools
import math

import jax
import jax.numpy as jnp
import numpy as np
from jax import lax
from jax.experimental import pallas as pl
from jax.experimental.pallas import tpu as pltpu

F32 = jnp.float32
BF16 = jnp.bfloat16
F8 = jnp.float8_e4m3fn
HI = lax.Precision.HIGHEST

EPS = 1e-6
LANES = 128
SUBLANES = 8
VMEM_LIMIT = 56 * 1024 * 1024

HY_WIDTH = 256
HY_EMB = 33
HY_BANDS = (HY_EMB - 1) // 2
HY_DECAY_TARGET = 1e-2
HY_SHORT_DECAY_PCT = 0.3
HY_LONG_DECAY_PCT = 1.5
SSD_HEADS = 8
SSD_HEAD_DIM = 64
SSD_INNER = SSD_HEADS * SSD_HEAD_DIM
SSD_GROUPS = 2
SSD_STATE = 128
SSD_CHUNK = 128
SSD_XBC = SSD_INNER + 2 * SSD_GROUPS * SSD_STATE
FN_WIDTH = 256
FN_GROUPS = 4
FN_GROUP_DIM = FN_WIDTH // FN_GROUPS
PEER_HEADS = 8
PEER_TOPK = 16
N_KEYS = 128
PEER_HALF = 128
OFF_HY = 0
OFF_Z = OFF_HY + 3 * HY_WIDTH
OFF_XBC = OFF_Z + SSD_INNER
OFF_DT = OFF_XBC + SSD_XBC
OFF_FN = OFF_DT + 2 * SSD_HEADS
D_IN_PROJ = OFF_FN + FN_WIDTH


def _params(*sem):
    return pltpu.CompilerParams(dimension_semantics=sem, vmem_limit_bytes=VMEM_LIMIT)


def _silu(x):
    return x * jax.nn.sigmoid(x)


def _softplus(x):
    return jnp.maximum(x, 0.0) + jnp.log1p(jnp.exp(-jnp.abs(x)))


def _gelu_tanh(x):
    return 0.5 * x * (1.0 + jnp.tanh(math.sqrt(2.0 / math.pi) * (x + 0.044715 * (x * x * x))))


def _ada_kernel(c_ref, w_ref, b_ref, o_ref):
    s = _silu(c_ref[...])
    o_ref[0] = jnp.dot(s, w_ref[0], preferred_element_type=F32, precision=HI) + b_ref[0]


def _ada_mod(cc, w_ada, b_ada):
    depth, d, n = w_ada.shape
    r = cc.shape[0]
    tn = 1536
    return pl.pallas_call(
        _ada_kernel,
        out_shape=jax.ShapeDtypeStruct((depth, r, n), F32),
        grid=(depth, n // tn),
        in_specs=[pl.BlockSpec((r, d), lambda l, j: (0, 0)),
                  pl.BlockSpec((1, d, tn), lambda l, j: (l, 0, j)),
                  pl.BlockSpec((1, 1, tn), lambda l, j: (l, 0, j))],
        out_specs=pl.BlockSpec((1, r, tn), lambda l, j: (l, 0, j)),
        compiler_params=_params("arbitrary", "arbitrary"),
        name="ada_mod",
    )(cc, w_ada, b_ada.reshape(depth, 1, n))


def _normmod(x, g, sh, sc):
    y = x * lax.rsqrt(jnp.mean(x * x, axis=-1, keepdims=True) + EPS) * g
    return y * (1.0 + sc) + sh


def _inproj_kernel(x_ref, g_ref, sh_ref, sc_ref, *rest, n_w):
    w_refs, o_refs = rest[:n_w], rest[n_w:]
    hb = _normmod(x_ref[0], g_ref[...], sh_ref[0], sc_ref[0]).astype(BF16)
    for w_ref, o_ref in zip(w_refs, o_refs):
        o_ref[0] = jnp.dot(hb, w_ref[...], preferred_element_type=F32).astype(o_ref.dtype)


def _in_proj(x, g, sh, sc, ws, dtypes):
    b, l, d = x.shape
    tm = min(512, l)
    return pl.pallas_call(
        functools.partial(_inproj_kernel, n_w=len(ws)),
        out_shape=[jax.ShapeDtypeStruct((b, l, w.shape[1]), dt) for w, dt in zip(ws, dtypes)],
        grid=(b, l // tm),
        in_specs=[pl.BlockSpec((1, tm, d), lambda i, j: (i, j, 0)),
                  pl.BlockSpec((1, d), lambda i, j: (0, 0)),
                  pl.BlockSpec((1, 1, d), lambda i, j: (i, 0, 0)),
                  pl.BlockSpec((1, 1, d), lambda i, j: (i, 0, 0))]
        + [pl.BlockSpec(w.shape, lambda i, j: (0, 0)) for w in ws],
        out_specs=[pl.BlockSpec((1, tm, w.shape[1]), lambda i, j: (i, j, 0)) for w in ws],
        compiler_params=_params("arbitrary", "arbitrary"),
        name="in_proj",
    )(x, g, sh, sc, *ws)


def _conv3(u, w, bias):
    l = u.shape[0]
    row = lax.broadcasted_iota(jnp.int32, u.shape, 0)
    um = jnp.where(row == 0, 0.0, pltpu.roll(u, 1, 0))
    up = jnp.where(row == l - 1, 0.0, pltpu.roll(u, l - 1, 0))
    return um * w[0:1, :] + u * w[1:2, :] + up * w[2:3, :] + bias


def _split3(x):
    hi = x.astype(BF16)
    r1 = x - hi.astype(F32)
    mid = r1.astype(BF16)
    lo = (r1 - mid.astype(F32)).astype(BF16)
    return hi, mid, lo


def _ssd_chunk(xbc, dt_raw, dtb, arow, st_ref, direction):
    q = SSD_CHUNK
    x = xbc[:, :SSD_INNER]
    gn = SSD_GROUPS * SSD_STATE
    bm = xbc[:, SSD_INNER:SSD_INNER + gn].astype(BF16)
    cm = xbc[:, SSD_INNER + gn:].astype(BF16)
    dtp = _softplus(dt_raw + dtb)
    da = dtp * arow
    ii = lax.broadcasted_iota(jnp.int32, (q, q), 0)
    jj = lax.broadcasted_iota(jnp.int32, (q, q), 1)
    tri = (ii >= jj) if direction == 0 else (ii <= jj)
    tri_b = tri.astype(BF16)
    cs = jnp.dot(jnp.concatenate([tri_b] * 3, axis=1), jnp.concatenate(_split3(da), axis=0),
                 preferred_element_type=F32)
    cs_t = cs.T
    lane = lax.broadcasted_iota(jnp.int32, (LANES, SSD_INNER), 0)
    chan = lax.broadcasted_iota(jnp.int32, (LANES, SSD_INNER), 1)
    head_of = lax.shift_right_logical(chan, int(math.log2(SSD_HEAD_DIM)))
    expand = (lane == direction * SSD_HEADS + head_of).astype(BF16)
    expand3 = jnp.concatenate([expand] * 3, axis=0)
    both = jnp.concatenate([jnp.concatenate(_split3(cs), axis=1), jnp.concatenate(_split3(dtp), axis=1)], axis=0)
    wide = jnp.dot(both, expand3, preferred_element_type=F32)
    acum, dtf = wide[:q], wide[q:]
    tot = acum[q - 1:q, :] if direction == 0 else acum[0:1, :]
    xdt = x * dtf
    ea = jnp.exp(acum)
    xdec = (jnp.exp(tot - acum) * xdt).astype(BF16)
    cd = jnp.exp(tot)
    xdt_b = xdt.astype(BF16)
    hg = SSD_HEADS // SSD_GROUPS
    gw = hg * SSD_HEAD_DIM
    outs = []
    for g in range(SSD_GROUPS):
        bg = bm[:, g * SSD_STATE:(g + 1) * SSD_STATE]
        cg = cm[:, g * SSD_STATE:(g + 1) * SSD_STATE]
        cb = lax.dot_general(cg, bg, (((1,), (1,)), ((), ())), preferred_element_type=F32)
        st = st_ref[g]
        y_g = jnp.dot(cg, st.astype(BF16), preferred_element_type=F32) * ea[:, g * gw:(g + 1) * gw]
        xg = xdt_b[:, g * gw:(g + 1) * gw]
        head_g = lax.shift_right_logical(lax.broadcasted_iota(jnp.int32, xg.shape, 1), int(math.log2(SSD_HEAD_DIM)))
        for k in range(hg):
            ln = direction * SSD_HEADS + g * hg + k
            seg = cs[:, ln:ln + 1] - cs_t[ln:ln + 1, :]
            lmat = jnp.where(tri, jnp.exp(jnp.where(tri, seg, 0.0)), 0.0)
            m = (cb * lmat).astype(BF16)
            y_g = y_g + jnp.dot(m, jnp.where(head_g == k, xg, jnp.zeros_like(xg)), preferred_element_type=F32)
        outs.append(y_g)
        sg = lax.dot_general(bg, xdec[:, g * gw:(g + 1) * gw], (((0,), (0,)), ((), ())),
                             preferred_element_type=F32)
        st_ref[g] = st * cd[:, g * gw:(g + 1) * gw] + sg
    return jnp.concatenate(outs, axis=1)


def _conv3_silu_chunk(cur_ref, before_ref, after_ref, w, bias, chunk, nc):
    cur = cur_ref[0]
    q = cur.shape[0]
    row = lax.broadcasted_iota(jnp.int32, cur.shape, 0)
    prev_row = jnp.where(chunk == 0, 0.0, before_ref[0, SUBLANES - 1:SUBLANES, :])
    next_row = jnp.where(chunk == nc - 1, 0.0, after_ref[0, 0:1, :])
    um = jnp.where(row == 0, prev_row, pltpu.roll(cur, 1, 0))
    up = jnp.where(row == q - 1, next_row, pltpu.roll(cur, q - 1, 0))
    return _silu(um * w[0:1, :] + cur * w[1:2, :] + up * w[2:3, :] + bias)


def _ssd_kernel(xf_ref, xf0_ref, xf1_ref, dtf_ref, xb_ref, xb0_ref, xb1_ref, dtb_ref, cw_ref, cb_ref,
                bias_ref, arow_ref, h0f_ref, h0b_ref, yf_ref, yb_ref, xs_ref, hlf_ref, hlb_ref,
                stf_ref, stb_ref, *, nc):
    c = pl.program_id(1)

    @pl.when(c == 0)
    def _():
        stf_ref[...] = h0f_ref[0]
        stb_ref[...] = h0b_ref[0]

    xf = _conv3_silu_chunk(xf_ref, xf0_ref, xf1_ref, cw_ref[...], cb_ref[...], c, nc)
    xs_ref[0] = xf[:, :SSD_INNER]
    yf_ref[0] = _ssd_chunk(xf, dtf_ref[0], bias_ref[...], arow_ref[...], stf_ref, 0)
    xb = _conv3_silu_chunk(xb_ref, xb0_ref, xb1_ref, cw_ref[...], cb_ref[...], nc - 1 - c, nc)
    yb_ref[0] = _ssd_chunk(xb, dtb_ref[0], bias_ref[...], arow_ref[...], stb_ref, 1)

    @pl.when(c == nc - 1)
    def _():
        hlf_ref[0] = stf_ref[...]
        hlb_ref[0] = stb_ref[...]


def _ssd_scan(xbc, dt, conv_w, conv_b, dtb_row, a_row, h0f, h0b):
    b, l, _ = xbc.shape
    nc = l // SSD_CHUNK
    per = SSD_CHUNK // SUBLANES
    last = l // SUBLANES - 1
    fwd = lambda i, c: (i, c, 0)
    bwd = lambda i, c: (i, nc - 1 - c, 0)
    before = lambda chunk_of: (lambda i, c: (i, jnp.maximum(chunk_of(c) * per - 1, 0), 0))
    after = lambda chunk_of: (lambda i, c: (i, jnp.minimum(chunk_of(c) * per + per, last), 0))
    f_of, b_of = (lambda c: c), (lambda c: nc - 1 - c)
    halo = lambda imap: pl.BlockSpec((1, SUBLANES, SSD_XBC), imap)
    st_shape = (SSD_GROUPS, SSD_STATE, SSD_INNER // SSD_GROUPS)
    st_spec = pl.BlockSpec((1,) + st_shape, lambda i, c: (i, 0, 0, 0))
    row = pl.BlockSpec((1, LANES), lambda i, c: (0, 0))
    return pl.pallas_call(
        functools.partial(_ssd_kernel, nc=nc),
        out_shape=[jax.ShapeDtypeStruct((b, l, SSD_INNER), F32)] * 3
        + [jax.ShapeDtypeStruct((b,) + st_shape, F32)] * 2,
        grid=(b, nc),
        in_specs=[pl.BlockSpec((1, SSD_CHUNK, SSD_XBC), fwd), halo(before(f_of)), halo(after(f_of)),
                  pl.BlockSpec((1, SSD_CHUNK, LANES), fwd),
                  pl.BlockSpec((1, SSD_CHUNK, SSD_XBC), bwd), halo(before(b_of)), halo(after(b_of)),
                  pl.BlockSpec((1, SSD_CHUNK, LANES), bwd),
                  pl.BlockSpec((3, SSD_XBC), lambda i, c: (0, 0)), pl.BlockSpec((1, SSD_XBC), lambda i, c: (0, 0)),
                  row, row, st_spec, st_spec],
        out_specs=[pl.BlockSpec((1, SSD_CHUNK, SSD_INNER), fwd), pl.BlockSpec((1, SSD_CHUNK, SSD_INNER), bwd),
                   pl.BlockSpec((1, SSD_CHUNK, SSD_INNER), fwd), st_spec, st_spec],
        scratch_shapes=[pltpu.VMEM(st_shape, F32), pltpu.VMEM(st_shape, F32)],
        compiler_params=_params("arbitrary", "arbitrary"),
        name="ssd_scan",
    )(xbc, xbc, xbc, dt, xbc, xbc, xbc, dt, conv_w, conv_b.reshape(1, SSD_XBC), dtb_row, a_row, h0f, h0b)


def _hy_filter_kernel(z_ref, w1_ref, b1_ref, w2_ref, b2_ref, w3_ref, fr_ref, win_ref, alt_ref,
                      fs_ref, fd_ref, nyq_ref):
    fr = fr_ref[...]
    h = jnp.sin(fr * (jnp.dot(z_ref[...], w1_ref[...], preferred_element_type=F32, precision=HI) + b1_ref[...]))
    h = jnp.sin(fr * (jnp.dot(h, w2_ref[...], preferred_element_type=F32, precision=HI) + b2_ref[...]))
    h = jnp.dot(h, w3_ref[...], preferred_element_type=F32, precision=HI)
    win = win_ref[...]
    row = lax.broadcasted_iota(jnp.int32, win.shape, 0)
    hf = h[:, :HY_WIDTH] * win
    hb = jnp.where(row == 0, 0.0, h[:, HY_WIDTH:] * win)
    nrm = lax.rsqrt(jnp.sum(hf * hf, axis=0, keepdims=True) + jnp.sum(hb * hb, axis=0, keepdims=True) + EPS)
    fs = (hf + hb) * nrm
    fs_ref[...] = fs
    fd_ref[...] = (hb - hf) * nrm
    nyq_ref[...] = jnp.sum(fs * alt_ref[...], axis=0, keepdims=True)


def _hy_filter(l, w1, b1, w2, b2, w3, freq):
    t = np.linspace(0.0, 1.0, l, dtype=np.float32)[:, None]
    w = (np.float32(2.0 * math.pi) * np.arange(l, dtype=np.float32)[:, None] / np.float32(l)).astype(np.float32)
    f = np.linspace(1e-4, HY_BANDS - 1, HY_BANDS, dtype=np.float32)[None, :]
    zf = jnp.asarray(f) * jnp.asarray(w)
    z = jnp.concatenate([jnp.asarray(t), jnp.cos(zf), -jnp.sin(zf)], axis=-1)
    z = jnp.pad(z, ((0, 0), (0, LANES - HY_EMB)))
    hid = w1.shape[1]
    w1p = jnp.pad(w1, ((0, LANES - HY_EMB), (0, LANES - hid)))
    w2p = jnp.pad(w2, ((0, LANES - hid), (0, LANES - hid)))
    w3p = jnp.pad(w3, ((0, LANES - hid), (0, 0)))
    pad_row = lambda v: jnp.pad(v.reshape(1, hid), ((0, 0), (0, LANES - hid)))
    max_decay = math.log(HY_DECAY_TARGET) / HY_SHORT_DECAY_PCT
    min_decay = math.log(HY_DECAY_TARGET) / HY_LONG_DECAY_PCT
    deltas = jnp.abs(jnp.linspace(min_decay, max_decay, HY_WIDTH, dtype=F32))
    win = jnp.exp(-jnp.asarray(t) * deltas)
    alt = jnp.asarray(np.where(np.arange(l) % 2 == 0, 1.0, -1.0).astype(np.float32)[:, None] * np.ones((1, HY_WIDTH), np.float32))
    return pl.pallas_call(
        _hy_filter_kernel,
        out_shape=[jax.ShapeDtypeStruct((l, HY_WIDTH), F32), jax.ShapeDtypeStruct((l, HY_WIDTH), F32),
                   jax.ShapeDtypeStruct((1, HY_WIDTH), F32)],
        compiler_params=pltpu.CompilerParams(vmem_limit_bytes=VMEM_LIMIT),
        name="hy_filter",
    )(z, w1p, pad_row(b1), w2p, pad_row(b2), w3p, pad_row(freq), win, alt)


def _mm_kernel(a_ref, b_ref, o_ref):
    o_ref[...] = jnp.dot(a_ref[...], b_ref[...], preferred_element_type=F32)


def _mm(a, b, tm):
    m, k = a.shape
    n = b.shape[1]
    return pl.pallas_call(
        _mm_kernel,
        out_shape=jax.ShapeDtypeStruct((m, n), F32),
        grid=(m // tm,),
        in_specs=[pl.BlockSpec((tm, k), lambda i: (i, 0)), pl.BlockSpec((k, n), lambda i: (0, 0))],
        out_specs=pl.BlockSpec((tm, n), lambda i: (i, 0)),
        compiler_params=_params("arbitrary"),
        name="mm",
    )(a, b)


def _dft_tables(l):
    n = 2 * l
    k = np.arange(l, dtype=np.int64)[:, None]
    s = np.arange(l, dtype=np.int64)[None, :]
    ang = (2.0 * np.pi / n) * ((k * s) % n).astype(np.float64)
    cos, sin = np.cos(ang), np.sin(ang)
    alt = np.where(np.arange(l) % 2 == 0, 1.0, -1.0)
    sin[0, :] = alt
    fwd = np.concatenate([cos, sin], axis=0)
    wgt = np.full((l, 1), 2.0)
    wgt[0, 0] = 1.0
    sin_i = -sin * wgt
    sin_i[0, :] = alt
    inv = np.concatenate([(cos * wgt).T, sin_i.T], axis=1) / n
    return fwd, inv


def _batch_rows(b):
    return 2 if b % 2 == 0 else 1


def _hy_conv_kernel(u_ref, cw_ref, cb_ref, ff_ref, fi_ref, ka_ref, kb_ref, kc_ref, bias_ref, o_ref,
                    ub_ref, uf_ref, x1_ref, acc_ref, *, nf):
    f = pl.program_id(1)
    w = HY_WIDTH
    rows = u_ref.shape[0]

    @pl.when(f == 0)
    def _():
        for r in range(rows):
            part = lambda k: _conv3(u_ref[r, :, k * w:(k + 1) * w].astype(F32), cw_ref[:, k * w:(k + 1) * w],
                                    cb_ref[:, k * w:(k + 1) * w])
            u = part(2) * part(0)
            cols = slice(r * w, (r + 1) * w)
            uf_ref[:, cols] = u
            ub_ref[:, cols] = u.astype(BF16)
            x1_ref[:, cols] = part(1)
        acc_ref[...] = jnp.zeros_like(acc_ref)

    uf = jnp.dot(ff_ref[0], ub_ref[...], preferred_element_type=F32)
    fb = uf.shape[0] // 2
    ur, us = uf[:fb], uf[fb:]
    ka, kb, kc = [jnp.concatenate([k_ref[...]] * rows, axis=1) for k_ref in (ka_ref, kb_ref, kc_ref)]
    p = jnp.concatenate([ur * ka + us * kb, ur * kb - us * kc], axis=0).astype(BF16)
    acc_ref[...] += jnp.dot(fi_ref[0], p, preferred_element_type=F32)

    @pl.when(f == nf - 1)
    def _():
        for r in range(rows):
            cols = slice(r * w, (r + 1) * w)
            o_ref[r] = x1_ref[:, cols] * (acc_ref[:, cols] + uf_ref[:, cols] * bias_ref[...])


def _hy_conv(proj, conv_w, conv_b, fs, fd, nyq, bias):
    b, l, _ = proj.shape
    rows = _batch_rows(b)
    fwd, inv = _dft_tables(l)
    fb = min(256, l)
    nf = l // fb
    ff = jnp.asarray(np.stack([np.concatenate([fwd[i * fb:(i + 1) * fb], fwd[l + i * fb:l + (i + 1) * fb]], axis=0)
                               for i in range(nf)]), dtype=BF16)
    fi = jnp.asarray(np.stack([np.concatenate([inv[:, i * fb:(i + 1) * fb], inv[:, l + i * fb:l + (i + 1) * fb]], axis=1)
                               for i in range(nf)]), dtype=BF16)
    fwd_b = jnp.asarray(fwd, dtype=BF16)
    tm = min(512, l)
    k_r = _mm(fwd_b[:l], fs.astype(BF16), tm)
    k_i = _mm(fwd_b[l:], fd.astype(BF16), tm)
    first = (jnp.arange(l) == 0)[:, None]
    ka = k_r
    kb = jnp.where(first, 0.0, k_i)
    kc = jnp.where(first, -nyq, k_r)
    return pl.pallas_call(
        functools.partial(_hy_conv_kernel, nf=nf),
        out_shape=jax.ShapeDtypeStruct((b, l, HY_WIDTH), F32),
        grid=(b // rows, nf),
        in_specs=[pl.BlockSpec((rows, l, 3 * HY_WIDTH), lambda i, f: (i, 0, 0)),
                  pl.BlockSpec((3, 3 * HY_WIDTH), lambda i, f: (0, 0)),
                  pl.BlockSpec((1, 3 * HY_WIDTH), lambda i, f: (0, 0)),
                  pl.BlockSpec((1, 2 * fb, l), lambda i, f: (f, 0, 0)),
                  pl.BlockSpec((1, l, 2 * fb), lambda i, f: (f, 0, 0)),
                  pl.BlockSpec((fb, HY_WIDTH), lambda i, f: (f, 0)),
                  pl.BlockSpec((fb, HY_WIDTH), lambda i, f: (f, 0)),
                  pl.BlockSpec((fb, HY_WIDTH), lambda i, f: (f, 0)),
                  pl.BlockSpec((1, HY_WIDTH), lambda i, f: (0, 0))],
        out_specs=pl.BlockSpec((rows, l, HY_WIDTH), lambda i, f: (i, 0, 0)),
        scratch_shapes=[pltpu.VMEM((l, rows * HY_WIDTH), BF16), pltpu.VMEM((l, rows * HY_WIDTH), F32),
                        pltpu.VMEM((l, rows * HY_WIDTH), F32), pltpu.VMEM((l, rows * HY_WIDTH), F32)],
        compiler_params=_params("arbitrary", "arbitrary"),
        name="hy_conv",
    )(proj, conv_w, conv_b.reshape(1, 3 * HY_WIDTH), ff, fi, ka, kb, kc, bias.reshape(1, HY_WIDTH))


def _fnet_kernel(x_ref, cc_ref, sc_ref, m_ref, o_ref, xx_ref):
    j = pl.program_id(1)
    rows, l, c = x_ref.shape

    @pl.when(j == 0)
    def _():
        for r in range(rows):
            xb = x_ref[r].astype(BF16)
            cols = slice(r * c, (r + 1) * c)
            xx_ref[:l, cols] = jnp.dot(xb, cc_ref[...], preferred_element_type=F32).astype(BF16)
            xx_ref[l:, cols] = jnp.dot(xb, sc_ref[...], preferred_element_type=F32).astype(BF16)

    y = jnp.dot(m_ref[...], xx_ref[...], preferred_element_type=F32)
    for r in range(rows):
        o_ref[r] = y[:, r * c:(r + 1) * c]


def _fnet(x):
    b, l, c = x.shape
    rows = _batch_rows(b)
    gd = FN_GROUP_DIM
    kc = np.arange(gd)[:, None] * np.arange(gd)[None, :]
    ang_c = 2.0 * np.pi * (kc % gd) / gd
    eye = np.eye(FN_GROUPS)
    cblk = jnp.asarray(np.kron(eye, np.cos(ang_c)), dtype=BF16)
    sblk = jnp.asarray(np.kron(eye, np.sin(ang_c)), dtype=BF16)
    kl = (np.arange(l, dtype=np.int64)[:, None] * np.arange(l, dtype=np.int64)[None, :]) % l
    ang_l = 2.0 * np.pi * kl / l
    scale = 1.0 / math.sqrt(l * gd)
    mat = jnp.asarray(np.concatenate([np.cos(ang_l), -np.sin(ang_l)], axis=1) * scale, dtype=BF16)
    tr = min(512, l)
    return pl.pallas_call(
        _fnet_kernel,
        out_shape=jax.ShapeDtypeStruct((b, l, c), F32),
        grid=(b // rows, l // tr),
        in_specs=[pl.BlockSpec((rows, l, c), lambda i, r: (i, 0, 0)),
                  pl.BlockSpec((c, c), lambda i, r: (0, 0)),
                  pl.BlockSpec((c, c), lambda i, r: (0, 0)),
                  pl.BlockSpec((tr, 2 * l), lambda i, r: (r, 0))],
        out_specs=pl.BlockSpec((rows, tr, c), lambda i, r: (i, r, 0)),
        scratch_shapes=[pltpu.VMEM((2 * l, rows * c), BF16)],
        compiler_params=_params("arbitrary", "arbitrary"),
        name="fnet",
    )(x, cblk, sblk, mat)


def _outproj_kernel(x_ref, ga_ref, yhy_ref, yf_ref, yb_ref, xs_ref, z_ref, yfn_ref, dsk_ref, ng_ref,
                    w1_ref, w2_ref, w3_ref, o_ref):
    y = yf_ref[0] + yb_ref[0] + dsk_ref[...] * xs_ref[0]
    y = y * _silu(z_ref[0])
    gw = SSD_INNER // SSD_GROUPS
    parts = []
    for g in range(SSD_GROUPS):
        yg = y[:, g * gw:(g + 1) * gw]
        parts.append(yg * lax.rsqrt(jnp.mean(yg * yg, axis=-1, keepdims=True) + EPS))
    yn = jnp.concatenate(parts, axis=1) * ng_ref[...]
    out = jnp.dot(yhy_ref[0].astype(BF16), w1_ref[...], preferred_element_type=F32)
    out += jnp.dot(yn.astype(BF16), w2_ref[...], preferred_element_type=F32)
    out += jnp.dot(yfn_ref[0].astype(BF16), w3_ref[...], preferred_element_type=F32)
    o_ref[0] = x_ref[0] + ga_ref[0] * out


def _out_proj(x, ga, y_hy, y_f, y_b, xbc, z, y_fn, dsk_row, ng_row, w1, w2, w3):
    b, l, d = x.shape
    tm = min(512, l)
    tok = lambda n: pl.BlockSpec((1, tm, n), lambda i, j: (i, j, 0))
    const = lambda a: pl.BlockSpec(a.shape, lambda i, j: (0,) * a.ndim)
    return pl.pallas_call(
        _outproj_kernel,
        out_shape=jax.ShapeDtypeStruct((b, l, d), F32),
        grid=(b, l // tm),
        in_specs=[tok(d), pl.BlockSpec((1, 1, d), lambda i, j: (i, 0, 0)), tok(HY_WIDTH), tok(SSD_INNER),
                  tok(SSD_INNER), tok(SSD_INNER), tok(SSD_INNER), tok(FN_WIDTH), const(dsk_row), const(ng_row),
                  const(w1), const(w2), const(w3)],
        out_specs=tok(d),
        compiler_params=_params("arbitrary", "arbitrary"),
        name="out_proj",
    )(x, ga, y_hy, y_f, y_b, xbc, z, y_fn, dsk_row, ng_row, w1, w2, w3)


def _peer_scores(x_ref, g_ref, sh_ref, sc_ref, wq_ref, k1_ref, k2_ref, ht_ref, s1_ref, s2_ref):
    hm = _normmod(x_ref[...], g_ref[...], sh_ref[0], sc_ref[0])
    ht_ref[...] = hm.T.astype(F8)
    qb = jnp.dot(hm.astype(BF16), wq_ref[...], preferred_element_type=F32).astype(BF16)
    nt = (((1,), (1,)), ((), ()))
    for h in range(PEER_HEADS):
        q1 = qb[:, (2 * h) * PEER_HALF:(2 * h + 1) * PEER_HALF]
        q2 = qb[:, (2 * h + 1) * PEER_HALF:(2 * h + 2) * PEER_HALF]
        s1_ref[h] = lax.dot_general(k1_ref[...], q1, nt, preferred_element_type=F32)
        s2_ref[h] = lax.dot_general(k2_ref[...], q2, nt, preferred_element_type=F32)


def _sorting_network(n):
    pairs = []

    def merge(lo, m, r):
        step = 2 * r
        if step < m:
            merge(lo, m, step)
            merge(lo + r, m, step)
            for i in range(lo + r, lo + m - r, step):
                pairs.append((i, i + r))
        else:
            pairs.append((lo, lo + r))

    def sort(lo, m):
        if m > 1:
            half = m // 2
            sort(lo, half)
            sort(lo + half, half)
            merge(lo, m, 1)

    sort(0, n)
    return tuple(pairs)


NEG = -3.0e38
N_TOP = PEER_TOPK + 1


def _top_sorted(rows, n_out):
    rows = list(rows)
    for i, j in _sorting_network(len(rows)):
        rows[i], rows[j] = jnp.maximum(rows[i], rows[j]), jnp.minimum(rows[i], rows[j])
    sub = lax.broadcasted_iota(jnp.int32, rows[0].shape, 0)
    out = []
    for r in range(n_out):
        m = jnp.max(rows[0], axis=0, keepdims=True)
        out.append(m)
        first = jnp.min(jnp.where(rows[0] == m, sub, SUBLANES), axis=0, keepdims=True)
        hit = sub == first
        keep = min(len(rows), n_out - r - 1)
        rows = [jnp.where(hit, rows[k + 1] if k + 1 < len(rows) else NEG, rows[k]) for k in range(keep)]
    return out


def _count_above(vals, x, strict):
    r = jnp.zeros_like(x)
    for j, v in enumerate(vals):
        r = jnp.where((v > x) if strict else (v >= x), float(j + 1), r)
    return r


def _dup_bf16(x):
    hi = pltpu.bitcast(x.astype(BF16).astype(F32), jnp.uint32)
    return hi | (hi >> 16)


def _peer_stats_kernel(s1_ref, s2_ref, r2_ref, a2_ref, ns_ref, cc_ref):
    nrow = N_KEYS // SUBLANES
    t = s1_ref.shape[2]
    sub = lax.broadcasted_iota(jnp.int32, (SUBLANES, t), 0)
    for h in range(PEER_HEADS):
        s1_rows = [s1_ref[h, SUBLANES * k:SUBLANES * (k + 1), :] for k in range(nrow)]
        s2_rows = [s2_ref[h, SUBLANES * k:SUBLANES * (k + 1), :] for k in range(nrow)]
        v1 = _top_sorted(s1_rows, N_TOP)
        v2 = _top_sorted(s2_rows, N_TOP)
        cands = [v1[i] + v2[j] for i in range(N_TOP) for j in range(N_TOP) if (i + 1) * (j + 1) <= N_TOP]
        packed = []
        for k in range(0, len(cands), SUBLANES):
            blk = jnp.full((SUBLANES, t), NEG, F32)
            for s, cv in enumerate(cands[k:k + SUBLANES]):
                blk = jnp.where(sub == s, cv, blk)
            packed.append(blk)
        ids = [sub + SUBLANES * k for k in range(len(packed))]
        big = SUBLANES * len(packed)
        top = []
        for r in range(N_TOP):
            m = packed[0]
            for blk in packed[1:]:
                m = jnp.maximum(m, blk)
            m = jnp.max(m, axis=0, keepdims=True)
            top.append(m)
            sel = jnp.where(packed[0] == m, ids[0], big)
            for blk, idk in zip(packed[1:], ids[1:]):
                sel = jnp.minimum(sel, jnp.where(blk == m, idk, big))
            sel = jnp.min(sel, axis=0, keepdims=True)
            packed = [jnp.where(idk == sel, NEG, blk) for blk, idk in zip(packed, ids)]
        mx = top[0]
        z = jnp.zeros((1, t), F32)
        for r in range(PEER_TOPK):
            z = z + jnp.exp(top[r] - mx)
        tau = 0.5 * (top[PEER_TOPK - 1] + top[PEER_TOPK])
        cscale = 0.5 / z
        for k in range(0, nrow, 2):
            pair = slice(SUBLANES * k, SUBLANES * (k + 2))
            r2 = [_count_above(v2[:PEER_TOPK], s2_rows[k + q], True) for q in range(2)]
            r2_ref[h, pair, :] = jnp.concatenate(r2, axis=0).astype(BF16)
            a2 = [jnp.exp(s2_rows[k + q] - v2[0]) for q in range(2)]
            a2_ref[h, pair, :] = jnp.concatenate(a2, axis=0).astype(BF16)
        for k in range(nrow):
            blk = slice(SUBLANES * k, SUBLANES * (k + 1))
            ns_ref[h, blk, :] = _dup_bf16(_count_above(v2[:PEER_TOPK], tau - s1_rows[k], False))
            cc_ref[h, blk, :] = _dup_bf16(jnp.exp(s1_rows[k] - v1[0]) * cscale)


def _peer_select_kernel(x_ref, g_ref, sh_ref, sc_ref, wq_ref, k1_ref, k2_ref,
                        ht_ref, r2_ref, a2_ref, ns_ref, cc_ref, s1_ref, s2_ref):
    _peer_scores(x_ref, g_ref, sh_ref, sc_ref, wq_ref, k1_ref, k2_ref, ht_ref, s1_ref, s2_ref)
    _peer_stats_kernel(s1_ref, s2_ref, r2_ref, a2_ref, ns_ref, cc_ref)


def _peer_select(x2, g, sh, sc, wq, k1, k2, l):
    n, d = x2.shape
    tm = min(256, l)
    per = l // tm
    shp = (PEER_HEADS, N_KEYS, n)
    blk = pl.BlockSpec((PEER_HEADS, N_KEYS, tm), lambda i: (0, 0, i))
    return pl.pallas_call(
        _peer_select_kernel,
        out_shape=[jax.ShapeDtypeStruct((d, n), F8),
                   jax.ShapeDtypeStruct(shp, BF16), jax.ShapeDtypeStruct(shp, BF16),
                   jax.ShapeDtypeStruct(shp, jnp.uint32), jax.ShapeDtypeStruct(shp, jnp.uint32)],
        grid=(n // tm,),
        in_specs=[pl.BlockSpec((tm, d), lambda i: (i, 0)),
                  pl.BlockSpec((1, d), lambda i: (0, 0)),
                  pl.BlockSpec((1, 1, d), lambda i: (i // per, 0, 0)),
                  pl.BlockSpec((1, 1, d), lambda i: (i // per, 0, 0)),
                  pl.BlockSpec(wq.shape, lambda i: (0, 0)),
                  pl.BlockSpec(k1.shape, lambda i: (0, 0)),
                  pl.BlockSpec(k2.shape, lambda i: (0, 0))],
        out_specs=[pl.BlockSpec((d, tm), lambda i: (0, i)), blk, blk, blk, blk],
        scratch_shapes=[pltpu.VMEM((PEER_HEADS, N_KEYS, tm), F32), pltpu.VMEM((PEER_HEADS, N_KEYS, tm), F32)],
        compiler_params=_params("arbitrary"),
        name="peer_select",
    )(x2, g, sh, sc, wq, k1, k2)


E_TILE = 1024
E_BLOCK = 2 * E_TILE
PEER_SCALE = 32.0


def _gelu_tanh_x2_scaled(s):
    c0 = math.sqrt(2.0 / math.pi) / PEER_SCALE
    c1 = math.sqrt(2.0 / math.pi) * 0.044715 / PEER_SCALE ** 3
    return s + s * jnp.tanh(s * (c0 + c1 * (s * s)))


ROWS_PER_TILE = E_TILE // N_KEYS


def _rows_bf16(word_row):
    words = word_row | jnp.zeros((SUBLANES, word_row.shape[1]), jnp.uint32)
    tile = pltpu.bitcast(words, BF16)
    return jnp.concatenate([tile] * (N_KEYS // tile.shape[0]), axis=0)


def _peer_mix_kernel(ht_ref, u_ref, vt_ref, r2_ref, a2_ref, nsp_ref, ccp_ref, nsc_ref, ccc_ref, x_ref, ga_ref,
                     gf_ref, o_ref, st0_ref, st1_ref, act0_ref, act1_ref, acc_ref, *, nblk, final_norm):
    m = pl.program_id(1)

    def scores(half, st_ref):
        st_ref[...] = jnp.dot(u_ref[half * E_TILE:(half + 1) * E_TILE, :], ht_ref[...],
                              preferred_element_type=F32).astype(BF16)

    def gate(st_ref, act_ref, ns_ref, cc_ref):
        for e in range(ROWS_PER_TILE):
            g = None
            for h in range(PEER_HEADS):
                nb = _rows_bf16(ns_ref[h, e:e + 1, :])
                cb = _rows_bf16(cc_ref[h, e:e + 1, :])
                a2 = a2_ref[h]
                term = jnp.where(r2_ref[h] < nb, a2, jnp.zeros_like(a2)) * cb
                g = term if g is None else g + term
            rows = slice(e * N_KEYS, (e + 1) * N_KEYS)
            act_ref[rows, :] = (_gelu_tanh_x2_scaled(st_ref[rows, :]) * g).astype(F8)

    def accumulate(half, act_ref):
        acc_ref[...] += jnp.dot(vt_ref[:, half * E_TILE:(half + 1) * E_TILE], act_ref[...],
                                preferred_element_type=F32)

    @pl.when(m == 0)
    def _():
        acc_ref[...] = jnp.zeros_like(acc_ref)
        scores(0, st0_ref)
        gate(st0_ref, act0_ref, nsc_ref, ccc_ref)
        scores(1, st1_ref)

    @pl.when((m > 0) & (m < nblk))
    def _():
        accumulate(0, act0_ref)
        gate(st1_ref, act1_ref, nsp_ref, ccp_ref)
        scores(0, st0_ref)
        accumulate(1, act1_ref)
        gate(st0_ref, act0_ref, nsc_ref, ccc_ref)
        scores(1, st1_ref)

    @pl.when(m == nblk)
    def _():
        accumulate(0, act0_ref)
        gate(st1_ref, act1_ref, nsp_ref, ccp_ref)
        accumulate(1, act1_ref)
        y = x_ref[...] + ga_ref[0] * (acc_ref[...].T * (1.0 / PEER_SCALE))
        if final_norm:
            y = y * lax.rsqrt(jnp.mean(y * y, axis=-1, keepdims=True) + EPS) * gf_ref[...]
        o_ref[...] = y


def _peer_mix(ht, u, vt, r2, a2, ns, cc, x2, ga, l, g_final=None):
    d, n = ht.shape
    n_exp = u.shape[0]
    t = min(512, l)
    per = l // t
    nblk = n_exp // E_BLOCK
    n_tiles = n_exp // E_TILE
    sblk = pl.BlockSpec((PEER_HEADS, N_KEYS, t), lambda i, m: (0, 0, i))
    prev = pl.BlockSpec((PEER_HEADS, ROWS_PER_TILE, t), lambda i, m: (0, jnp.maximum(2 * m - 1, 0), i))
    cur = pl.BlockSpec((PEER_HEADS, ROWS_PER_TILE, t), lambda i, m: (0, jnp.minimum(2 * m, n_tiles - 1), i))
    return pl.pallas_call(
        functools.partial(_peer_mix_kernel, nblk=nblk, final_norm=g_final is not None),
        out_shape=jax.ShapeDtypeStruct((n, d), F32),
        grid=(n // t, nblk + 1),
        in_specs=[pl.BlockSpec((d, t), lambda i, m: (0, i)),
                  pl.BlockSpec((E_BLOCK, d), lambda i, m: (jnp.minimum(m, nblk - 1), 0)),
                  pl.BlockSpec((d, E_BLOCK), lambda i, m: (0, jnp.maximum(m - 1, 0))),
                  sblk, sblk, prev, prev, cur, cur,
                  pl.BlockSpec((t, d), lambda i, m: (i, 0)),
                  pl.BlockSpec((1, 1, d), lambda i, m: (i // per, 0, 0)),
                  pl.BlockSpec((1, d), lambda i, m: (0, 0))],
        out_specs=pl.BlockSpec((t, d), lambda i, m: (i, 0)),
        scratch_shapes=[pltpu.VMEM((E_TILE, t), BF16), pltpu.VMEM((E_TILE, t), BF16),
                        pltpu.VMEM((E_TILE, t), F8), pltpu.VMEM((E_TILE, t), F8),
                        pltpu.VMEM((d, t), F32)],
        compiler_params=_params("arbitrary", "arbitrary"),
        name="peer_mix",
    )(ht, u, vt, r2, a2, ns, cc, ns, cc, x2, ga, jnp.ones((1, d), F32) if g_final is None else g_final)


def _pad_lanes(a):
    return jnp.pad(a, ((0, 0), (0, LANES - a.shape[1])))


def _token_mix_inputs(p, proj_hy, l):
    fs, fd, nyq = _hy_filter(l, p['hf_w1'], p['hf_b1'], p['hf_w2'], p['hf_b2'], p['hf_w3'], p['hf_freq'])
    return _hy_conv(proj_hy, p['hy_conv_w'], p['hy_conv_b'], fs, fd, nyq, p['hy_bias'])


def _peer(x, p, sh, sc, ga, g_final=None):
    b, l, d = x.shape
    x2 = x.reshape(b * l, d)
    ht, r2, a2, ns, cc = _peer_select(x2, p['g_norm2'], sh, sc, p['wq'], p['k1'], p['k2'], l)
    out = _peer_mix(ht, p['u'], p['vt'], r2, a2, ns, cc, x2, ga, l, g_final)
    return out.reshape(b, l, d)


def _layer(xl, xc, mod_l, mod_c, p, ctx_out, g_final=None):
    b = xl.shape[0]
    d = xl.shape[2]
    sh1, sc1, ga1, sh2, sc2, ga2 = [m.reshape(b, 1, d) for m in jnp.split(mod_l, 6, axis=-1)]
    csh1, csc1, cga1, csh2, csc2, cga2 = [jnp.broadcast_to(m.reshape(1, 1, d), (b, 1, d))
                                          for m in jnp.split(mod_c, 6, axis=-1)]
    w_in = p['w_in']
    w_hy, w_z, w_xbc, w_fn = (w_in[:, OFF_HY:OFF_Z], w_in[:, OFF_Z:OFF_XBC], w_in[:, OFF_XBC:OFF_DT],
                              w_in[:, OFF_FN:D_IN_PROJ])
    w_dt = _pad_lanes(w_in[:, OFF_DT:OFF_FN])
    g1 = p['g_norm1']
    all_w, all_dt = [w_hy, w_z, w_xbc, w_dt, w_fn], [BF16, F32, F32, F32, BF16]
    pl_hy, pl_z, pl_xbc, pl_dt, pl_fn = _in_proj(xl, g1, sh1, sc1, all_w, all_dt)
    if ctx_out:
        pc_hy, pc_z, pc_xbc, pc_dt, pc_fn = _in_proj(xc, g1, csh1, csc1, all_w, all_dt)
    else:
        pc_xbc, pc_dt = _in_proj(xc, g1, csh1, csc1, [w_xbc, w_dt], [F32, F32])

    dtb, arow, cw, cb = p['dtb_row'], p['a_row'], p['ssd_conv_w'], p['ssd_conv_b']
    h0 = jnp.zeros((b, SSD_GROUPS, SSD_STATE, SSD_INNER // SSD_GROUPS), F32)
    yc_f, yc_b, xs_c, hc_f, hc_b = _ssd_scan(pc_xbc, pc_dt, cw, cb, dtb, arow, h0, h0)
    yl_f, yl_b, xs_l, _, _ = _ssd_scan(pl_xbc, pl_dt, cw, cb, dtb, arow, hc_f, hc_b)

    def token_mix(x, ga, proj_hy, y_f, y_b, xs, z, proj_fn):
        y_hy = _token_mix_inputs(p, proj_hy, x.shape[1])
        y_fn = _fnet(proj_fn)
        return _out_proj(x, ga, y_hy, y_f, y_b, xs, z, y_fn, p['dsk_row'], p['ng_row'], p['wo1'], p['wo2'], p['wo3'])

    xl = token_mix(xl, ga1, pl_hy, yl_f, yl_b, xs_l, pl_z, pl_fn)
    xl = _peer(xl, p, sh2, sc2, ga2, g_final)
    if ctx_out:
        xc = token_mix(xc, cga1, pc_hy, yc_f, yc_b, xs_c, pc_z, pc_fn)
        xc = _peer(xc, p, csh2, csc2, cga2)
    return xl, xc


def kernel(x, c, ctx, c_ctx, w_ada, b_ada, g_norm1, g_norm2, w_in, hy_conv_w, hy_conv_b, hf_w1, hf_b1, hf_w2, hf_b2, hf_w3, hf_freq, hy_bias, ssd_conv_w, ssd_conv_b, ssd_dt_bias, ssd_a_log, ssd_d, ssd_norm_g, w_out, peer_wq, peer_k1, peer_k2, peer_u, peer_v, g_final):
    depth = w_ada.shape[0]
    b, l, d = x.shape
    rows = -(-(b + 1) // SUBLANES) * SUBLANES
    cc = jnp.concatenate([c, c_ctx[None, :], jnp.zeros((rows - b - 1, d), F32)], axis=0)
    mods = _ada_mod(cc, w_ada, b_ada)
    xl, xc = x, ctx
    for i in range(depth):
        wo = w_out[i].astype(BF16)
        p = {
            'g_norm1': g_norm1[i].reshape(1, d), 'g_norm2': g_norm2[i].reshape(1, d),
            'w_in': w_in[i].astype(BF16),
            'hy_conv_w': hy_conv_w[i], 'hy_conv_b': hy_conv_b[i],
            'hf_w1': hf_w1[i], 'hf_b1': hf_b1[i], 'hf_w2': hf_w2[i], 'hf_b2': hf_b2[i], 'hf_w3': hf_w3[i],
            'hf_freq': hf_freq[i], 'hy_bias': hy_bias[i],
            'ssd_conv_w': ssd_conv_w[i], 'ssd_conv_b': ssd_conv_b[i],
            'dtb_row': _pad_lanes(ssd_dt_bias[i].reshape(1, 2 * SSD_HEADS)),
            'a_row': _pad_lanes(-jnp.exp(ssd_a_log[i].astype(F32)).reshape(1, 2 * SSD_HEADS)),
            'dsk_row': jnp.repeat(ssd_d[i].astype(F32), SSD_HEAD_DIM).reshape(1, SSD_INNER),
            'ng_row': ssd_norm_g[i].reshape(1, SSD_INNER),
            'wo1': wo[:HY_WIDTH], 'wo2': wo[HY_WIDTH:HY_WIDTH + SSD_INNER], 'wo3': wo[HY_WIDTH + SSD_INNER:],
            'wq': peer_wq[i].astype(BF16), 'k1': peer_k1[i].astype(BF16), 'k2': peer_k2[i].astype(BF16),
            'u': (peer_u[i] * PEER_SCALE).astype(F8), 'vt': peer_v[i].astype(F8).T,
        }
        last = i == depth - 1
        xl, xc = _layer(xl, xc, mods[i, :b], mods[i, b], p, not last, g_final.reshape(1, d) if last else None)
    return xl
```

```python
import functools
import math

import jax
import jax.numpy as jnp
import numpy as np
from jax import lax
from jax.experimental import pallas as pl
from jax.experimental.pallas import tpu as pltpu

F32 = jnp.float32
BF16 = jnp.bfloat16
F8 = jnp.float8_e4m3fn
HI = lax.Precision.HIGHEST

EPS = 1e-6
LANES = 128
SUBLANES = 8
VMEM_LIMIT = 56 * 1024 * 1024

ADA_COL_TILE = 1536
TOKEN_TILE = 512
SELECT_TOKENS = 256
DFT_BIN_TILE = 256

HY_WIDTH = 256
HY_EMB = 33
HY_BANDS = (HY_EMB - 1) // 2
HY_DECAY_TARGET = 1e-2
HY_SHORT_DECAY_PCT = 0.3
HY_LONG_DECAY_PCT = 1.5
SSD_HEADS = 8
SSD_HEAD_DIM = 64
SSD_INNER = SSD_HEADS * SSD_HEAD_DIM
SSD_GROUPS = 2
SSD_STATE = 128
SSD_CHUNK = 128
SSD_XBC = SSD_INNER + 2 * SSD_GROUPS * SSD_STATE
FN_WIDTH = 256
FN_GROUPS = 4
FN_GROUP_DIM = FN_WIDTH // FN_GROUPS
PEER_HEADS = 8
PEER_TOPK = 16
N_KEYS = 128
PEER_HALF = 128
OFF_HY = 0
OFF_Z = OFF_HY + 3 * HY_WIDTH
OFF_XBC = OFF_Z + SSD_INNER
OFF_DT = OFF_XBC + SSD_XBC
OFF_FN = OFF_DT + 2 * SSD_HEADS
D_IN_PROJ = OFF_FN + FN_WIDTH


def _params(*sem):
    return pltpu.CompilerParams(dimension_semantics=sem, vmem_limit_bytes=VMEM_LIMIT)


def _silu(x):
    return x * jax.nn.sigmoid(x)


def _softplus(x):
    return jnp.maximum(x, 0.0) + jnp.log1p(jnp.exp(-jnp.abs(x)))


def _gelu_tanh(x):
    return 0.5 * x * (1.0 + jnp.tanh(math.sqrt(2.0 / math.pi) * (x + 0.044715 * (x * x * x))))


def _ada_kernel(c_ref, w_ref, b_ref, o_ref):
    s = _silu(c_ref[...])
    o_ref[0] = jnp.dot(s, w_ref[0], preferred_element_type=F32, precision=HI) + b_ref[0]


def _ada_mod(cc, w_ada, b_ada):
    depth, d, n = w_ada.shape
    r = cc.shape[0]
    tn = ADA_COL_TILE
    return pl.pallas_call(
        _ada_kernel,
        out_shape=jax.ShapeDtypeStruct((depth, r, n), F32),
        grid=(depth, n // tn),
        in_specs=[pl.BlockSpec((r, d), lambda l, j: (0, 0)),
                  pl.BlockSpec((1, d, tn), lambda l, j: (l, 0, j)),
                  pl.BlockSpec((1, 1, tn), lambda l, j: (l, 0, j))],
        out_specs=pl.BlockSpec((1, r, tn), lambda l, j: (l, 0, j)),
        compiler_params=_params("arbitrary", "arbitrary"),
        name="ada_mod",
    )(cc, w_ada, b_ada.reshape(depth, 1, n))


def _normmod(x, g, sh, sc):
    y = x * lax.rsqrt(jnp.mean(x * x, axis=-1, keepdims=True) + EPS) * g
    return y * (1.0 + sc) + sh


def _inproj_kernel(x_ref, g_ref, sh_ref, sc_ref, *rest, n_w):
    w_refs, o_refs = rest[:n_w], rest[n_w:]
    hb = _normmod(x_ref[0], g_ref[...], sh_ref[0], sc_ref[0]).astype(BF16)
    for w_ref, o_ref in zip(w_refs, o_refs):
        o_ref[0] = jnp.dot(hb, w_ref[...], preferred_element_type=F32).astype(o_ref.dtype)


def _in_proj(x, g, sh, sc, ws, dtypes):
    b, l, d = x.shape
    tm = min(TOKEN_TILE, l)
    return pl.pallas_call(
        functools.partial(_inproj_kernel, n_w=len(ws)),
        out_shape=[jax.ShapeDtypeStruct((b, l, w.shape[1]), dt) for w, dt in zip(ws, dtypes)],
        grid=(b, l // tm),
        in_specs=[pl.BlockSpec((1, tm, d), lambda i, j: (i, j, 0)),
                  pl.BlockSpec((1, d), lambda i, j: (0, 0)),
                  pl.BlockSpec((1, 1, d), lambda i, j: (i, 0, 0)),
                  pl.BlockSpec((1, 1, d), lambda i, j: (i, 0, 0))]
        + [pl.BlockSpec(w.shape, lambda i, j: (0, 0)) for w in ws],
        out_specs=[pl.BlockSpec((1, tm, w.shape[1]), lambda i, j: (i, j, 0)) for w in ws],
        compiler_params=_params("arbitrary", "arbitrary"),
        name="in_proj",
    )(x, g, sh, sc, *ws)


def _conv3(u, w, bias):
    l = u.shape[0]
    row = lax.broadcasted_iota(jnp.int32, u.shape, 0)
    um = jnp.where(row == 0, 0.0, pltpu.roll(u, 1, 0))
    up = jnp.where(row == l - 1, 0.0, pltpu.roll(u, l - 1, 0))
    return um * w[0:1, :] + u * w[1:2, :] + up * w[2:3, :] + bias


def _split3(x):
    hi = x.astype(BF16)
    r1 = x - hi.astype(F32)
    mid = r1.astype(BF16)
    lo = (r1 - mid.astype(F32)).astype(BF16)
    return hi, mid, lo


def _ssd_chunk(xbc, dt_raw, dtb, arow, st_ref, direction):
    q = SSD_CHUNK
    x = xbc[:, :SSD_INNER]
    gn = SSD_GROUPS * SSD_STATE
    bm = xbc[:, SSD_INNER:SSD_INNER + gn].astype(BF16)
    cm = xbc[:, SSD_INNER + gn:].astype(BF16)
    dtp = _softplus(dt_raw + dtb)
    da = dtp * arow
    ii = lax.broadcasted_iota(jnp.int32, (q, q), 0)
    jj = lax.broadcasted_iota(jnp.int32, (q, q), 1)
    tri = (ii >= jj) if direction == 0 else (ii <= jj)
    tri_b = tri.astype(BF16)
    cs = jnp.dot(jnp.concatenate([tri_b] * 3, axis=1), jnp.concatenate(_split3(da), axis=0),
                 preferred_element_type=F32)
    cs_t = cs.T
    lane = lax.broadcasted_iota(jnp.int32, (LANES, SSD_INNER), 0)
    chan = lax.broadcasted_iota(jnp.int32, (LANES, SSD_INNER), 1)
    head_of = lax.shift_right_logical(chan, int(math.log2(SSD_HEAD_DIM)))
    expand = (lane == direction * SSD_HEADS + head_of).astype(BF16)
    expand3 = jnp.concatenate([expand] * 3, axis=0)
    both = jnp.concatenate([jnp.concatenate(_split3(cs), axis=1), jnp.concatenate(_split3(dtp), axis=1)], axis=0)
    wide = jnp.dot(both, expand3, preferred_element_type=F32)
    acum, dtf = wide[:q], wide[q:]
    tot = acum[q - 1:q, :] if direction == 0 else acum[0:1, :]
    xdt = x * dtf
    ea = jnp.exp(acum)
    xdec = (jnp.exp(tot - acum) * xdt).astype(BF16)
    cd = jnp.exp(tot)
    xdt_b = xdt.astype(BF16)
    hg = SSD_HEADS // SSD_GROUPS
    gw = hg * SSD_HEAD_DIM
    outs = []
    for g in range(SSD_GROUPS):
        bg = bm[:, g * SSD_STATE:(g + 1) * SSD_STATE]
        cg = cm[:, g * SSD_STATE:(g + 1) * SSD_STATE]
        cb = lax.dot_general(cg, bg, (((1,), (1,)), ((), ())), preferred_element_type=F32)
        st = st_ref[g]
        y_g = jnp.dot(cg, st.astype(BF16), preferred_element_type=F32) * ea[:, g * gw:(g + 1) * gw]
        xg = xdt_b[:, g * gw:(g + 1) * gw]
        head_g = lax.shift_right_logical(lax.broadcasted_iota(jnp.int32, xg.shape, 1), int(math.log2(SSD_HEAD_DIM)))
        for k in range(hg):
            ln = direction * SSD_HEADS + g * hg + k
            seg = cs[:, ln:ln + 1] - cs_t[ln:ln + 1, :]
            lmat = jnp.where(tri, jnp.exp(jnp.where(tri, seg, 0.0)), 0.0)
            m = (cb * lmat).astype(BF16)
            y_g = y_g + jnp.dot(m, jnp.where(head_g == k, xg, jnp.zeros_like(xg)), preferred_element_type=F32)
        outs.append(y_g)
        sg = lax.dot_general(bg, xdec[:, g * gw:(g + 1) * gw], (((0,), (0,)), ((), ())),
                             preferred_element_type=F32)
        st_ref[g] = st * cd[:, g * gw:(g + 1) * gw] + sg
    return jnp.concatenate(outs, axis=1)


def _conv3_silu_chunk(cur_ref, before_ref, after_ref, w, bias, chunk, nc):
    cur = cur_ref[0]
    q = cur.shape[0]
    row = lax.broadcasted_iota(jnp.int32, cur.shape, 0)
    prev_row = jnp.where(chunk == 0, 0.0, before_ref[0, SUBLANES - 1:SUBLANES, :])
    next_row = jnp.where(chunk == nc - 1, 0.0, after_ref[0, 0:1, :])
    um = jnp.where(row == 0, prev_row, pltpu.roll(cur, 1, 0))
    up = jnp.where(row == q - 1, next_row, pltpu.roll(cur, q - 1, 0))
    return _silu(um * w[0:1, :] + cur * w[1:2, :] + up * w[2:3, :] + bias)


def _ssd_kernel(xf_ref, xf0_ref, xf1_ref, dtf_ref, xb_ref, xb0_ref, xb1_ref, dtb_ref, cw_ref, cb_ref,
                bias_ref, arow_ref, h0f_ref, h0b_ref, yf_ref, yb_ref, xs_ref, hlf_ref, hlb_ref,
                stf_ref, stb_ref, *, nc):
    c = pl.program_id(1)

    @pl.when(c == 0)
    def _():
        stf_ref[...] = h0f_ref[0]
        stb_ref[...] = h0b_ref[0]

    xf = _conv3_silu_chunk(xf_ref, xf0_ref, xf1_ref, cw_ref[...], cb_ref[...], c, nc)
    xs_ref[0] = xf[:, :SSD_INNER]
    yf_ref[0] = _ssd_chunk(xf, dtf_ref[0], bias_ref[...], arow_ref[...], stf_ref, 0)
    xb = _conv3_silu_chunk(xb_ref, xb0_ref, xb1_ref, cw_ref[...], cb_ref[...], nc - 1 - c, nc)
    yb_ref[0] = _ssd_chunk(xb, dtb_ref[0], bias_ref[...], arow_ref[...], stb_ref, 1)

    @pl.when(c == nc - 1)
    def _():
        hlf_ref[0] = stf_ref[...]
        hlb_ref[0] = stb_ref[...]


def _ssd_scan(xbc, dt, conv_w, conv_b, dtb_row, a_row, h0f, h0b):
    b, l, _ = xbc.shape
    nc = l // SSD_CHUNK
    per = SSD_CHUNK // SUBLANES
    last = l // SUBLANES - 1
    fwd = lambda i, c: (i, c, 0)
    bwd = lambda i, c: (i, nc - 1 - c, 0)
    before = lambda chunk_of: (lambda i, c: (i, jnp.maximum(chunk_of(c) * per - 1, 0), 0))
    after = lambda chunk_of: (lambda i, c: (i, jnp.minimum(chunk_of(c) * per + per, last), 0))
    f_of, b_of = (lambda c: c), (lambda c: nc - 1 - c)
    halo = lambda imap: pl.BlockSpec((1, SUBLANES, SSD_XBC), imap)
    st_shape = (SSD_GROUPS, SSD_STATE, SSD_INNER // SSD_GROUPS)
    st_spec = pl.BlockSpec((1,) + st_shape, lambda i, c: (i, 0, 0, 0))
    row = pl.BlockSpec((1, LANES), lambda i, c: (0, 0))
    return pl.pallas_call(
        functools.partial(_ssd_kernel, nc=nc),
        out_shape=[jax.ShapeDtypeStruct((b, l, SSD_INNER), F32)] * 3
        + [jax.ShapeDtypeStruct((b,) + st_shape, F32)] * 2,
        grid=(b, nc),
        in_specs=[pl.BlockSpec((1, SSD_CHUNK, SSD_XBC), fwd), halo(before(f_of)), halo(after(f_of)),
                  pl.BlockSpec((1, SSD_CHUNK, LANES), fwd),
                  pl.BlockSpec((1, SSD_CHUNK, SSD_XBC), bwd), halo(before(b_of)), halo(after(b_of)),
                  pl.BlockSpec((1, SSD_CHUNK, LANES), bwd),
                  pl.BlockSpec((3, SSD_XBC), lambda i, c: (0, 0)), pl.BlockSpec((1, SSD_XBC), lambda i, c: (0, 0)),
                  row, row, st_spec, st_spec],
        out_specs=[pl.BlockSpec((1, SSD_CHUNK, SSD_INNER), fwd), pl.BlockSpec((1, SSD_CHUNK, SSD_INNER), bwd),
                   pl.BlockSpec((1, SSD_CHUNK, SSD_INNER), fwd), st_spec, st_spec],
        scratch_shapes=[pltpu.VMEM(st_shape, F32), pltpu.VMEM(st_shape, F32)],
        compiler_params=_params("arbitrary", "arbitrary"),
        name="ssd_scan",
    )(xbc, xbc, xbc, dt, xbc, xbc, xbc, dt, conv_w, conv_b.reshape(1, SSD_XBC), dtb_row, a_row, h0f, h0b)


def _hy_filter_kernel(z_ref, w1_ref, b1_ref, w2_ref, b2_ref, w3_ref, fr_ref, win_ref, alt_ref,
                      fs_ref, fd_ref, nyq_ref):
    fr = fr_ref[...]
    h = jnp.sin(fr * (jnp.dot(z_ref[...], w1_ref[...], preferred_element_type=F32, precision=HI) + b1_ref[...]))
    h = jnp.sin(fr * (jnp.dot(h, w2_ref[...], preferred_element_type=F32, precision=HI) + b2_ref[...]))
    h = jnp.dot(h, w3_ref[...], preferred_element_type=F32, precision=HI)
    win = win_ref[...]
    row = lax.broadcasted_iota(jnp.int32, win.shape, 0)
    hf = h[:, :HY_WIDTH] * win
    hb = jnp.where(row == 0, 0.0, h[:, HY_WIDTH:] * win)
    nrm = lax.rsqrt(jnp.sum(hf * hf, axis=0, keepdims=True) + jnp.sum(hb * hb, axis=0, keepdims=True) + EPS)
    fs = (hf + hb) * nrm
    fs_ref[...] = fs
    fd_ref[...] = (hb - hf) * nrm
    nyq_ref[...] = jnp.sum(fs * alt_ref[...], axis=0, keepdims=True)


def _hy_filter(l, w1, b1, w2, b2, w3, freq):
    t = np.linspace(0.0, 1.0, l, dtype=np.float32)[:, None]
    w = (np.float32(2.0 * math.pi) * np.arange(l, dtype=np.float32)[:, None] / np.float32(l)).astype(np.float32)
    f = np.linspace(1e-4, HY_BANDS - 1, HY_BANDS, dtype=np.float32)[None, :]
    zf = jnp.asarray(f) * jnp.asarray(w)
    z = jnp.concatenate([jnp.asarray(t), jnp.cos(zf), -jnp.sin(zf)], axis=-1)
    z = jnp.pad(z, ((0, 0), (0, LANES - HY_EMB)))
    hid = w1.shape[1]
    w1p = jnp.pad(w1, ((0, LANES - HY_EMB), (0, LANES - hid)))
    w2p = jnp.pad(w2, ((0, LANES - hid), (0, LANES - hid)))
    w3p = jnp.pad(w3, ((0, LANES - hid), (0, 0)))
    pad_row = lambda v: jnp.pad(v.reshape(1, hid), ((0, 0), (0, LANES - hid)))
    max_decay = math.log(HY_DECAY_TARGET) / HY_SHORT_DECAY_PCT
    min_decay = math.log(HY_DECAY_TARGET) / HY_LONG_DECAY_PCT
    deltas = jnp.abs(jnp.linspace(min_decay, max_decay, HY_WIDTH, dtype=F32))
    win = jnp.exp(-jnp.asarray(t) * deltas)
    alt = jnp.asarray(np.where(np.arange(l) % 2 == 0, 1.0, -1.0).astype(np.float32)[:, None] * np.ones((1, HY_WIDTH), np.float32))
    return pl.pallas_call(
        _hy_filter_kernel,
        out_shape=[jax.ShapeDtypeStruct((l, HY_WIDTH), F32), jax.ShapeDtypeStruct((l, HY_WIDTH), F32),
                   jax.ShapeDtypeStruct((1, HY_WIDTH), F32)],
        compiler_params=pltpu.CompilerParams(vmem_limit_bytes=VMEM_LIMIT),
        name="hy_filter",
    )(z, w1p, pad_row(b1), w2p, pad_row(b2), w3p, pad_row(freq), win, alt)


def _mm_kernel(a_ref, b_ref, o_ref):
    o_ref[...] = jnp.dot(a_ref[...], b_ref[...], preferred_element_type=F32)


def _mm(a, b, tm):
    m, k = a.shape
    n = b.shape[1]
    return pl.pallas_call(
        _mm_kernel,
        out_shape=jax.ShapeDtypeStruct((m, n), F32),
        grid=(m // tm,),
        in_specs=[pl.BlockSpec((tm, k), lambda i: (i, 0)), pl.BlockSpec((k, n), lambda i: (0, 0))],
        out_specs=pl.BlockSpec((tm, n), lambda i: (i, 0)),
        compiler_params=_params("arbitrary"),
        name="mm",
    )(a, b)


def _dft_tables(l):
    n = 2 * l
    k = np.arange(l, dtype=np.int64)[:, None]
    s = np.arange(l, dtype=np.int64)[None, :]
    ang = (2.0 * np.pi / n) * ((k * s) % n).astype(np.float64)
    cos, sin = np.cos(ang), np.sin(ang)
    alt = np.where(np.arange(l) % 2 == 0, 1.0, -1.0)
    sin[0, :] = alt
    fwd = np.concatenate([cos, sin], axis=0)
    wgt = np.full((l, 1), 2.0)
    wgt[0, 0] = 1.0
    sin_i = -sin * wgt
    sin_i[0, :] = alt
    inv = np.concatenate([(cos * wgt).T, sin_i.T], axis=1) / n
    return fwd, inv


def _batch_rows(b):
    return 2 if b % 2 == 0 else 1


def _hy_conv_kernel(u_ref, cw_ref, cb_ref, ff_ref, fi_ref, ka_ref, kb_ref, kc_ref, bias_ref, o_ref,
                    ub_ref, uf_ref, x1_ref, acc_ref, *, nf):
    f = pl.program_id(1)
    w = HY_WIDTH
    rows = u_ref.shape[0]

    @pl.when(f == 0)
    def _():
        for r in range(rows):
            part = lambda k: _conv3(u_ref[r, :, k * w:(k + 1) * w].astype(F32), cw_ref[:, k * w:(k + 1) * w],
                                    cb_ref[:, k * w:(k + 1) * w])
            u = part(2) * part(0)
            cols = slice(r * w, (r + 1) * w)
            uf_ref[:, cols] = u
            ub_ref[:, cols] = u.astype(BF16)
            x1_ref[:, cols] = part(1)
        acc_ref[...] = jnp.zeros_like(acc_ref)

    uf = jnp.dot(ff_ref[0], ub_ref[...], preferred_element_type=F32)
    fb = uf.shape[0] // 2
    ur, us = uf[:fb], uf[fb:]
    ka, kb, kc = [jnp.concatenate([k_ref[...]] * rows, axis=1) for k_ref in (ka_ref, kb_ref, kc_ref)]
    p = jnp.concatenate([ur * ka + us * kb, ur * kb - us * kc], axis=0).astype(BF16)
    acc_ref[...] += jnp.dot(fi_ref[0], p, preferred_element_type=F32)

    @pl.when(f == nf - 1)
    def _():
        for r in range(rows):
            cols = slice(r * w, (r + 1) * w)
            o_ref[r] = x1_ref[:, cols] * (acc_ref[:, cols] + uf_ref[:, cols] * bias_ref[...])


def _hy_conv(proj, conv_w, conv_b, fs, fd, nyq, bias):
    b, l, _ = proj.shape
    rows = _batch_rows(b)
    fwd, inv = _dft_tables(l)
    fb = min(DFT_BIN_TILE, l)
    nf = l // fb
    ff = jnp.asarray(np.stack([np.concatenate([fwd[i * fb:(i + 1) * fb], fwd[l + i * fb:l + (i + 1) * fb]], axis=0)
                               for i in range(nf)]), dtype=BF16)
    fi = jnp.asarray(np.stack([np.concatenate([inv[:, i * fb:(i + 1) * fb], inv[:, l + i * fb:l + (i + 1) * fb]], axis=1)
                               for i in range(nf)]), dtype=BF16)
    fwd_b = jnp.asarray(fwd, dtype=BF16)
    tm = min(TOKEN_TILE, l)
    k_r = _mm(fwd_b[:l], fs.astype(BF16), tm)
    k_i = _mm(fwd_b[l:], fd.astype(BF16), tm)
    first = (jnp.arange(l) == 0)[:, None]
    ka = k_r
    kb = jnp.where(first, 0.0, k_i)
    kc = jnp.where(first, -nyq, k_r)
    return pl.pallas_call(
        functools.partial(_hy_conv_kernel, nf=nf),
        out_shape=jax.ShapeDtypeStruct((b, l, HY_WIDTH), F32),
        grid=(b // rows, nf),
        in_specs=[pl.BlockSpec((rows, l, 3 * HY_WIDTH), lambda i, f: (i, 0, 0)),
                  pl.BlockSpec((3, 3 * HY_WIDTH), lambda i, f: (0, 0)),
                  pl.BlockSpec((1, 3 * HY_WIDTH), lambda i, f: (0, 0)),
                  pl.BlockSpec((1, 2 * fb, l), lambda i, f: (f, 0, 0)),
                  pl.BlockSpec((1, l, 2 * fb), lambda i, f: (f, 0, 0)),
                  pl.BlockSpec((fb, HY_WIDTH), lambda i, f: (f, 0)),
                  pl.BlockSpec((fb, HY_WIDTH), lambda i, f: (f, 0)),
                  pl.BlockSpec((fb, HY_WIDTH), lambda i, f: (f, 0)),
                  pl.BlockSpec((1, HY_WIDTH), lambda i, f: (0, 0))],
        out_specs=pl.BlockSpec((rows, l, HY_WIDTH), lambda i, f: (i, 0, 0)),
        scratch_shapes=[pltpu.VMEM((l, rows * HY_WIDTH), BF16), pltpu.VMEM((l, rows * HY_WIDTH), F32),
                        pltpu.VMEM((l, rows * HY_WIDTH), F32), pltpu.VMEM((l, rows * HY_WIDTH), F32)],
        compiler_params=_params("arbitrary", "arbitrary"),
        name="hy_conv",
    )(proj, conv_w, conv_b.reshape(1, 3 * HY_WIDTH), ff, fi, ka, kb, kc, bias.reshape(1, HY_WIDTH))


def _fnet_kernel(x_ref, cc_ref, sc_ref, m_ref, o_ref, xx_ref):
    j = pl.program_id(1)
    rows, l, c = x_ref.shape

    @pl.when(j == 0)
    def _():
        for r in range(rows):
            xb = x_ref[r].astype(BF16)
            cols = slice(r * c, (r + 1) * c)
            xx_ref[:l, cols] = jnp.dot(xb, cc_ref[...], preferred_element_type=F32).astype(BF16)
            xx_ref[l:, cols] = jnp.dot(xb, sc_ref[...], preferred_element_type=F32).astype(BF16)

    y = jnp.dot(m_ref[...], xx_ref[...], preferred_element_type=F32)
    for r in range(rows):
        o_ref[r] = y[:, r * c:(r + 1) * c]


def _fnet(x):
    b, l, c = x.shape
    rows = _batch_rows(b)
    gd = FN_GROUP_DIM
    kc = np.arange(gd)[:, None] * np.arange(gd)[None, :]
    ang_c = 2.0 * np.pi * (kc % gd) / gd
    eye = np.eye(FN_GROUPS)
    cblk = jnp.asarray(np.kron(eye, np.cos(ang_c)), dtype=BF16)
    sblk = jnp.asarray(np.kron(eye, np.sin(ang_c)), dtype=BF16)
    kl = (np.arange(l, dtype=np.int64)[:, None] * np.arange(l, dtype=np.int64)[None, :]) % l
    ang_l = 2.0 * np.pi * kl / l
    scale = 1.0 / math.sqrt(l * gd)
    mat = jnp.asarray(np.concatenate([np.cos(ang_l), -np.sin(ang_l)], axis=1) * scale, dtype=BF16)
    tr = min(TOKEN_TILE, l)
    return pl.pallas_call(
        _fnet_kernel,
        out_shape=jax.ShapeDtypeStruct((b, l, c), F32),
        grid=(b // rows, l // tr),
        in_specs=[pl.BlockSpec((rows, l, c), lambda i, r: (i, 0, 0)),
                  pl.BlockSpec((c, c), lambda i, r: (0, 0)),
                  pl.BlockSpec((c, c), lambda i, r: (0, 0)),
                  pl.BlockSpec((tr, 2 * l), lambda i, r: (r, 0))],
        out_specs=pl.BlockSpec((rows, tr, c), lambda i, r: (i, r, 0)),
        scratch_shapes=[pltpu.VMEM((2 * l, rows * c), BF16)],
        compiler_params=_params("arbitrary", "arbitrary"),
        name="fnet",
    )(x, cblk, sblk, mat)


def _outproj_kernel(x_ref, ga_ref, yhy_ref, yf_ref, yb_ref, xs_ref, z_ref, yfn_ref, dsk_ref, ng_ref,
                    w1_ref, w2_ref, w3_ref, o_ref):
    y = yf_ref[0] + yb_ref[0] + dsk_ref[...] * xs_ref[0]
    y = y * _silu(z_ref[0])
    gw = SSD_INNER // SSD_GROUPS
    parts = []
    for g in range(SSD_GROUPS):
        yg = y[:, g * gw:(g + 1) * gw]
        parts.append(yg * lax.rsqrt(jnp.mean(yg * yg, axis=-1, keepdims=True) + EPS))
    yn = jnp.concatenate(parts, axis=1) * ng_ref[...]
    out = jnp.dot(yhy_ref[0].astype(BF16), w1_ref[...], preferred_element_type=F32)
    out += jnp.dot(yn.astype(BF16), w2_ref[...], preferred_element_type=F32)
    out += jnp.dot(yfn_ref[0].astype(BF16), w3_ref[...], preferred_element_type=F32)
    o_ref[0] = x_ref[0] + ga_ref[0] * out


def _out_proj(x, ga, y_hy, y_f, y_b, xbc, z, y_fn, dsk_row, ng_row, w1, w2, w3):
    b, l, d = x.shape
    tm = min(TOKEN_TILE, l)
    tok = lambda n: pl.BlockSpec((1, tm, n), lambda i, j: (i, j, 0))
    const = lambda a: pl.BlockSpec(a.shape, lambda i, j: (0,) * a.ndim)
    return pl.pallas_call(
        _outproj_kernel,
        out_shape=jax.ShapeDtypeStruct((b, l, d), F32),
        grid=(b, l // tm),
        in_specs=[tok(d), pl.BlockSpec((1, 1, d), lambda i, j: (i, 0, 0)), tok(HY_WIDTH), tok(SSD_INNER),
                  tok(SSD_INNER), tok(SSD_INNER), tok(SSD_INNER), tok(FN_WIDTH), const(dsk_row), const(ng_row),
                  const(w1), const(w2), const(w3)],
        out_specs=tok(d),
        compiler_params=_params("arbitrary", "arbitrary"),
        name="out_proj",
    )(x, ga, y_hy, y_f, y_b, xbc, z, y_fn, dsk_row, ng_row, w1, w2, w3)


def _peer_scores(x_ref, g_ref, sh_ref, sc_ref, wq_ref, k1_ref, k2_ref, ht_ref, s1_ref, s2_ref):
    hm = _normmod(x_ref[...], g_ref[...], sh_ref[0], sc_ref[0])
    ht_ref[...] = hm.T.astype(F8)
    qb = jnp.dot(hm.astype(BF16), wq_ref[...], preferred_element_type=F32).astype(BF16)
    nt = (((1,), (1,)), ((), ()))
    for h in range(PEER_HEADS):
        q1 = qb[:, (2 * h) * PEER_HALF:(2 * h + 1) * PEER_HALF]
        q2 = qb[:, (2 * h + 1) * PEER_HALF:(2 * h + 2) * PEER_HALF]
        s1_ref[h] = lax.dot_general(k1_ref[...], q1, nt, preferred_element_type=F32)
        s2_ref[h] = lax.dot_general(k2_ref[...], q2, nt, preferred_element_type=F32)


def _sorting_network(n):
    pairs = []

    def merge(lo, m, r):
        step = 2 * r
        if step < m:
            merge(lo, m, step)
            merge(lo + r, m, step)
            for i in range(lo + r, lo + m - r, step):
                pairs.append((i, i + r))
        else:
            pairs.append((lo, lo + r))

    def sort(lo, m):
        if m > 1:
            half = m // 2
            sort(lo, half)
            sort(lo + half, half)
            merge(lo, m, 1)

    sort(0, n)
    return tuple(pairs)


NEG = -3.0e38
N_TOP = PEER_TOPK + 1


def _top_sorted(rows, n_out):
    rows = list(rows)
    for i, j in _sorting_network(len(rows)):
        rows[i], rows[j] = jnp.maximum(rows[i], rows[j]), jnp.minimum(rows[i], rows[j])
    sub = lax.broadcasted_iota(jnp.int32, rows[0].shape, 0)
    out = []
    for r in range(n_out):
        m = jnp.max(rows[0], axis=0, keepdims=True)
        out.append(m)
        first = jnp.min(jnp.where(rows[0] == m, sub, SUBLANES), axis=0, keepdims=True)
        hit = sub == first
        keep = min(len(rows), n_out - r - 1)
        rows = [jnp.where(hit, rows[k + 1] if k + 1 < len(rows) else NEG, rows[k]) for k in range(keep)]
    return out


def _count_above(vals, x, strict):
    assert len(vals) == 16
    test = (lambda v: v > x) if strict else (lambda v: v >= x)
    c8 = test(vals[7])
    c4 = test(jnp.where(c8, vals[11], vals[3]))
    c2 = test(jnp.where(c8, jnp.where(c4, vals[13], vals[9]), jnp.where(c4, vals[5], vals[1])))
    upper = jnp.where(c4, jnp.where(c2, vals[14], vals[12]), jnp.where(c2, vals[10], vals[8]))
    lower = jnp.where(c4, jnp.where(c2, vals[6], vals[4]), jnp.where(c2, vals[2], vals[0]))
    c1 = test(jnp.where(c8, upper, lower))
    r = (jnp.where(c8, 8.0, 0.0) + jnp.where(c4, 4.0, 0.0)) + (jnp.where(c2, 2.0, 0.0) + jnp.where(c1, 1.0, 0.0))
    return jnp.where(test(vals[15]), 16.0, r)


def _dup_bf16(x):
    hi = pltpu.bitcast(x.astype(BF16).astype(F32), jnp.uint32)
    return hi | (hi >> 16)


def _peer_stats_kernel(s1_ref, s2_ref, r2_ref, a2_ref, ns_ref, cc_ref):
    nrow = N_KEYS // SUBLANES
    t = s1_ref.shape[2]
    sub = lax.broadcasted_iota(jnp.int32, (SUBLANES, t), 0)
    for h in range(PEER_HEADS):
        s1_rows = [s1_ref[h, SUBLANES * k:SUBLANES * (k + 1), :] for k in range(nrow)]
        s2_rows = [s2_ref[h, SUBLANES * k:SUBLANES * (k + 1), :] for k in range(nrow)]
        v1 = _top_sorted(s1_rows, N_TOP)
        v2 = _top_sorted(s2_rows, N_TOP)
        cands = [v1[i] + v2[j] for i in range(N_TOP) for j in range(N_TOP) if (i + 1) * (j + 1) <= N_TOP]
        packed = []
        for k in range(0, len(cands), SUBLANES):
            blk = jnp.full((SUBLANES, t), NEG, F32)
            for s, cv in enumerate(cands[k:k + SUBLANES]):
                blk = jnp.where(sub == s, cv, blk)
            packed.append(blk)
        ids = [sub + SUBLANES * k for k in range(len(packed))]
        big = SUBLANES * len(packed)
        top = []
        for r in range(N_TOP):
            m = packed[0]
            for blk in packed[1:]:
                m = jnp.maximum(m, blk)
            m = jnp.max(m, axis=0, keepdims=True)
            top.append(m)
            sel = jnp.where(packed[0] == m, ids[0], big)
            for blk, idk in zip(packed[1:], ids[1:]):
                sel = jnp.minimum(sel, jnp.where(blk == m, idk, big))
            sel = jnp.min(sel, axis=0, keepdims=True)
            packed = [jnp.where(idk == sel, NEG, blk) for blk, idk in zip(packed, ids)]
        mx = top[0]
        z = jnp.zeros((1, t), F32)
        for r in range(PEER_TOPK):
            z = z + jnp.exp(top[r] - mx)
        tau = 0.5 * (top[PEER_TOPK - 1] + top[PEER_TOPK])
        cscale = 0.5 / z
        for k in range(0, nrow, 2):
            pair = slice(SUBLANES * k, SUBLANES * (k + 2))
            r2 = [_count_above(v2[:PEER_TOPK], s2_rows[k + q], True) for q in range(2)]
            r2_ref[h, pair, :] = jnp.concatenate(r2, axis=0).astype(BF16)
            a2 = [jnp.exp(s2_rows[k + q] - v2[0]) for q in range(2)]
            a2_ref[h, pair, :] = jnp.concatenate(a2, axis=0).astype(BF16)
        for k in range(nrow):
            blk = slice(SUBLANES * k, SUBLANES * (k + 1))
            ns_ref[h, blk, :] = _dup_bf16(_count_above(v2[:PEER_TOPK], tau - s1_rows[k], False))
            cc_ref[h, blk, :] = _dup_bf16(jnp.exp(s1_rows[k] - v1[0]) * cscale)


def _peer_select_kernel(x_ref, g_ref, sh_ref, sc_ref, wq_ref, k1_ref, k2_ref,
                        ht_ref, r2_ref, a2_ref, ns_ref, cc_ref, s1_ref, s2_ref):
    _peer_scores(x_ref, g_ref, sh_ref, sc_ref, wq_ref, k1_ref, k2_ref, ht_ref, s1_ref, s2_ref)
    _peer_stats_kernel(s1_ref, s2_ref, r2_ref, a2_ref, ns_ref, cc_ref)


def _peer_select(x2, g, sh, sc, wq, k1, k2, l):
    n, d = x2.shape
    tm = min(SELECT_TOKENS, l)
    per = l // tm
    shp = (PEER_HEADS, N_KEYS, n)
    blk = pl.BlockSpec((PEER_HEADS, N_KEYS, tm), lambda i: (0, 0, i))
    return pl.pallas_call(
        _peer_select_kernel,
        out_shape=[jax.ShapeDtypeStruct((d, n), F8),
                   jax.ShapeDtypeStruct(shp, BF16), jax.ShapeDtypeStruct(shp, BF16),
                   jax.ShapeDtypeStruct(shp, jnp.uint32), jax.ShapeDtypeStruct(shp, jnp.uint32)],
        grid=(n // tm,),
        in_specs=[pl.BlockSpec((tm, d), lambda i: (i, 0)),
                  pl.BlockSpec((1, d), lambda i: (0, 0)),
                  pl.BlockSpec((1, 1, d), lambda i: (i // per, 0, 0)),
                  pl.BlockSpec((1, 1, d), lambda i: (i // per, 0, 0)),
                  pl.BlockSpec(wq.shape, lambda i: (0, 0)),
                  pl.BlockSpec(k1.shape, lambda i: (0, 0)),
                  pl.BlockSpec(k2.shape, lambda i: (0, 0))],
        out_specs=[pl.BlockSpec((d, tm), lambda i: (0, i)), blk, blk, blk, blk],
        scratch_shapes=[pltpu.VMEM((PEER_HEADS, N_KEYS, tm), F32), pltpu.VMEM((PEER_HEADS, N_KEYS, tm), F32)],
        compiler_params=_params("arbitrary"),
        name="peer_select",
    )(x2, g, sh, sc, wq, k1, k2)


E_TILE = 1024
E_BLOCK = 2 * E_TILE
PEER_SCALE = 32.0


def _gelu_tanh_x2_scaled(s):
    c0 = math.sqrt(2.0 / math.pi) / PEER_SCALE
    c1 = math.sqrt(2.0 / math.pi) * 0.044715 / PEER_SCALE ** 3
    return s + s * jnp.tanh(s * (c0 + c1 * (s * s)))


ROWS_PER_TILE = E_TILE // N_KEYS


def _rows_bf16(word_row):
    words = word_row | jnp.zeros((SUBLANES, word_row.shape[1]), jnp.uint32)
    tile = pltpu.bitcast(words, BF16)
    return jnp.concatenate([tile] * (N_KEYS // tile.shape[0]), axis=0)


def _peer_mix_kernel(ht_ref, u_ref, vt_ref, r2_ref, a2_ref, nsp_ref, ccp_ref, nsc_ref, ccc_ref, x_ref, ga_ref,
                     gf_ref, o_ref, st0_ref, st1_ref, act0_ref, act1_ref, acc_ref, *, nblk, final_norm):
    m = pl.program_id(1)

    def scores(half, st_ref):
        st_ref[...] = jnp.dot(u_ref[half * E_TILE:(half + 1) * E_TILE, :], ht_ref[...],
                              preferred_element_type=F32).astype(BF16)

    def gate(st_ref, act_ref, ns_ref, cc_ref):
        for e in range(ROWS_PER_TILE):
            g = None
            for h in range(PEER_HEADS):
                nb = _rows_bf16(ns_ref[h, e:e + 1, :])
                cb = _rows_bf16(cc_ref[h, e:e + 1, :])
                a2 = a2_ref[h]
                term = jnp.where(r2_ref[h] < nb, a2, jnp.zeros_like(a2)) * cb
                g = term if g is None else g + term
            rows = slice(e * N_KEYS, (e + 1) * N_KEYS)
            act_ref[rows, :] = (_gelu_tanh_x2_scaled(st_ref[rows, :]) * g).astype(F8)

    def accumulate(half, act_ref):
        acc_ref[...] += jnp.dot(vt_ref[:, half * E_TILE:(half + 1) * E_TILE], act_ref[...],
                                preferred_element_type=F32)

    @pl.when(m == 0)
    def _():
        acc_ref[...] = jnp.zeros_like(acc_ref)
        scores(0, st0_ref)
        gate(st0_ref, act0_ref, nsc_ref, ccc_ref)
        scores(1, st1_ref)

    @pl.when((m > 0) & (m < nblk))
    def _():
        accumulate(0, act0_ref)
        gate(st1_ref, act1_ref, nsp_ref, ccp_ref)
        scores(0, st0_ref)
        accumulate(1, act1_ref)
        gate(st0_ref, act0_ref, nsc_ref, ccc_ref)
        scores(1, st1_ref)

    @pl.when(m == nblk)
    def _():
        accumulate(0, act0_ref)
        gate(st1_ref, act1_ref, nsp_ref, ccp_ref)
        accumulate(1, act1_ref)
        y = x_ref[...] + ga_ref[0] * (acc_ref[...].T * (1.0 / PEER_SCALE))
        if final_norm:
            y = y * lax.rsqrt(jnp.mean(y * y, axis=-1, keepdims=True) + EPS) * gf_ref[...]
        o_ref[...] = y


def _peer_mix(ht, u, vt, r2, a2, ns, cc, x2, ga, l, g_final=None):
    d, n = ht.shape
    n_exp = u.shape[0]
    t = min(TOKEN_TILE, l)
    per = l // t
    nblk = n_exp // E_BLOCK
    n_tiles = n_exp // E_TILE
    sblk = pl.BlockSpec((PEER_HEADS, N_KEYS, t), lambda i, m: (0, 0, i))
    prev = pl.BlockSpec((PEER_HEADS, ROWS_PER_TILE, t), lambda i, m: (0, jnp.maximum(2 * m - 1, 0), i))
    cur = pl.BlockSpec((PEER_HEADS, ROWS_PER_TILE, t), lambda i, m: (0, jnp.minimum(2 * m, n_tiles - 1), i))
    return pl.pallas_call(
        functools.partial(_peer_mix_kernel, nblk=nblk, final_norm=g_final is not None),
        out_shape=jax.ShapeDtypeStruct((n, d), F32),
        grid=(n // t, nblk + 1),
        in_specs=[pl.BlockSpec((d, t), lambda i, m: (0, i)),
                  pl.BlockSpec((E_BLOCK, d), lambda i, m: (jnp.minimum(m, nblk - 1), 0)),
                  pl.BlockSpec((d, E_BLOCK), lambda i, m: (0, jnp.maximum(m - 1, 0))),
                  sblk, sblk, prev, prev, cur, cur,
                  pl.BlockSpec((t, d), lambda i, m: (i, 0)),
                  pl.BlockSpec((1, 1, d), lambda i, m: (i // per, 0, 0)),
                  pl.BlockSpec((1, d), lambda i, m: (0, 0))],
        out_specs=pl.BlockSpec((t, d), lambda i, m: (i, 0)),
        scratch_shapes=[pltpu.VMEM((E_TILE, t), BF16), pltpu.VMEM((E_TILE, t), BF16),
                        pltpu.VMEM((E_TILE, t), F8), pltpu.VMEM((E_TILE, t), F8),
                        pltpu.VMEM((d, t), F32)],
        compiler_params=_params("arbitrary", "arbitrary"),
        name="peer_mix",
    )(ht, u, vt, r2, a2, ns, cc, ns, cc, x2, ga, jnp.ones((1, d), F32) if g_final is None else g_final)


def _pad_lanes(a):
    return jnp.pad(a, ((0, 0), (0, LANES - a.shape[1])))


def _token_mix_inputs(p, proj_hy, l):
    fs, fd, nyq = _hy_filter(l, p['hf_w1'], p['hf_b1'], p['hf_w2'], p['hf_b2'], p['hf_w3'], p['hf_freq'])
    return _hy_conv(proj_hy, p['hy_conv_w'], p['hy_conv_b'], fs, fd, nyq, p['hy_bias'])


def _peer(x, p, sh, sc, ga, g_final=None):
    b, l, d = x.shape
    x2 = x.reshape(b * l, d)
    ht, r2, a2, ns, cc = _peer_select(x2, p['g_norm2'], sh, sc, p['wq'], p['k1'], p['k2'], l)
    out = _peer_mix(ht, p['u'], p['vt'], r2, a2, ns, cc, x2, ga, l, g_final)
    return out.reshape(b, l, d)


def _layer(xl, xc, mod_l, mod_c, p, ctx_out, g_final=None):
    b = xl.shape[0]
    d = xl.shape[2]
    sh1, sc1, ga1, sh2, sc2, ga2 = [m.reshape(b, 1, d) for m in jnp.split(mod_l, 6, axis=-1)]
    csh1, csc1, cga1, csh2, csc2, cga2 = [jnp.broadcast_to(m.reshape(1, 1, d), (b, 1, d))
                                          for m in jnp.split(mod_c, 6, axis=-1)]
    w_in = p['w_in']
    w_hy, w_z, w_xbc, w_fn = (w_in[:, OFF_HY:OFF_Z], w_in[:, OFF_Z:OFF_XBC], w_in[:, OFF_XBC:OFF_DT],
                              w_in[:, OFF_FN:D_IN_PROJ])
    w_dt = _pad_lanes(w_in[:, OFF_DT:OFF_FN])
    g1 = p['g_norm1']
    all_w, all_dt = [w_hy, w_z, w_xbc, w_dt, w_fn], [BF16, F32, F32, F32, BF16]
    pl_hy, pl_z, pl_xbc, pl_dt, pl_fn = _in_proj(xl, g1, sh1, sc1, all_w, all_dt)
    if ctx_out:
        pc_hy, pc_z, pc_xbc, pc_dt, pc_fn = _in_proj(xc, g1, csh1, csc1, all_w, all_dt)
    else:
        pc_xbc, pc_dt = _in_proj(xc, g1, csh1, csc1, [w_xbc, w_dt], [F32, F32])

    dtb, arow, cw, cb = p['dtb_row'], p['a_row'], p['ssd_conv_w'], p['ssd_conv_b']
    h0 = jnp.zeros((b, SSD_GROUPS, SSD_STATE, SSD_INNER // SSD_GROUPS), F32)
    yc_f, yc_b, xs_c, hc_f, hc_b = _ssd_scan(pc_xbc, pc_dt, cw, cb, dtb, arow, h0, h0)
    yl_f, yl_b, xs_l, _, _ = _ssd_scan(pl_xbc, pl_dt, cw, cb, dtb, arow, hc_f, hc_b)

    def token_mix(x, ga, proj_hy, y_f, y_b, xs, z, proj_fn):
        y_hy = _token_mix_inputs(p, proj_hy, x.shape[1])
        y_fn = _fnet(proj_fn)
        return _out_proj(x, ga, y_hy, y_f, y_b, xs, z, y_fn, p['dsk_row'], p['ng_row'], p['wo1'], p['wo2'], p['wo3'])

    xl = token_mix(xl, ga1, pl_hy, yl_f, yl_b, xs_l, pl_z, pl_fn)
    xl = _peer(xl, p, sh2, sc2, ga2, g_final)
    if ctx_out:
        xc = token_mix(xc, cga1, pc_hy, yc_f, yc_b, xs_c, pc_z, pc_fn)
        xc = _peer(xc, p, csh2, csc2, cga2)
    return xl, xc


def kernel(x, c, ctx, c_ctx, w_ada, b_ada, g_norm1, g_norm2, w_in, hy_conv_w, hy_conv_b, hf_w1, hf_b1, hf_w2, hf_b2, hf_w3, hf_freq, hy_bias, ssd_conv_w, ssd_conv_b, ssd_dt_bias, ssd_a_log, ssd_d, ssd_norm_g, w_out, peer_wq, peer_k1, peer_k2, peer_u, peer_v, g_final):
    depth = w_ada.shape[0]
    b, l, d = x.shape
    rows = -(-(b + 1) // SUBLANES) * SUBLANES
    cc = jnp.concatenate([c, c_ctx[None, :], jnp.zeros((rows - b - 1, d), F32)], axis=0)
    mods = _ada_mod(cc, w_ada, b_ada)
    xl, xc = x, ctx
    for i in range(depth):
        wo = w_out[i].astype(BF16)
        p = {
            'g_norm1': g_norm1[i].reshape(1, d), 'g_norm2': g_norm2[i].reshape(1, d),
            'w_in': w_in[i].astype(BF16),
            'hy_conv_w': hy_conv_w[i], 'hy_conv_b': hy_conv_b[i],
            'hf_w1': hf_w1[i], 'hf_b1': hf_b1[i], 'hf_w2': hf_w2[i], 'hf_b2': hf_b2[i], 'hf_w3': hf_w3[i],
            'hf_freq': hf_freq[i], 'hy_bias': hy_bias[i],
            'ssd_conv_w': ssd_conv_w[i], 'ssd_conv_b': ssd_conv_b[i],
            'dtb_row': _pad_lanes(ssd_dt_bias[i].reshape(1, 2 * SSD_HEADS)),
            'a_row': _pad_lanes(-jnp.exp(ssd_a_log[i].astype(F32)).reshape(1, 2 * SSD_HEADS)),
            'dsk_row': jnp.repeat(ssd_d[i].astype(F32), SSD_HEAD_DIM).reshape(1, SSD_INNER),
            'ng_row': ssd_norm_g[i].reshape(1, SSD_INNER),
            'wo1': wo[:HY_WIDTH], 'wo2': wo[HY_WIDTH:HY_WIDTH + SSD_INNER], 'wo3': wo[HY_WIDTH + SSD_INNER:],
            'wq': peer_wq[i].astype(BF16), 'k1': peer_k1[i].astype(BF16), 'k2': peer_k2[i].astype(BF16),
            'u': (peer_u[i] * PEER_SCALE).astype(F8), 'vt': peer_v[i].astype(F8).T,
        }
        last = i == depth - 1
        xl, xc = _layer(xl, xc, mods[i, :b], mods[i, b], p, not last, g_final.reshape(1, d) if last else None)
    return xl
```

```python
import functools
import math

import jax
import jax.numpy as jnp
import numpy as np
from jax import lax
from jax.experimental import pallas as pl
from jax.experimental.pallas import tpu as pltpu

F32 = jnp.float32
BF16 = jnp.bfloat16
F8 = jnp.float8_e4m3fn
HI = lax.Precision.HIGHEST

EPS = 1e-6
LANES = 128
SUBLANES = 8
VMEM_LIMIT = 56 * 1024 * 1024

ADA_COL_TILE = 1536
TOKEN_TILE = 512
SELECT_TOKENS = 256
DFT_BIN_TILE = 256

HY_WIDTH = 256
HY_EMB = 33
HY_BANDS = (HY_EMB - 1) // 2
HY_DECAY_TARGET = 1e-2
HY_SHORT_DECAY_PCT = 0.3
HY_LONG_DECAY_PCT = 1.5
SSD_HEADS = 8
SSD_HEAD_DIM = 64
SSD_INNER = SSD_HEADS * SSD_HEAD_DIM
SSD_GROUPS = 2
SSD_STATE = 128
SSD_CHUNK = 128
SSD_XBC = SSD_INNER + 2 * SSD_GROUPS * SSD_STATE
FN_WIDTH = 256
FN_GROUPS = 4
FN_GROUP_DIM = FN_WIDTH // FN_GROUPS
PEER_HEADS = 8
PEER_TOPK = 16
N_KEYS = 128
PEER_HALF = 128
OFF_HY = 0
OFF_Z = OFF_HY + 3 * HY_WIDTH
OFF_XBC = OFF_Z + SSD_INNER
OFF_DT = OFF_XBC + SSD_XBC
OFF_FN = OFF_DT + 2 * SSD_HEADS
D_IN_PROJ = OFF_FN + FN_WIDTH


def _params(*sem):
    return pltpu.CompilerParams(dimension_semantics=sem, vmem_limit_bytes=VMEM_LIMIT)


def _silu(x):
    return x * jax.nn.sigmoid(x)


def _softplus(x):
    return jnp.maximum(x, 0.0) + jnp.log1p(jnp.exp(-jnp.abs(x)))


def _gelu_tanh(x):
    return 0.5 * x * (1.0 + jnp.tanh(math.sqrt(2.0 / math.pi) * (x + 0.044715 * (x * x * x))))


def _ada_kernel(c_ref, w_ref, b_ref, o_ref):
    s = _silu(c_ref[...])
    o_ref[0] = jnp.dot(s, w_ref[0], preferred_element_type=F32, precision=HI) + b_ref[0]


def _ada_mod(cc, w_ada, b_ada):
    depth, d, n = w_ada.shape
    r = cc.shape[0]
    tn = ADA_COL_TILE
    return pl.pallas_call(
        _ada_kernel,
        out_shape=jax.ShapeDtypeStruct((depth, r, n), F32),
        grid=(depth, n // tn),
        in_specs=[pl.BlockSpec((r, d), lambda l, j: (0, 0)),
                  pl.BlockSpec((1, d, tn), lambda l, j: (l, 0, j)),
                  pl.BlockSpec((1, 1, tn), lambda l, j: (l, 0, j))],
        out_specs=pl.BlockSpec((1, r, tn), lambda l, j: (l, 0, j)),
        compiler_params=_params("arbitrary", "arbitrary"),
        name="ada_mod",
    )(cc, w_ada, b_ada.reshape(depth, 1, n))


def _normmod(x, g, sh, sc):
    y = x * lax.rsqrt(jnp.mean(x * x, axis=-1, keepdims=True) + EPS) * g
    return y * (1.0 + sc) + sh


def _inproj_kernel(x_ref, g_ref, sh_ref, sc_ref, *rest, n_w):
    w_refs, o_refs = rest[:n_w], rest[n_w:]
    hb = _normmod(x_ref[0], g_ref[...], sh_ref[0], sc_ref[0]).astype(BF16)
    for w_ref, o_ref in zip(w_refs, o_refs):
        o_ref[0] = jnp.dot(hb, w_ref[...], preferred_element_type=F32).astype(o_ref.dtype)


def _in_proj(x, g, sh, sc, ws, dtypes):
    b, l, d = x.shape
    tm = min(TOKEN_TILE, l)
    return pl.pallas_call(
        functools.partial(_inproj_kernel, n_w=len(ws)),
        out_shape=[jax.ShapeDtypeStruct((b, l, w.shape[1]), dt) for w, dt in zip(ws, dtypes)],
        grid=(b, l // tm),
        in_specs=[pl.BlockSpec((1, tm, d), lambda i, j: (i, j, 0)),
                  pl.BlockSpec((1, d), lambda i, j: (0, 0)),
                  pl.BlockSpec((1, 1, d), lambda i, j: (i, 0, 0)),
                  pl.BlockSpec((1, 1, d), lambda i, j: (i, 0, 0))]
        + [pl.BlockSpec(w.shape, lambda i, j: (0, 0)) for w in ws],
        out_specs=[pl.BlockSpec((1, tm, w.shape[1]), lambda i, j: (i, j, 0)) for w in ws],
        compiler_params=_params("arbitrary", "arbitrary"),
        name="in_proj",
    )(x, g, sh, sc, *ws)


def _conv3(u, w, bias):
    l = u.shape[0]
    row = lax.broadcasted_iota(jnp.int32, u.shape, 0)
    um = jnp.where(row == 0, 0.0, pltpu.roll(u, 1, 0))
    up = jnp.where(row == l - 1, 0.0, pltpu.roll(u, l - 1, 0))
    return um * w[0:1, :] + u * w[1:2, :] + up * w[2:3, :] + bias


def _split3(x):
    hi = x.astype(BF16)
    r1 = x - hi.astype(F32)
    mid = r1.astype(BF16)
    lo = (r1 - mid.astype(F32)).astype(BF16)
    return hi, mid, lo


def _ssd_chunk(xbc, dt_raw, dtb, arow, st_ref, direction):
    q = SSD_CHUNK
    x = xbc[:, :SSD_INNER]
    gn = SSD_GROUPS * SSD_STATE
    bm = xbc[:, SSD_INNER:SSD_INNER + gn].astype(BF16)
    cm = xbc[:, SSD_INNER + gn:].astype(BF16)
    dtp = _softplus(dt_raw + dtb)
    da = dtp * arow
    ii = lax.broadcasted_iota(jnp.int32, (q, q), 0)
    jj = lax.broadcasted_iota(jnp.int32, (q, q), 1)
    tri = (ii >= jj) if direction == 0 else (ii <= jj)
    tri_b = tri.astype(BF16)
    cs = jnp.dot(jnp.concatenate([tri_b] * 3, axis=1), jnp.concatenate(_split3(da), axis=0),
                 preferred_element_type=F32)
    cs_t = cs.T
    lane = lax.broadcasted_iota(jnp.int32, (LANES, SSD_INNER), 0)
    chan = lax.broadcasted_iota(jnp.int32, (LANES, SSD_INNER), 1)
    head_of = lax.shift_right_logical(chan, int(math.log2(SSD_HEAD_DIM)))
    expand = (lane == direction * SSD_HEADS + head_of).astype(BF16)
    expand3 = jnp.concatenate([expand] * 3, axis=0)
    both = jnp.concatenate([jnp.concatenate(_split3(cs), axis=1), jnp.concatenate(_split3(dtp), axis=1)], axis=0)
    wide = jnp.dot(both, expand3, preferred_element_type=F32)
    acum, dtf = wide[:q], wide[q:]
    tot = acum[q - 1:q, :] if direction == 0 else acum[0:1, :]
    xdt = x * dtf
    ea = jnp.exp(acum)
    xdec = (jnp.exp(tot - acum) * xdt).astype(BF16)
    cd = jnp.exp(tot)
    xdt_b = xdt.astype(BF16)
    hg = SSD_HEADS // SSD_GROUPS
    gw = hg * SSD_HEAD_DIM
    outs = []
    for g in range(SSD_GROUPS):
        bg = bm[:, g * SSD_STATE:(g + 1) * SSD_STATE]
        cg = cm[:, g * SSD_STATE:(g + 1) * SSD_STATE]
        cb = lax.dot_general(cg, bg, (((1,), (1,)), ((), ())), preferred_element_type=F32)
        st = st_ref[g]
        y_g = jnp.dot(cg, st.astype(BF16), preferred_element_type=F32) * ea[:, g * gw:(g + 1) * gw]
        xg = xdt_b[:, g * gw:(g + 1) * gw]
        head_g = lax.shift_right_logical(lax.broadcasted_iota(jnp.int32, xg.shape, 1), int(math.log2(SSD_HEAD_DIM)))
        for k in range(hg):
            ln = direction * SSD_HEADS + g * hg + k
            seg = cs[:, ln:ln + 1] - cs_t[ln:ln + 1, :]
            lmat = jnp.where(tri, jnp.exp(jnp.where(tri, seg, 0.0)), 0.0)
            m = (cb * lmat).astype(BF16)
            y_g = y_g + jnp.dot(m, jnp.where(head_g == k, xg, jnp.zeros_like(xg)), preferred_element_type=F32)
        outs.append(y_g)
        sg = lax.dot_general(bg, xdec[:, g * gw:(g + 1) * gw], (((0,), (0,)), ((), ())),
                             preferred_element_type=F32)
        st_ref[g] = st * cd[:, g * gw:(g + 1) * gw] + sg
    return jnp.concatenate(outs, axis=1)


def _conv3_silu_chunk(cur_ref, before_ref, after_ref, w, bias, chunk, nc):
    cur = cur_ref[0]
    q = cur.shape[0]
    row = lax.broadcasted_iota(jnp.int32, cur.shape, 0)
    prev_row = jnp.where(chunk == 0, 0.0, before_ref[0, SUBLANES - 1:SUBLANES, :])
    next_row = jnp.where(chunk == nc - 1, 0.0, after_ref[0, 0:1, :])
    um = jnp.where(row == 0, prev_row, pltpu.roll(cur, 1, 0))
    up = jnp.where(row == q - 1, next_row, pltpu.roll(cur, q - 1, 0))
    return _silu(um * w[0:1, :] + cur * w[1:2, :] + up * w[2:3, :] + bias)


def _ssd_kernel(xf_ref, xf0_ref, xf1_ref, dtf_ref, xb_ref, xb0_ref, xb1_ref, dtb_ref, cw_ref, cb_ref,
                bias_ref, arow_ref, h0f_ref, h0b_ref, yf_ref, yb_ref, xs_ref, hlf_ref, hlb_ref,
                stf_ref, stb_ref, *, nc):
    c = pl.program_id(1)

    @pl.when(c == 0)
    def _():
        stf_ref[...] = h0f_ref[0]
        stb_ref[...] = h0b_ref[0]

    xf = _conv3_silu_chunk(xf_ref, xf0_ref, xf1_ref, cw_ref[...], cb_ref[...], c, nc)
    xs_ref[0] = xf[:, :SSD_INNER]
    yf_ref[0] = _ssd_chunk(xf, dtf_ref[0], bias_ref[...], arow_ref[...], stf_ref, 0)
    xb = _conv3_silu_chunk(xb_ref, xb0_ref, xb1_ref, cw_ref[...], cb_ref[...], nc - 1 - c, nc)
    yb_ref[0] = _ssd_chunk(xb, dtb_ref[0], bias_ref[...], arow_ref[...], stb_ref, 1)

    @pl.when(c == nc - 1)
    def _():
        hlf_ref[0] = stf_ref[...]
        hlb_ref[0] = stb_ref[...]


def _ssd_scan(xbc, dt, conv_w, conv_b, dtb_row, a_row, h0f, h0b):
    b, l, _ = xbc.shape
    nc = l // SSD_CHUNK
    per = SSD_CHUNK // SUBLANES
    last = l // SUBLANES - 1
    fwd = lambda i, c: (i, c, 0)
    bwd = lambda i, c: (i, nc - 1 - c, 0)
    before = lambda chunk_of: (lambda i, c: (i, jnp.maximum(chunk_of(c) * per - 1, 0), 0))
    after = lambda chunk_of: (lambda i, c: (i, jnp.minimum(chunk_of(c) * per + per, last), 0))
    f_of, b_of = (lambda c: c), (lambda c: nc - 1 - c)
    halo = lambda imap: pl.BlockSpec((1, SUBLANES, SSD_XBC), imap)
    st_shape = (SSD_GROUPS, SSD_STATE, SSD_INNER // SSD_GROUPS)
    st_spec = pl.BlockSpec((1,) + st_shape, lambda i, c: (i, 0, 0, 0))
    row = pl.BlockSpec((1, LANES), lambda i, c: (0, 0))
    return pl.pallas_call(
        functools.partial(_ssd_kernel, nc=nc),
        out_shape=[jax.ShapeDtypeStruct((b, l, SSD_INNER), F32)] * 3
        + [jax.ShapeDtypeStruct((b,) + st_shape, F32)] * 2,
        grid=(b, nc),
        in_specs=[pl.BlockSpec((1, SSD_CHUNK, SSD_XBC), fwd), halo(before(f_of)), halo(after(f_of)),
                  pl.BlockSpec((1, SSD_CHUNK, LANES), fwd),
                  pl.BlockSpec((1, SSD_CHUNK, SSD_XBC), bwd), halo(before(b_of)), halo(after(b_of)),
                  pl.BlockSpec((1, SSD_CHUNK, LANES), bwd),
                  pl.BlockSpec((3, SSD_XBC), lambda i, c: (0, 0)), pl.BlockSpec((1, SSD_XBC), lambda i, c: (0, 0)),
                  row, row, st_spec, st_spec],
        out_specs=[pl.BlockSpec((1, SSD_CHUNK, SSD_INNER), fwd), pl.BlockSpec((1, SSD_CHUNK, SSD_INNER), bwd),
                   pl.BlockSpec((1, SSD_CHUNK, SSD_INNER), fwd), st_spec, st_spec],
        scratch_shapes=[pltpu.VMEM(st_shape, F32), pltpu.VMEM(st_shape, F32)],
        compiler_params=_params("arbitrary", "arbitrary"),
        name="ssd_scan",
    )(xbc, xbc, xbc, dt, xbc, xbc, xbc, dt, conv_w, conv_b.reshape(1, SSD_XBC), dtb_row, a_row, h0f, h0b)


def _hy_filter_kernel(z_ref, w1_ref, b1_ref, w2_ref, b2_ref, w3_ref, fr_ref, win_ref, alt_ref,
                      fs_ref, fd_ref, nyq_ref):
    fr = fr_ref[...]
    h = jnp.sin(fr * (jnp.dot(z_ref[...], w1_ref[...], preferred_element_type=F32, precision=HI) + b1_ref[...]))
    h = jnp.sin(fr * (jnp.dot(h, w2_ref[...], preferred_element_type=F32, precision=HI) + b2_ref[...]))
    h = jnp.dot(h, w3_ref[...], preferred_element_type=F32, precision=HI)
    win = win_ref[...]
    row = lax.broadcasted_iota(jnp.int32, win.shape, 0)
    hf = h[:, :HY_WIDTH] * win
    hb = jnp.where(row == 0, 0.0, h[:, HY_WIDTH:] * win)
    nrm = lax.rsqrt(jnp.sum(hf * hf, axis=0, keepdims=True) + jnp.sum(hb * hb, axis=0, keepdims=True) + EPS)
    fs = (hf + hb) * nrm
    fs_ref[...] = fs
    fd_ref[...] = (hb - hf) * nrm
    nyq_ref[...] = jnp.sum(fs * alt_ref[...], axis=0, keepdims=True)


def _hy_filter(l, w1, b1, w2, b2, w3, freq):
    t = np.linspace(0.0, 1.0, l, dtype=np.float32)[:, None]
    w = (np.float32(2.0 * math.pi) * np.arange(l, dtype=np.float32)[:, None] / np.float32(l)).astype(np.float32)
    f = np.linspace(1e-4, HY_BANDS - 1, HY_BANDS, dtype=np.float32)[None, :]
    zf = jnp.asarray(f) * jnp.asarray(w)
    z = jnp.concatenate([jnp.asarray(t), jnp.cos(zf), -jnp.sin(zf)], axis=-1)
    z = jnp.pad(z, ((0, 0), (0, LANES - HY_EMB)))
    hid = w1.shape[1]
    w1p = jnp.pad(w1, ((0, LANES - HY_EMB), (0, LANES - hid)))
    w2p = jnp.pad(w2, ((0, LANES - hid), (0, LANES - hid)))
    w3p = jnp.pad(w3, ((0, LANES - hid), (0, 0)))
    pad_row = lambda v: jnp.pad(v.reshape(1, hid), ((0, 0), (0, LANES - hid)))
    max_decay = math.log(HY_DECAY_TARGET) / HY_SHORT_DECAY_PCT
    min_decay = math.log(HY_DECAY_TARGET) / HY_LONG_DECAY_PCT
    deltas = jnp.abs(jnp.linspace(min_decay, max_decay, HY_WIDTH, dtype=F32))
    win = jnp.exp(-jnp.asarray(t) * deltas)
    alt = jnp.asarray(np.where(np.arange(l) % 2 == 0, 1.0, -1.0).astype(np.float32)[:, None] * np.ones((1, HY_WIDTH), np.float32))
    return pl.pallas_call(
        _hy_filter_kernel,
        out_shape=[jax.ShapeDtypeStruct((l, HY_WIDTH), F32), jax.ShapeDtypeStruct((l, HY_WIDTH), F32),
                   jax.ShapeDtypeStruct((1, HY_WIDTH), F32)],
        compiler_params=pltpu.CompilerParams(vmem_limit_bytes=VMEM_LIMIT),
        name="hy_filter",
    )(z, w1p, pad_row(b1), w2p, pad_row(b2), w3p, pad_row(freq), win, alt)


def _mm_kernel(a_ref, b_ref, o_ref):
    o_ref[...] = jnp.dot(a_ref[...], b_ref[...], preferred_element_type=F32)


def _mm(a, b, tm):
    m, k = a.shape
    n = b.shape[1]
    return pl.pallas_call(
        _mm_kernel,
        out_shape=jax.ShapeDtypeStruct((m, n), F32),
        grid=(m // tm,),
        in_specs=[pl.BlockSpec((tm, k), lambda i: (i, 0)), pl.BlockSpec((k, n), lambda i: (0, 0))],
        out_specs=pl.BlockSpec((tm, n), lambda i: (i, 0)),
        compiler_params=_params("arbitrary"),
        name="mm",
    )(a, b)


def _dft_tables(l):
    n = 2 * l
    k = np.arange(l, dtype=np.int64)[:, None]
    s = np.arange(l, dtype=np.int64)[None, :]
    ang = (2.0 * np.pi / n) * ((k * s) % n).astype(np.float64)
    cos, sin = np.cos(ang), np.sin(ang)
    alt = np.where(np.arange(l) % 2 == 0, 1.0, -1.0)
    sin[0, :] = alt
    fwd = np.concatenate([cos, sin], axis=0)
    wgt = np.full((l, 1), 2.0)
    wgt[0, 0] = 1.0
    sin_i = -sin * wgt
    sin_i[0, :] = alt
    inv = np.concatenate([(cos * wgt).T, sin_i.T], axis=1) / n
    return fwd, inv


def _batch_rows(b):
    return 2 if b % 2 == 0 else 1


def _hy_conv_kernel(u_ref, cw_ref, cb_ref, ff_ref, fi_ref, ka_ref, kb_ref, kc_ref, bias_ref, o_ref,
                    ub_ref, uf_ref, x1_ref, acc_ref, *, nf):
    f = pl.program_id(1)
    w = HY_WIDTH
    rows = u_ref.shape[0]

    @pl.when(f == 0)
    def _():
        for r in range(rows):
            part = lambda k: _conv3(u_ref[r, :, k * w:(k + 1) * w].astype(F32), cw_ref[:, k * w:(k + 1) * w],
                                    cb_ref[:, k * w:(k + 1) * w])
            u = part(2) * part(0)
            cols = slice(r * w, (r + 1) * w)
            uf_ref[:, cols] = u
            ub_ref[:, cols] = u.astype(BF16)
            x1_ref[:, cols] = part(1)
        acc_ref[...] = jnp.zeros_like(acc_ref)

    uf = jnp.dot(ff_ref[0], ub_ref[...], preferred_element_type=F32)
    fb = uf.shape[0] // 2
    ur, us = uf[:fb], uf[fb:]
    ka, kb, kc = [jnp.concatenate([k_ref[...]] * rows, axis=1) for k_ref in (ka_ref, kb_ref, kc_ref)]
    p = jnp.concatenate([ur * ka + us * kb, ur * kb - us * kc], axis=0).astype(BF16)
    acc_ref[...] += jnp.dot(fi_ref[0], p, preferred_element_type=F32)

    @pl.when(f == nf - 1)
    def _():
        for r in range(rows):
            cols = slice(r * w, (r + 1) * w)
            o_ref[r] = x1_ref[:, cols] * (acc_ref[:, cols] + uf_ref[:, cols] * bias_ref[...])


def _hy_conv(proj, conv_w, conv_b, fs, fd, nyq, bias):
    b, l, _ = proj.shape
    rows = _batch_rows(b)
    fwd, inv = _dft_tables(l)
    fb = min(DFT_BIN_TILE, l)
    nf = l // fb
    ff = jnp.asarray(np.stack([np.concatenate([fwd[i * fb:(i + 1) * fb], fwd[l + i * fb:l + (i + 1) * fb]], axis=0)
                               for i in range(nf)]), dtype=BF16)
    fi = jnp.asarray(np.stack([np.concatenate([inv[:, i * fb:(i + 1) * fb], inv[:, l + i * fb:l + (i + 1) * fb]], axis=1)
                               for i in range(nf)]), dtype=BF16)
    fwd_b = jnp.asarray(fwd, dtype=BF16)
    tm = min(TOKEN_TILE, l)
    k_r = _mm(fwd_b[:l], fs.astype(BF16), tm)
    k_i = _mm(fwd_b[l:], fd.astype(BF16), tm)
    first = (jnp.arange(l) == 0)[:, None]
    ka = k_r
    kb = jnp.where(first, 0.0, k_i)
    kc = jnp.where(first, -nyq, k_r)
    return pl.pallas_call(
        functools.partial(_hy_conv_kernel, nf=nf),
        out_shape=jax.ShapeDtypeStruct((b, l, HY_WIDTH), F32),
        grid=(b // rows, nf),
        in_specs=[pl.BlockSpec((rows, l, 3 * HY_WIDTH), lambda i, f: (i, 0, 0)),
                  pl.BlockSpec((3, 3 * HY_WIDTH), lambda i, f: (0, 0)),
                  pl.BlockSpec((1, 3 * HY_WIDTH), lambda i, f: (0, 0)),
                  pl.BlockSpec((1, 2 * fb, l), lambda i, f: (f, 0, 0)),
                  pl.BlockSpec((1, l, 2 * fb), lambda i, f: (f, 0, 0)),
                  pl.BlockSpec((fb, HY_WIDTH), lambda i, f: (f, 0)),
                  pl.BlockSpec((fb, HY_WIDTH), lambda i, f: (f, 0)),
                  pl.BlockSpec((fb, HY_WIDTH), lambda i, f: (f, 0)),
                  pl.BlockSpec((1, HY_WIDTH), lambda i, f: (0, 0))],
        out_specs=pl.BlockSpec((rows, l, HY_WIDTH), lambda i, f: (i, 0, 0)),
        scratch_shapes=[pltpu.VMEM((l, rows * HY_WIDTH), BF16), pltpu.VMEM((l, rows * HY_WIDTH), F32),
                        pltpu.VMEM((l, rows * HY_WIDTH), F32), pltpu.VMEM((l, rows * HY_WIDTH), F32)],
        compiler_params=_params("arbitrary", "arbitrary"),
        name="hy_conv",
    )(proj, conv_w, conv_b.reshape(1, 3 * HY_WIDTH), ff, fi, ka, kb, kc, bias.reshape(1, HY_WIDTH))


def _fnet_kernel(x_ref, cc_ref, sc_ref, m_ref, o_ref, xx_ref):
    j = pl.program_id(1)
    rows, l, c = x_ref.shape

    @pl.when(j == 0)
    def _():
        for r in range(rows):
            xb = x_ref[r].astype(BF16)
            cols = slice(r * c, (r + 1) * c)
            xx_ref[:l, cols] = jnp.dot(xb, cc_ref[...], preferred_element_type=F32).astype(BF16)
            xx_ref[l:, cols] = jnp.dot(xb, sc_ref[...], preferred_element_type=F32).astype(BF16)

    y = jnp.dot(m_ref[...], xx_ref[...], preferred_element_type=F32)
    for r in range(rows):
        o_ref[r] = y[:, r * c:(r + 1) * c]


def _fnet(x):
    b, l, c = x.shape
    rows = _batch_rows(b)
    gd = FN_GROUP_DIM
    kc = np.arange(gd)[:, None] * np.arange(gd)[None, :]
    ang_c = 2.0 * np.pi * (kc % gd) / gd
    eye = np.eye(FN_GROUPS)
    cblk = jnp.asarray(np.kron(eye, np.cos(ang_c)), dtype=BF16)
    sblk = jnp.asarray(np.kron(eye, np.sin(ang_c)), dtype=BF16)
    kl = (np.arange(l, dtype=np.int64)[:, None] * np.arange(l, dtype=np.int64)[None, :]) % l
    ang_l = 2.0 * np.pi * kl / l
    scale = 1.0 / math.sqrt(l * gd)
    mat = jnp.asarray(np.concatenate([np.cos(ang_l), -np.sin(ang_l)], axis=1) * scale, dtype=BF16)
    tr = min(TOKEN_TILE, l)
    return pl.pallas_call(
        _fnet_kernel,
        out_shape=jax.ShapeDtypeStruct((b, l, c), F32),
        grid=(b // rows, l // tr),
        in_specs=[pl.BlockSpec((rows, l, c), lambda i, r: (i, 0, 0)),
                  pl.BlockSpec((c, c), lambda i, r: (0, 0)),
                  pl.BlockSpec((c, c), lambda i, r: (0, 0)),
                  pl.BlockSpec((tr, 2 * l), lambda i, r: (r, 0))],
        out_specs=pl.BlockSpec((rows, tr, c), lambda i, r: (i, r, 0)),
        scratch_shapes=[pltpu.VMEM((2 * l, rows * c), BF16)],
        compiler_params=_params("arbitrary", "arbitrary"),
        name="fnet",
    )(x, cblk, sblk, mat)


def _outproj_kernel(x_ref, ga_ref, yhy_ref, yf_ref, yb_ref, xs_ref, z_ref, yfn_ref, dsk_ref, ng_ref,
                    w1_ref, w2_ref, w3_ref, o_ref):
    y = yf_ref[0] + yb_ref[0] + dsk_ref[...] * xs_ref[0]
    y = y * _silu(z_ref[0])
    gw = SSD_INNER // SSD_GROUPS
    parts = []
    for g in range(SSD_GROUPS):
        yg = y[:, g * gw:(g + 1) * gw]
        parts.append(yg * lax.rsqrt(jnp.mean(yg * yg, axis=-1, keepdims=True) + EPS))
    yn = jnp.concatenate(parts, axis=1) * ng_ref[...]
    out = jnp.dot(yhy_ref[0].astype(BF16), w1_ref[...], preferred_element_type=F32)
    out += jnp.dot(yn.astype(BF16), w2_ref[...], preferred_element_type=F32)
    out += jnp.dot(yfn_ref[0].astype(BF16), w3_ref[...], preferred_element_type=F32)
    o_ref[0] = x_ref[0] + ga_ref[0] * out


def _out_proj(x, ga, y_hy, y_f, y_b, xbc, z, y_fn, dsk_row, ng_row, w1, w2, w3):
    b, l, d = x.shape
    tm = min(TOKEN_TILE, l)
    tok = lambda n: pl.BlockSpec((1, tm, n), lambda i, j: (i, j, 0))
    const = lambda a: pl.BlockSpec(a.shape, lambda i, j: (0,) * a.ndim)
    return pl.pallas_call(
        _outproj_kernel,
        out_shape=jax.ShapeDtypeStruct((b, l, d), F32),
        grid=(b, l // tm),
        in_specs=[tok(d), pl.BlockSpec((1, 1, d), lambda i, j: (i, 0, 0)), tok(HY_WIDTH), tok(SSD_INNER),
                  tok(SSD_INNER), tok(SSD_INNER), tok(SSD_INNER), tok(FN_WIDTH), const(dsk_row), const(ng_row),
                  const(w1), const(w2), const(w3)],
        out_specs=tok(d),
        compiler_params=_params("arbitrary", "arbitrary"),
        name="out_proj",
    )(x, ga, y_hy, y_f, y_b, xbc, z, y_fn, dsk_row, ng_row, w1, w2, w3)


def _peer_scores(x_ref, g_ref, sh_ref, sc_ref, wq_ref, k1_ref, k2_ref, ht_ref, s1_ref, s2_ref):
    hm = _normmod(x_ref[...], g_ref[...], sh_ref[0], sc_ref[0])
    ht_ref[...] = hm.T.astype(F8)
    qb = jnp.dot(hm.astype(BF16), wq_ref[...], preferred_element_type=F32).astype(BF16)
    nt = (((1,), (1,)), ((), ()))
    for h in range(PEER_HEADS):
        q1 = qb[:, (2 * h) * PEER_HALF:(2 * h + 1) * PEER_HALF]
        q2 = qb[:, (2 * h + 1) * PEER_HALF:(2 * h + 2) * PEER_HALF]
        s1_ref[h] = lax.dot_general(k1_ref[...], q1, nt, preferred_element_type=F32)
        s2_ref[h] = lax.dot_general(k2_ref[...], q2, nt, preferred_element_type=F32)


def _sorting_network(n):
    pairs = []

    def merge(lo, m, r):
        step = 2 * r
        if step < m:
            merge(lo, m, step)
            merge(lo + r, m, step)
            for i in range(lo + r, lo + m - r, step):
                pairs.append((i, i + r))
        else:
            pairs.append((lo, lo + r))

    def sort(lo, m):
        if m > 1:
            half = m // 2
            sort(lo, half)
            sort(lo + half, half)
            merge(lo, m, 1)

    sort(0, n)
    return tuple(pairs)


NEG = -3.0e38
N_TOP = PEER_TOPK + 1


def _top_sorted(rows, n_out):
    rows = list(rows)
    for i, j in _sorting_network(len(rows)):
        rows[i], rows[j] = jnp.maximum(rows[i], rows[j]), jnp.minimum(rows[i], rows[j])
    sub = lax.broadcasted_iota(jnp.int32, rows[0].shape, 0)
    out = []
    for r in range(n_out):
        m = jnp.max(rows[0], axis=0, keepdims=True)
        out.append(m)
        first = jnp.min(jnp.where(rows[0] == m, sub, SUBLANES), axis=0, keepdims=True)
        hit = sub == first
        keep = min(len(rows), n_out - r - 1)
        rows = [jnp.where(hit, rows[k + 1] if k + 1 < len(rows) else NEG, rows[k]) for k in range(keep)]
    return out


def _count_above(vals, x, strict):
    assert len(vals) == 16
    test = (lambda v: v > x) if strict else (lambda v: v >= x)
    c8 = test(vals[7])
    c4 = test(jnp.where(c8, vals[11], vals[3]))
    c2 = test(jnp.where(c8, jnp.where(c4, vals[13], vals[9]), jnp.where(c4, vals[5], vals[1])))
    upper = jnp.where(c4, jnp.where(c2, vals[14], vals[12]), jnp.where(c2, vals[10], vals[8]))
    lower = jnp.where(c4, jnp.where(c2, vals[6], vals[4]), jnp.where(c2, vals[2], vals[0]))
    c1 = test(jnp.where(c8, upper, lower))
    r = (jnp.where(c8, 8.0, 0.0) + jnp.where(c4, 4.0, 0.0)) + (jnp.where(c2, 2.0, 0.0) + jnp.where(c1, 1.0, 0.0))
    return jnp.where(test(vals[15]), 16.0, r)


def _dup_bf16(x):
    hi = pltpu.bitcast(x.astype(BF16).astype(F32), jnp.uint32)
    return hi | (hi >> 16)


def _peer_stats_kernel(s1_ref, s2_ref, r2_ref, a2_ref, ns_ref, cc_ref):
    nrow = N_KEYS // SUBLANES
    t = s1_ref.shape[2]
    sub = lax.broadcasted_iota(jnp.int32, (SUBLANES, t), 0)
    for h in range(PEER_HEADS):
        s1_rows = [s1_ref[h, SUBLANES * k:SUBLANES * (k + 1), :] for k in range(nrow)]
        s2_rows = [s2_ref[h, SUBLANES * k:SUBLANES * (k + 1), :] for k in range(nrow)]
        v1 = _top_sorted(s1_rows, N_TOP)
        v2 = _top_sorted(s2_rows, N_TOP)
        cands = [v1[i] + v2[j] for i in range(N_TOP) for j in range(N_TOP) if (i + 1) * (j + 1) <= N_TOP]
        packed = []
        for k in range(0, len(cands), SUBLANES):
            blk = jnp.full((SUBLANES, t), NEG, F32)
            for s, cv in enumerate(cands[k:k + SUBLANES]):
                blk = jnp.where(sub == s, cv, blk)
            packed.append(blk)
        ids = [sub + SUBLANES * k for k in range(len(packed))]
        big = SUBLANES * len(packed)
        top = []
        for r in range(N_TOP):
            m = packed[0]
            for blk in packed[1:]:
                m = jnp.maximum(m, blk)
            m = jnp.max(m, axis=0, keepdims=True)
            top.append(m)
            sel = jnp.where(packed[0] == m, ids[0], big)
            for blk, idk in zip(packed[1:], ids[1:]):
                sel = jnp.minimum(sel, jnp.where(blk == m, idk, big))
            sel = jnp.min(sel, axis=0, keepdims=True)
            packed = [jnp.where(idk == sel, NEG, blk) for blk, idk in zip(packed, ids)]
        mx = top[0]
        z = jnp.zeros((1, t), F32)
        for r in range(PEER_TOPK):
            z = z + jnp.exp(top[r] - mx)
        tau = 0.5 * (top[PEER_TOPK - 1] + top[PEER_TOPK])
        cscale = 0.5 / z
        for k in range(0, nrow, 2):
            pair = slice(SUBLANES * k, SUBLANES * (k + 2))
            r2 = [_count_above(v2[:PEER_TOPK], s2_rows[k + q], True) for q in range(2)]
            r2_ref[h, pair, :] = jnp.concatenate(r2, axis=0).astype(BF16)
            a2 = [jnp.exp(s2_rows[k + q] - v2[0]) for q in range(2)]
            a2_ref[h, pair, :] = jnp.concatenate(a2, axis=0).astype(BF16)
        for k in range(nrow):
            blk = slice(SUBLANES * k, SUBLANES * (k + 1))
            ns_ref[h, blk, :] = _dup_bf16(_count_above(v2[:PEER_TOPK], tau - s1_rows[k], False))
            cc_ref[h, blk, :] = _dup_bf16(jnp.exp(s1_rows[k] - v1[0]) * cscale)


def _peer_select_kernel(x_ref, g_ref, sh_ref, sc_ref, wq_ref, k1_ref, k2_ref,
                        ht_ref, r2_ref, a2_ref, ns_ref, cc_ref, s1_ref, s2_ref):
    _peer_scores(x_ref, g_ref, sh_ref, sc_ref, wq_ref, k1_ref, k2_ref, ht_ref, s1_ref, s2_ref)
    _peer_stats_kernel(s1_ref, s2_ref, r2_ref, a2_ref, ns_ref, cc_ref)


def _peer_select(x2, g, sh, sc, wq, k1, k2, l):
    n, d = x2.shape
    tm = min(SELECT_TOKENS, l)
    per = l // tm
    shp = (PEER_HEADS, N_KEYS, n)
    blk = pl.BlockSpec((PEER_HEADS, N_KEYS, tm), lambda i: (0, 0, i))
    return pl.pallas_call(
        _peer_select_kernel,
        out_shape=[jax.ShapeDtypeStruct((d, n), F8),
                   jax.ShapeDtypeStruct(shp, BF16), jax.ShapeDtypeStruct(shp, BF16),
                   jax.ShapeDtypeStruct(shp, jnp.uint32), jax.ShapeDtypeStruct(shp, jnp.uint32)],
        grid=(n // tm,),
        in_specs=[pl.BlockSpec((tm, d), lambda i: (i, 0)),
                  pl.BlockSpec((1, d), lambda i: (0, 0)),
                  pl.BlockSpec((1, 1, d), lambda i: (i // per, 0, 0)),
                  pl.BlockSpec((1, 1, d), lambda i: (i // per, 0, 0)),
                  pl.BlockSpec(wq.shape, lambda i: (0, 0)),
                  pl.BlockSpec(k1.shape, lambda i: (0, 0)),
                  pl.BlockSpec(k2.shape, lambda i: (0, 0))],
        out_specs=[pl.BlockSpec((d, tm), lambda i: (0, i)), blk, blk, blk, blk],
        scratch_shapes=[pltpu.VMEM((PEER_HEADS, N_KEYS, tm), F32), pltpu.VMEM((PEER_HEADS, N_KEYS, tm), F32)],
        compiler_params=_params("arbitrary"),
        name="peer_select",
    )(x2, g, sh, sc, wq, k1, k2)


E_TILE = 2048
E_BLOCK = 2 * E_TILE
PEER_SCALE = 32.0


def _gelu_tanh_x2_scaled(s):
    c0 = math.sqrt(2.0 / math.pi) / PEER_SCALE
    c1 = math.sqrt(2.0 / math.pi) * 0.044715 / PEER_SCALE ** 3
    return s + s * jnp.tanh(s * (c0 + c1 * (s * s)))


ROWS_PER_TILE = E_TILE // N_KEYS


def _rows_bf16(word_row):
    words = word_row | jnp.zeros((SUBLANES, word_row.shape[1]), jnp.uint32)
    tile = pltpu.bitcast(words, BF16)
    return jnp.concatenate([tile] * (N_KEYS // tile.shape[0]), axis=0)


def _peer_mix_kernel(ht_ref, u_ref, vt_ref, r2_ref, a2_ref, nsp_ref, ccp_ref, nsc_ref, ccc_ref, x_ref, ga_ref,
                     gf_ref, o_ref, st0_ref, st1_ref, act0_ref, act1_ref, acc_ref, *, nblk, final_norm):
    m = pl.program_id(1)

    def scores(half, st_ref):
        st_ref[...] = jnp.dot(u_ref[half * E_TILE:(half + 1) * E_TILE, :], ht_ref[...],
                              preferred_element_type=F32).astype(BF16)

    def gate(st_ref, act_ref, ns_ref, cc_ref):
        for e in range(ROWS_PER_TILE):
            g = None
            for h in range(PEER_HEADS):
                nb = _rows_bf16(ns_ref[h, e:e + 1, :])
                cb = _rows_bf16(cc_ref[h, e:e + 1, :])
                a2 = a2_ref[h]
                term = jnp.where(r2_ref[h] < nb, a2, jnp.zeros_like(a2)) * cb
                g = term if g is None else g + term
            rows = slice(e * N_KEYS, (e + 1) * N_KEYS)
            act_ref[rows, :] = (_gelu_tanh_x2_scaled(st_ref[rows, :]) * g).astype(F8)

    def accumulate(half, act_ref):
        acc_ref[...] += jnp.dot(vt_ref[:, half * E_TILE:(half + 1) * E_TILE], act_ref[...],
                                preferred_element_type=F32)

    @pl.when(m == 0)
    def _():
        acc_ref[...] = jnp.zeros_like(acc_ref)
        scores(0, st0_ref)
        gate(st0_ref, act0_ref, nsc_ref, ccc_ref)
        scores(1, st1_ref)

    @pl.when((m > 0) & (m < nblk))
    def _():
        accumulate(0, act0_ref)
        gate(st1_ref, act1_ref, nsp_ref, ccp_ref)
        scores(0, st0_ref)
        accumulate(1, act1_ref)
        gate(st0_ref, act0_ref, nsc_ref, ccc_ref)
        scores(1, st1_ref)

    @pl.when(m == nblk)
    def _():
        accumulate(0, act0_ref)
        gate(st1_ref, act1_ref, nsp_ref, ccp_ref)
        accumulate(1, act1_ref)
        y = x_ref[...] + ga_ref[0] * (acc_ref[...].T * (1.0 / PEER_SCALE))
        if final_norm:
            y = y * lax.rsqrt(jnp.mean(y * y, axis=-1, keepdims=True) + EPS) * gf_ref[...]
        o_ref[...] = y


def _peer_mix(ht, u, vt, r2, a2, ns, cc, x2, ga, l, g_final=None):
    d, n = ht.shape
    n_exp = u.shape[0]
    t = min(TOKEN_TILE, l)
    per = l // t
    nblk = n_exp // E_BLOCK
    n_tiles = n_exp // E_TILE
    sblk = pl.BlockSpec((PEER_HEADS, N_KEYS, t), lambda i, m: (0, 0, i))
    prev = pl.BlockSpec((PEER_HEADS, ROWS_PER_TILE, t), lambda i, m: (0, jnp.maximum(2 * m - 1, 0), i))
    cur = pl.BlockSpec((PEER_HEADS, ROWS_PER_TILE, t), lambda i, m: (0, jnp.minimum(2 * m, n_tiles - 1), i))
    return pl.pallas_call(
        functools.partial(_peer_mix_kernel, nblk=nblk, final_norm=g_final is not None),
        out_shape=jax.ShapeDtypeStruct((n, d), F32),
        grid=(n // t, nblk + 1),
        in_specs=[pl.BlockSpec((d, t), lambda i, m: (0, i)),
                  pl.BlockSpec((E_BLOCK, d), lambda i, m: (jnp.minimum(m, nblk - 1), 0)),
                  pl.BlockSpec((d, E_BLOCK), lambda i, m: (0, jnp.maximum(m - 1, 0))),
                  sblk, sblk, prev, prev, cur, cur,
                  pl.BlockSpec((t, d), lambda i, m: (i, 0)),
                  pl.BlockSpec((1, 1, d), lambda i, m: (i // per, 0, 0)),
                  pl.BlockSpec((1, d), lambda i, m: (0, 0))],
        out_specs=pl.BlockSpec((t, d), lambda i, m: (i, 0)),
        scratch_shapes=[pltpu.VMEM((E_TILE, t), BF16), pltpu.VMEM((E_TILE, t), BF16),
                        pltpu.VMEM((E_TILE, t), F8), pltpu.VMEM((E_TILE, t), F8),
                        pltpu.VMEM((d, t), F32)],
        compiler_params=_params("arbitrary", "arbitrary"),
        name="peer_mix",
    )(ht, u, vt, r2, a2, ns, cc, ns, cc, x2, ga, jnp.ones((1, d), F32) if g_final is None else g_final)


def _pad_lanes(a):
    return jnp.pad(a, ((0, 0), (0, LANES - a.shape[1])))


def _token_mix_inputs(p, proj_hy, l):
    fs, fd, nyq = _hy_filter(l, p['hf_w1'], p['hf_b1'], p['hf_w2'], p['hf_b2'], p['hf_w3'], p['hf_freq'])
    return _hy_conv(proj_hy, p['hy_conv_w'], p['hy_conv_b'], fs, fd, nyq, p['hy_bias'])


def _peer(x, p, sh, sc, ga, g_final=None):
    b, l, d = x.shape
    x2 = x.reshape(b * l, d)
    ht, r2, a2, ns, cc = _peer_select(x2, p['g_norm2'], sh, sc, p['wq'], p['k1'], p['k2'], l)
    out = _peer_mix(ht, p['u'], p['vt'], r2, a2, ns, cc, x2, ga, l, g_final)
    return out.reshape(b, l, d)


def _layer(xl, xc, mod_l, mod_c, p, ctx_out, g_final=None):
    b = xl.shape[0]
    d = xl.shape[2]
    sh1, sc1, ga1, sh2, sc2, ga2 = [m.reshape(b, 1, d) for m in jnp.split(mod_l, 6, axis=-1)]
    csh1, csc1, cga1, csh2, csc2, cga2 = [jnp.broadcast_to(m.reshape(1, 1, d), (b, 1, d))
                                          for m in jnp.split(mod_c, 6, axis=-1)]
    w_in = p['w_in']
    w_hy, w_z, w_xbc, w_fn = (w_in[:, OFF_HY:OFF_Z], w_in[:, OFF_Z:OFF_XBC], w_in[:, OFF_XBC:OFF_DT],
                              w_in[:, OFF_FN:D_IN_PROJ])
    w_dt = _pad_lanes(w_in[:, OFF_DT:OFF_FN])
    g1 = p['g_norm1']
    all_w, all_dt = [w_hy, w_z, w_xbc, w_dt, w_fn], [BF16, F32, F32, F32, BF16]
    pl_hy, pl_z, pl_xbc, pl_dt, pl_fn = _in_proj(xl, g1, sh1, sc1, all_w, all_dt)
    if ctx_out:
        pc_hy, pc_z, pc_xbc, pc_dt, pc_fn = _in_proj(xc, g1, csh1, csc1, all_w, all_dt)
    else:
        pc_xbc, pc_dt = _in_proj(xc, g1, csh1, csc1, [w_xbc, w_dt], [F32, F32])

    dtb, arow, cw, cb = p['dtb_row'], p['a_row'], p['ssd_conv_w'], p['ssd_conv_b']
    h0 = jnp.zeros((b, SSD_GROUPS, SSD_STATE, SSD_INNER // SSD_GROUPS), F32)
    yc_f, yc_b, xs_c, hc_f, hc_b = _ssd_scan(pc_xbc, pc_dt, cw, cb, dtb, arow, h0, h0)
    yl_f, yl_b, xs_l, _, _ = _ssd_scan(pl_xbc, pl_dt, cw, cb, dtb, arow, hc_f, hc_b)

    def token_mix(x, ga, proj_hy, y_f, y_b, xs, z, proj_fn):
        y_hy = _token_mix_inputs(p, proj_hy, x.shape[1])
        y_fn = _fnet(proj_fn)
        return _out_proj(x, ga, y_hy, y_f, y_b, xs, z, y_fn, p['dsk_row'], p['ng_row'], p['wo1'], p['wo2'], p['wo3'])

    xl = token_mix(xl, ga1, pl_hy, yl_f, yl_b, xs_l, pl_z, pl_fn)
    xl = _peer(xl, p, sh2, sc2, ga2, g_final)
    if ctx_out:
        xc = token_mix(xc, cga1, pc_hy, yc_f, yc_b, xs_c, pc_z, pc_fn)
        xc = _peer(xc, p, csh2, csc2, cga2)
    return xl, xc


def kernel(x, c, ctx, c_ctx, w_ada, b_ada, g_norm1, g_norm2, w_in, hy_conv_w, hy_conv_b, hf_w1, hf_b1, hf_w2, hf_b2, hf_w3, hf_freq, hy_bias, ssd_conv_w, ssd_conv_b, ssd_dt_bias, ssd_a_log, ssd_d, ssd_norm_g, w_out, peer_wq, peer_k1, peer_k2, peer_u, peer_v, g_final):
    depth = w_ada.shape[0]
    b, l, d = x.shape
    rows = -(-(b + 1) // SUBLANES) * SUBLANES
    cc = jnp.concatenate([c, c_ctx[None, :], jnp.zeros((rows - b - 1, d), F32)], axis=0)
    mods = _ada_mod(cc, w_ada, b_ada)
    xl, xc = x, ctx
    for i in range(depth):
        wo = w_out[i].astype(BF16)
        p = {
            'g_norm1': g_norm1[i].reshape(1, d), 'g_norm2': g_norm2[i].reshape(1, d),
            'w_in': w_in[i].astype(BF16),
            'hy_conv_w': hy_conv_w[i], 'hy_conv_b': hy_conv_b[i],
            'hf_w1': hf_w1[i], 'hf_b1': hf_b1[i], 'hf_w2': hf_w2[i], 'hf_b2': hf_b2[i], 'hf_w3': hf_w3[i],
            'hf_freq': hf_freq[i], 'hy_bias': hy_bias[i],
            'ssd_conv_w': ssd_conv_w[i], 'ssd_conv_b': ssd_conv_b[i],
            'dtb_row': _pad_lanes(ssd_dt_bias[i].reshape(1, 2 * SSD_HEADS)),
            'a_row': _pad_lanes(-jnp.exp(ssd_a_log[i].astype(F32)).reshape(1, 2 * SSD_HEADS)),
            'dsk_row': jnp.repeat(ssd_d[i].astype(F32), SSD_HEAD_DIM).reshape(1, SSD_INNER),
            'ng_row': ssd_norm_g[i].reshape(1, SSD_INNER),
            'wo1': wo[:HY_WIDTH], 'wo2': wo[HY_WIDTH:HY_WIDTH + SSD_INNER], 'wo3': wo[HY_WIDTH + SSD_INNER:],
            'wq': peer_wq[i].astype(BF16), 'k1': peer_k1[i].astype(BF16), 'k2': peer_k2[i].astype(BF16),
            'u': (peer_u[i] * PEER_SCALE).astype(F8), 'vt': peer_v[i].astype(F8).T,
        }
        last = i == depth - 1
        xl, xc = _layer(xl, xc, mods[i, :b], mods[i, b], p, not last, g_final.reshape(1, d) if last else None)
    return xl
```

```python
import functools
import math

import jax
import jax.numpy as jnp
import numpy as np
from jax import lax
from jax.experimental import pallas as pl
from jax.experimental.pallas import tpu as pltpu

F32 = jnp.float32
BF16 = jnp.bfloat16
F8 = jnp.float8_e4m3fn
HI = lax.Precision.HIGHEST

EPS = 1e-6
LANES = 128
SUBLANES = 8
VMEM_LIMIT = 56 * 1024 * 1024

ADA_COL_TILE = 1536
TOKEN_TILE = 512
SELECT_TOKENS = 256
DFT_BIN_TILE = 256

HY_WIDTH = 256
HY_EMB = 33
HY_BANDS = (HY_EMB - 1) // 2
HY_DECAY_TARGET = 1e-2
HY_SHORT_DECAY_PCT = 0.3
HY_LONG_DECAY_PCT = 1.5
SSD_HEADS = 8
SSD_HEAD_DIM = 64
SSD_INNER = SSD_HEADS * SSD_HEAD_DIM
SSD_GROUPS = 2
SSD_STATE = 128
SSD_CHUNK = 128
SSD_XBC = SSD_INNER + 2 * SSD_GROUPS * SSD_STATE
FN_WIDTH = 256
FN_GROUPS = 4
FN_GROUP_DIM = FN_WIDTH // FN_GROUPS
PEER_HEADS = 8
PEER_TOPK = 16
N_KEYS = 128
PEER_HALF = 128
OFF_HY = 0
OFF_Z = OFF_HY + 3 * HY_WIDTH
OFF_XBC = OFF_Z + SSD_INNER
OFF_DT = OFF_XBC + SSD_XBC
OFF_FN = OFF_DT + 2 * SSD_HEADS
D_IN_PROJ = OFF_FN + FN_WIDTH


def _params(*sem):
    return pltpu.CompilerParams(dimension_semantics=sem, vmem_limit_bytes=VMEM_LIMIT)


def _silu(x):
    return x * jax.nn.sigmoid(x)


def _softplus(x):
    return jnp.maximum(x, 0.0) + jnp.log1p(jnp.exp(-jnp.abs(x)))


def _ada_kernel(c_ref, w_ref, b_ref, o_ref):
    s = _silu(c_ref[...])
    o_ref[0] = jnp.dot(s, w_ref[0], preferred_element_type=F32, precision=HI) + b_ref[0]


def _ada_mod(cc, w_ada, b_ada):
    depth, d, n = w_ada.shape
    r = cc.shape[0]
    tn = ADA_COL_TILE
    return pl.pallas_call(
        _ada_kernel,
        out_shape=jax.ShapeDtypeStruct((depth, r, n), F32),
        grid=(depth, n // tn),
        in_specs=[pl.BlockSpec((r, d), lambda l, j: (0, 0)),
                  pl.BlockSpec((1, d, tn), lambda l, j: (l, 0, j)),
                  pl.BlockSpec((1, 1, tn), lambda l, j: (l, 0, j))],
        out_specs=pl.BlockSpec((1, r, tn), lambda l, j: (l, 0, j)),
        compiler_params=_params("arbitrary", "arbitrary"),
        name="ada_mod",
    )(cc, w_ada, b_ada.reshape(depth, 1, n))


def _normmod(x, g, sh, sc):
    y = x * lax.rsqrt(jnp.mean(x * x, axis=-1, keepdims=True) + EPS) * g
    return y * (1.0 + sc) + sh


def _inproj_kernel(x_ref, g_ref, sh_ref, sc_ref, *rest, n_w):
    w_refs, o_refs = rest[:n_w], rest[n_w:]
    hb = _normmod(x_ref[0], g_ref[...], sh_ref[0], sc_ref[0]).astype(BF16)
    for w_ref, o_ref in zip(w_refs, o_refs):
        o_ref[0] = jnp.dot(hb, w_ref[...], preferred_element_type=F32).astype(o_ref.dtype)


def _in_proj(x, g, sh, sc, ws, dtypes):
    b, l, d = x.shape
    tm = min(TOKEN_TILE, l)
    return pl.pallas_call(
        functools.partial(_inproj_kernel, n_w=len(ws)),
        out_shape=[jax.ShapeDtypeStruct((b, l, w.shape[1]), dt) for w, dt in zip(ws, dtypes)],
        grid=(b, l // tm),
        in_specs=[pl.BlockSpec((1, tm, d), lambda i, j: (i, j, 0)),
                  pl.BlockSpec((1, d), lambda i, j: (0, 0)),
                  pl.BlockSpec((1, 1, d), lambda i, j: (i, 0, 0)),
                  pl.BlockSpec((1, 1, d), lambda i, j: (i, 0, 0))]
        + [pl.BlockSpec(w.shape, lambda i, j: (0, 0)) for w in ws],
        out_specs=[pl.BlockSpec((1, tm, w.shape[1]), lambda i, j: (i, j, 0)) for w in ws],
        compiler_params=_params("arbitrary", "arbitrary"),
        name="in_proj",
    )(x, g, sh, sc, *ws)


def _conv3(u, w, bias):
    l = u.shape[0]
    row = lax.broadcasted_iota(jnp.int32, u.shape, 0)
    um = jnp.where(row == 0, 0.0, pltpu.roll(u, 1, 0))
    up = jnp.where(row == l - 1, 0.0, pltpu.roll(u, l - 1, 0))
    return um * w[0:1, :] + u * w[1:2, :] + up * w[2:3, :] + bias


def _split3(x):
    hi = x.astype(BF16)
    r1 = x - hi.astype(F32)
    mid = r1.astype(BF16)
    lo = (r1 - mid.astype(F32)).astype(BF16)
    return hi, mid, lo


def _ssd_chunk(xbc, dt_raw, dtb, arow, st_ref, direction):
    q = SSD_CHUNK
    x = xbc[:, :SSD_INNER]
    gn = SSD_GROUPS * SSD_STATE
    bm = xbc[:, SSD_INNER:SSD_INNER + gn].astype(BF16)
    cm = xbc[:, SSD_INNER + gn:].astype(BF16)
    dtp = _softplus(dt_raw + dtb)
    da = dtp * arow
    ii = lax.broadcasted_iota(jnp.int32, (q, q), 0)
    jj = lax.broadcasted_iota(jnp.int32, (q, q), 1)
    tri = (ii >= jj) if direction == 0 else (ii <= jj)
    tri_b = tri.astype(BF16)
    cs = jnp.dot(jnp.concatenate([tri_b] * 3, axis=1), jnp.concatenate(_split3(da), axis=0),
                 preferred_element_type=F32)
    cs_t = cs.T
    lane = lax.broadcasted_iota(jnp.int32, (LANES, SSD_INNER), 0)
    chan = lax.broadcasted_iota(jnp.int32, (LANES, SSD_INNER), 1)
    head_of = lax.shift_right_logical(chan, int(math.log2(SSD_HEAD_DIM)))
    expand = (lane == direction * SSD_HEADS + head_of).astype(BF16)
    expand3 = jnp.concatenate([expand] * 3, axis=0)
    both = jnp.concatenate([jnp.concatenate(_split3(cs), axis=1), jnp.concatenate(_split3(dtp), axis=1)], axis=0)
    wide = jnp.dot(both, expand3, preferred_element_type=F32)
    acum, dtf = wide[:q], wide[q:]
    tot = acum[q - 1:q, :] if direction == 0 else acum[0:1, :]
    xdt = x * dtf
    ea = jnp.exp(acum)
    xdec = (jnp.exp(tot - acum) * xdt).astype(BF16)
    cd = jnp.exp(tot)
    xdt_b = xdt.astype(BF16)
    hg = SSD_HEADS // SSD_GROUPS
    gw = hg * SSD_HEAD_DIM
    outs = []
    for g in range(SSD_GROUPS):
        bg = bm[:, g * SSD_STATE:(g + 1) * SSD_STATE]
        cg = cm[:, g * SSD_STATE:(g + 1) * SSD_STATE]
        cb = lax.dot_general(cg, bg, (((1,), (1,)), ((), ())), preferred_element_type=F32)
        st = st_ref[g]
        y_g = jnp.dot(cg, st.astype(BF16), preferred_element_type=F32) * ea[:, g * gw:(g + 1) * gw]
        xg = xdt_b[:, g * gw:(g + 1) * gw]
        head_g = lax.shift_right_logical(lax.broadcasted_iota(jnp.int32, xg.shape, 1), int(math.log2(SSD_HEAD_DIM)))
        for k in range(hg):
            ln = direction * SSD_HEADS + g * hg + k
            seg = cs[:, ln:ln + 1] - cs_t[ln:ln + 1, :]
            lmat = jnp.where(tri, jnp.exp(jnp.where(tri, seg, 0.0)), 0.0)
            m = (cb * lmat).astype(BF16)
            y_g = y_g + jnp.dot(m, jnp.where(head_g == k, xg, jnp.zeros_like(xg)), preferred_element_type=F32)
        outs.append(y_g)
        sg = lax.dot_general(bg, xdec[:, g * gw:(g + 1) * gw], (((0,), (0,)), ((), ())),
                             preferred_element_type=F32)
        st_ref[g] = st * cd[:, g * gw:(g + 1) * gw] + sg
    return jnp.concatenate(outs, axis=1)


def _conv3_silu_chunk(cur_ref, before_ref, after_ref, w, bias, chunk, nc):
    cur = cur_ref[0]
    q = cur.shape[0]
    row = lax.broadcasted_iota(jnp.int32, cur.shape, 0)
    prev_row = jnp.where(chunk == 0, 0.0, before_ref[0, SUBLANES - 1:SUBLANES, :])
    next_row = jnp.where(chunk == nc - 1, 0.0, after_ref[0, 0:1, :])
    um = jnp.where(row == 0, prev_row, pltpu.roll(cur, 1, 0))
    up = jnp.where(row == q - 1, next_row, pltpu.roll(cur, q - 1, 0))
    return _silu(um * w[0:1, :] + cur * w[1:2, :] + up * w[2:3, :] + bias)


def _ssd_kernel(xf_ref, xf0_ref, xf1_ref, dtf_ref, xb_ref, xb0_ref, xb1_ref, dtb_ref, cw_ref, cb_ref,
                bias_ref, arow_ref, h0f_ref, h0b_ref, yf_ref, yb_ref, xs_ref, hlf_ref, hlb_ref,
                stf_ref, stb_ref, *, nc):
    c = pl.program_id(1)

    @pl.when(c == 0)
    def _():
        stf_ref[...] = h0f_ref[0]
        stb_ref[...] = h0b_ref[0]

    xf = _conv3_silu_chunk(xf_ref, xf0_ref, xf1_ref, cw_ref[...], cb_ref[...], c, nc)
    xs_ref[0] = xf[:, :SSD_INNER]
    yf_ref[0] = _ssd_chunk(xf, dtf_ref[0], bias_ref[...], arow_ref[...], stf_ref, 0)
    xb = _conv3_silu_chunk(xb_ref, xb0_ref, xb1_ref, cw_ref[...], cb_ref[...], nc - 1 - c, nc)
    yb_ref[0] = _ssd_chunk(xb, dtb_ref[0], bias_ref[...], arow_ref[...], stb_ref, 1)

    @pl.when(c == nc - 1)
    def _():
        hlf_ref[0] = stf_ref[...]
        hlb_ref[0] = stb_ref[...]


def _ssd_scan(xbc, dt, conv_w, conv_b, dtb_row, a_row, h0f, h0b):
    b, l, _ = xbc.shape
    nc = l // SSD_CHUNK
    per = SSD_CHUNK // SUBLANES
    last = l // SUBLANES - 1
    fwd = lambda i, c: (i, c, 0)
    bwd = lambda i, c: (i, nc - 1 - c, 0)
    before = lambda chunk_of: (lambda i, c: (i, jnp.maximum(chunk_of(c) * per - 1, 0), 0))
    after = lambda chunk_of: (lambda i, c: (i, jnp.minimum(chunk_of(c) * per + per, last), 0))
    f_of, b_of = (lambda c: c), (lambda c: nc - 1 - c)
    halo = lambda imap: pl.BlockSpec((1, SUBLANES, SSD_XBC), imap)
    st_shape = (SSD_GROUPS, SSD_STATE, SSD_INNER // SSD_GROUPS)
    st_spec = pl.BlockSpec((1,) + st_shape, lambda i, c: (i, 0, 0, 0))
    row = pl.BlockSpec((1, LANES), lambda i, c: (0, 0))
    return pl.pallas_call(
        functools.partial(_ssd_kernel, nc=nc),
        out_shape=[jax.ShapeDtypeStruct((b, l, SSD_INNER), F32)] * 3
        + [jax.ShapeDtypeStruct((b,) + st_shape, F32)] * 2,
        grid=(b, nc),
        in_specs=[pl.BlockSpec((1, SSD_CHUNK, SSD_XBC), fwd), halo(before(f_of)), halo(after(f_of)),
                  pl.BlockSpec((1, SSD_CHUNK, LANES), fwd),
                  pl.BlockSpec((1, SSD_CHUNK, SSD_XBC), bwd), halo(before(b_of)), halo(after(b_of)),
                  pl.BlockSpec((1, SSD_CHUNK, LANES), bwd),
                  pl.BlockSpec((3, SSD_XBC), lambda i, c: (0, 0)), pl.BlockSpec((1, SSD_XBC), lambda i, c: (0, 0)),
                  row, row, st_spec, st_spec],
        out_specs=[pl.BlockSpec((1, SSD_CHUNK, SSD_INNER), fwd), pl.BlockSpec((1, SSD_CHUNK, SSD_INNER), bwd),
                   pl.BlockSpec((1, SSD_CHUNK, SSD_INNER), fwd), st_spec, st_spec],
        scratch_shapes=[pltpu.VMEM(st_shape, F32), pltpu.VMEM(st_shape, F32)],
        compiler_params=_params("arbitrary", "arbitrary"),
        name="ssd_scan",
    )(xbc, xbc, xbc, dt, xbc, xbc, xbc, dt, conv_w, conv_b.reshape(1, SSD_XBC), dtb_row, a_row, h0f, h0b)


def _hy_filter_kernel(z_ref, w1_ref, b1_ref, w2_ref, b2_ref, w3_ref, fr_ref, win_ref, alt_ref,
                      fs_ref, fd_ref, nyq_ref):
    fr = fr_ref[...]
    h = jnp.sin(fr * (jnp.dot(z_ref[...], w1_ref[...], preferred_element_type=F32, precision=HI) + b1_ref[...]))
    h = jnp.sin(fr * (jnp.dot(h, w2_ref[...], preferred_element_type=F32, precision=HI) + b2_ref[...]))
    h = jnp.dot(h, w3_ref[...], preferred_element_type=F32, precision=HI)
    win = win_ref[...]
    row = lax.broadcasted_iota(jnp.int32, win.shape, 0)
    hf = h[:, :HY_WIDTH] * win
    hb = jnp.where(row == 0, 0.0, h[:, HY_WIDTH:] * win)
    nrm = lax.rsqrt(jnp.sum(hf * hf, axis=0, keepdims=True) + jnp.sum(hb * hb, axis=0, keepdims=True) + EPS)
    fs = (hf + hb) * nrm
    fs_ref[...] = fs
    fd_ref[...] = (hb - hf) * nrm
    nyq_ref[...] = jnp.sum(fs * alt_ref[...], axis=0, keepdims=True)


def _hy_filter(l, w1, b1, w2, b2, w3, freq):
    t = np.linspace(0.0, 1.0, l, dtype=np.float32)[:, None]
    w = (np.float32(2.0 * math.pi) * np.arange(l, dtype=np.float32)[:, None] / np.float32(l)).astype(np.float32)
    f = np.linspace(1e-4, HY_BANDS - 1, HY_BANDS, dtype=np.float32)[None, :]
    zf = jnp.asarray(f) * jnp.asarray(w)
    z = jnp.concatenate([jnp.asarray(t), jnp.cos(zf), -jnp.sin(zf)], axis=-1)
    z = jnp.pad(z, ((0, 0), (0, LANES - HY_EMB)))
    hid = w1.shape[1]
    w1p = jnp.pad(w1, ((0, LANES - HY_EMB), (0, LANES - hid)))
    w2p = jnp.pad(w2, ((0, LANES - hid), (0, LANES - hid)))
    w3p = jnp.pad(w3, ((0, LANES - hid), (0, 0)))
    pad_row = lambda v: jnp.pad(v.reshape(1, hid), ((0, 0), (0, LANES - hid)))
    max_decay = math.log(HY_DECAY_TARGET) / HY_SHORT_DECAY_PCT
    min_decay = math.log(HY_DECAY_TARGET) / HY_LONG_DECAY_PCT
    deltas = jnp.abs(jnp.linspace(min_decay, max_decay, HY_WIDTH, dtype=F32))
    win = jnp.exp(-jnp.asarray(t) * deltas)
    alt = jnp.asarray(np.where(np.arange(l) % 2 == 0, 1.0, -1.0).astype(np.float32)[:, None] * np.ones((1, HY_WIDTH), np.float32))
    return pl.pallas_call(
        _hy_filter_kernel,
        out_shape=[jax.ShapeDtypeStruct((l, HY_WIDTH), F32), jax.ShapeDtypeStruct((l, HY_WIDTH), F32),
                   jax.ShapeDtypeStruct((1, HY_WIDTH), F32)],
        compiler_params=pltpu.CompilerParams(vmem_limit_bytes=VMEM_LIMIT),
        name="hy_filter",
    )(z, w1p, pad_row(b1), w2p, pad_row(b2), w3p, pad_row(freq), win, alt)


def _mm_kernel(a_ref, b_ref, o_ref):
    o_ref[...] = jnp.dot(a_ref[...], b_ref[...], preferred_element_type=F32)


def _mm(a, b, tm):
    m, k = a.shape
    n = b.shape[1]
    return pl.pallas_call(
        _mm_kernel,
        out_shape=jax.ShapeDtypeStruct((m, n), F32),
        grid=(m // tm,),
        in_specs=[pl.BlockSpec((tm, k), lambda i: (i, 0)), pl.BlockSpec((k, n), lambda i: (0, 0))],
        out_specs=pl.BlockSpec((tm, n), lambda i: (i, 0)),
        compiler_params=_params("arbitrary"),
        name="mm",
    )(a, b)


def _dft_tables(l):
    n = 2 * l
    k = np.arange(l, dtype=np.int64)[:, None]
    s = np.arange(l, dtype=np.int64)[None, :]
    ang = (2.0 * np.pi / n) * ((k * s) % n).astype(np.float64)
    cos, sin = np.cos(ang), np.sin(ang)
    alt = np.where(np.arange(l) % 2 == 0, 1.0, -1.0)
    sin[0, :] = alt
    fwd = np.concatenate([cos, sin], axis=0)
    wgt = np.full((l, 1), 2.0)
    wgt[0, 0] = 1.0
    sin_i = -sin * wgt
    sin_i[0, :] = alt
    inv = np.concatenate([(cos * wgt).T, sin_i.T], axis=1) / n
    return fwd, inv


def _batch_rows(b):
    return 2 if b % 2 == 0 else 1


def _hy_conv_kernel(u_ref, cw_ref, cb_ref, ff_ref, fi_ref, ka_ref, kb_ref, kc_ref, bias_ref, o_ref,
                    ub_ref, uf_ref, x1_ref, acc_ref, *, nf):
    f = pl.program_id(1)
    w = HY_WIDTH
    rows = u_ref.shape[0]

    @pl.when(f == 0)
    def _():
        for r in range(rows):
            part = lambda k: _conv3(u_ref[r, :, k * w:(k + 1) * w].astype(F32), cw_ref[:, k * w:(k + 1) * w],
                                    cb_ref[:, k * w:(k + 1) * w])
            u = part(2) * part(0)
            cols = slice(r * w, (r + 1) * w)
            uf_ref[:, cols] = u
            ub_ref[:, cols] = u.astype(BF16)
            x1_ref[:, cols] = part(1)
        acc_ref[...] = jnp.zeros_like(acc_ref)

    uf = jnp.dot(ff_ref[0], ub_ref[...], preferred_element_type=F32)
    fb = uf.shape[0] // 2
    ur, us = uf[:fb], uf[fb:]
    ka, kb, kc = [jnp.concatenate([k_ref[...]] * rows, axis=1) for k_ref in (ka_ref, kb_ref, kc_ref)]
    p = jnp.concatenate([ur * ka + us * kb, ur * kb - us * kc], axis=0).astype(BF16)
    acc_ref[...] += jnp.dot(fi_ref[0], p, preferred_element_type=F32)

    @pl.when(f == nf - 1)
    def _():
        for r in range(rows):
            cols = slice(r * w, (r + 1) * w)
            o_ref[r] = x1_ref[:, cols] * (acc_ref[:, cols] + uf_ref[:, cols] * bias_ref[...])


def _hy_conv(proj, conv_w, conv_b, fs, fd, nyq, bias):
    b, l, _ = proj.shape
    rows = _batch_rows(b)
    fwd, inv = _dft_tables(l)
    fb = min(DFT_BIN_TILE, l)
    nf = l // fb
    ff = jnp.asarray(np.stack([np.concatenate([fwd[i * fb:(i + 1) * fb], fwd[l + i * fb:l + (i + 1) * fb]], axis=0)
                               for i in range(nf)]), dtype=BF16)
    fi = jnp.asarray(np.stack([np.concatenate([inv[:, i * fb:(i + 1) * fb], inv[:, l + i * fb:l + (i + 1) * fb]], axis=1)
                               for i in range(nf)]), dtype=BF16)
    fwd_b = jnp.asarray(fwd, dtype=BF16)
    tm = min(TOKEN_TILE, l)
    k_r = _mm(fwd_b[:l], fs.astype(BF16), tm)
    k_i = _mm(fwd_b[l:], fd.astype(BF16), tm)
    first = (jnp.arange(l) == 0)[:, None]
    ka = k_r
    kb = jnp.where(first, 0.0, k_i)
    kc = jnp.where(first, -nyq, k_r)
    return pl.pallas_call(
        functools.partial(_hy_conv_kernel, nf=nf),
        out_shape=jax.ShapeDtypeStruct((b, l, HY_WIDTH), F32),
        grid=(b // rows, nf),
        in_specs=[pl.BlockSpec((rows, l, 3 * HY_WIDTH), lambda i, f: (i, 0, 0)),
                  pl.BlockSpec((3, 3 * HY_WIDTH), lambda i, f: (0, 0)),
                  pl.BlockSpec((1, 3 * HY_WIDTH), lambda i, f: (0, 0)),
                  pl.BlockSpec((1, 2 * fb, l), lambda i, f: (f, 0, 0)),
                  pl.BlockSpec((1, l, 2 * fb), lambda i, f: (f, 0, 0)),
                  pl.BlockSpec((fb, HY_WIDTH), lambda i, f: (f, 0)),
                  pl.BlockSpec((fb, HY_WIDTH), lambda i, f: (f, 0)),
                  pl.BlockSpec((fb, HY_WIDTH), lambda i, f: (f, 0)),
                  pl.BlockSpec((1, HY_WIDTH), lambda i, f: (0, 0))],
        out_specs=pl.BlockSpec((rows, l, HY_WIDTH), lambda i, f: (i, 0, 0)),
        scratch_shapes=[pltpu.VMEM((l, rows * HY_WIDTH), BF16), pltpu.VMEM((l, rows * HY_WIDTH), F32),
                        pltpu.VMEM((l, rows * HY_WIDTH), F32), pltpu.VMEM((l, rows * HY_WIDTH), F32)],
        compiler_params=_params("arbitrary", "arbitrary"),
        name="hy_conv",
    )(proj, conv_w, conv_b.reshape(1, 3 * HY_WIDTH), ff, fi, ka, kb, kc, bias.reshape(1, HY_WIDTH))


def _fnet_kernel(x_ref, cc_ref, sc_ref, m_ref, o_ref, xx_ref):
    j = pl.program_id(1)
    rows, l, c = x_ref.shape

    @pl.when(j == 0)
    def _():
        for r in range(rows):
            xb = x_ref[r].astype(BF16)
            cols = slice(r * c, (r + 1) * c)
            xx_ref[:l, cols] = jnp.dot(xb, cc_ref[...], preferred_element_type=F32).astype(BF16)
            xx_ref[l:, cols] = jnp.dot(xb, sc_ref[...], preferred_element_type=F32).astype(BF16)

    y = jnp.dot(m_ref[...], xx_ref[...], preferred_element_type=F32)
    for r in range(rows):
        o_ref[r] = y[:, r * c:(r + 1) * c]


def _fnet(x):
    b, l, c = x.shape
    rows = _batch_rows(b)
    gd = FN_GROUP_DIM
    kc = np.arange(gd)[:, None] * np.arange(gd)[None, :]
    ang_c = 2.0 * np.pi * (kc % gd) / gd
    eye = np.eye(FN_GROUPS)
    cblk = jnp.asarray(np.kron(eye, np.cos(ang_c)), dtype=BF16)
    sblk = jnp.asarray(np.kron(eye, np.sin(ang_c)), dtype=BF16)
    kl = (np.arange(l, dtype=np.int64)[:, None] * np.arange(l, dtype=np.int64)[None, :]) % l
    ang_l = 2.0 * np.pi * kl / l
    scale = 1.0 / math.sqrt(l * gd)
    mat = jnp.asarray(np.concatenate([np.cos(ang_l), -np.sin(ang_l)], axis=1) * scale, dtype=BF16)
    tr = min(TOKEN_TILE, l)
    return pl.pallas_call(
        _fnet_kernel,
        out_shape=jax.ShapeDtypeStruct((b, l, c), F32),
        grid=(b // rows, l // tr),
        in_specs=[pl.BlockSpec((rows, l, c), lambda i, r: (i, 0, 0)),
                  pl.BlockSpec((c, c), lambda i, r: (0, 0)),
                  pl.BlockSpec((c, c), lambda i, r: (0, 0)),
                  pl.BlockSpec((tr, 2 * l), lambda i, r: (r, 0))],
        out_specs=pl.BlockSpec((rows, tr, c), lambda i, r: (i, r, 0)),
        scratch_shapes=[pltpu.VMEM((2 * l, rows * c), BF16)],
        compiler_params=_params("arbitrary", "arbitrary"),
        name="fnet",
    )(x, cblk, sblk, mat)


def _outproj_kernel(x_ref, ga_ref, yhy_ref, yf_ref, yb_ref, xs_ref, z_ref, yfn_ref, dsk_ref, ng_ref,
                    w1_ref, w2_ref, w3_ref, o_ref):
    y = yf_ref[0] + yb_ref[0] + dsk_ref[...] * xs_ref[0]
    y = y * _silu(z_ref[0])
    gw = SSD_INNER // SSD_GROUPS
    parts = []
    for g in range(SSD_GROUPS):
        yg = y[:, g * gw:(g + 1) * gw]
        parts.append(yg * lax.rsqrt(jnp.mean(yg * yg, axis=-1, keepdims=True) + EPS))
    yn = jnp.concatenate(parts, axis=1) * ng_ref[...]
    out = jnp.dot(yhy_ref[0].astype(BF16), w1_ref[...], preferred_element_type=F32)
    out += jnp.dot(yn.astype(BF16), w2_ref[...], preferred_element_type=F32)
    out += jnp.dot(yfn_ref[0].astype(BF16), w3_ref[...], preferred_element_type=F32)
    o_ref[0] = x_ref[0] + ga_ref[0] * out


def _out_proj(x, ga, y_hy, y_f, y_b, xbc, z, y_fn, dsk_row, ng_row, w1, w2, w3):
    b, l, d = x.shape
    tm = min(TOKEN_TILE, l)
    tok = lambda n: pl.BlockSpec((1, tm, n), lambda i, j: (i, j, 0))
    const = lambda a: pl.BlockSpec(a.shape, lambda i, j: (0,) * a.ndim)
    return pl.pallas_call(
        _outproj_kernel,
        out_shape=jax.ShapeDtypeStruct((b, l, d), F32),
        grid=(b, l // tm),
        in_specs=[tok(d), pl.BlockSpec((1, 1, d), lambda i, j: (i, 0, 0)), tok(HY_WIDTH), tok(SSD_INNER),
                  tok(SSD_INNER), tok(SSD_INNER), tok(SSD_INNER), tok(FN_WIDTH), const(dsk_row), const(ng_row),
                  const(w1), const(w2), const(w3)],
        out_specs=tok(d),
        compiler_params=_params("arbitrary", "arbitrary"),
        name="out_proj",
    )(x, ga, y_hy, y_f, y_b, xbc, z, y_fn, dsk_row, ng_row, w1, w2, w3)


def _peer_scores(x_ref, g_ref, sh_ref, sc_ref, wq_ref, k1_ref, k2_ref, ht_ref, s1_ref, s2_ref):
    hm = _normmod(x_ref[...], g_ref[...], sh_ref[0], sc_ref[0])
    ht_ref[...] = hm.T.astype(F8)
    qb = jnp.dot(hm.astype(BF16), wq_ref[...], preferred_element_type=F32).astype(BF16)
    nt = (((1,), (1,)), ((), ()))
    for h in range(PEER_HEADS):
        q1 = qb[:, (2 * h) * PEER_HALF:(2 * h + 1) * PEER_HALF]
        q2 = qb[:, (2 * h + 1) * PEER_HALF:(2 * h + 2) * PEER_HALF]
        s1_ref[h] = lax.dot_general(k1_ref[...], q1, nt, preferred_element_type=F32)
        s2_ref[h] = lax.dot_general(k2_ref[...], q2, nt, preferred_element_type=F32)


def _sorting_network(n):
    pairs = []

    def merge(lo, m, r):
        step = 2 * r
        if step < m:
            merge(lo, m, step)
            merge(lo + r, m, step)
            for i in range(lo + r, lo + m - r, step):
                pairs.append((i, i + r))
        else:
            pairs.append((lo, lo + r))

    def sort(lo, m):
        if m > 1:
            half = m // 2
            sort(lo, half)
            sort(lo + half, half)
            merge(lo, m, 1)

    sort(0, n)
    return tuple(pairs)


NEG = -3.0e38
N_TOP = PEER_TOPK + 1


def _top_sorted(rows, n_out):
    rows = list(rows)
    for i, j in _sorting_network(len(rows)):
        rows[i], rows[j] = jnp.maximum(rows[i], rows[j]), jnp.minimum(rows[i], rows[j])
    sub = lax.broadcasted_iota(jnp.int32, rows[0].shape, 0)
    out = []
    for r in range(n_out):
        m = jnp.max(rows[0], axis=0, keepdims=True)
        out.append(m)
        first = jnp.min(jnp.where(rows[0] == m, sub, SUBLANES), axis=0, keepdims=True)
        hit = sub == first
        keep = min(len(rows), n_out - r - 1)
        rows = [jnp.where(hit, rows[k + 1] if k + 1 < len(rows) else NEG, rows[k]) for k in range(keep)]
    return out


def _count_above(vals, x, strict):
    assert len(vals) == 16
    test = (lambda v: v > x) if strict else (lambda v: v >= x)
    c8 = test(vals[7])
    c4 = test(jnp.where(c8, vals[11], vals[3]))
    c2 = test(jnp.where(c8, jnp.where(c4, vals[13], vals[9]), jnp.where(c4, vals[5], vals[1])))
    upper = jnp.where(c4, jnp.where(c2, vals[14], vals[12]), jnp.where(c2, vals[10], vals[8]))
    lower = jnp.where(c4, jnp.where(c2, vals[6], vals[4]), jnp.where(c2, vals[2], vals[0]))
    c1 = test(jnp.where(c8, upper, lower))
    r = (jnp.where(c8, 8.0, 0.0) + jnp.where(c4, 4.0, 0.0)) + (jnp.where(c2, 2.0, 0.0) + jnp.where(c1, 1.0, 0.0))
    return jnp.where(test(vals[15]), 16.0, r)


def _dup_bf16(x):
    hi = pltpu.bitcast(x.astype(BF16).astype(F32), jnp.uint32)
    return hi | (hi >> 16)


def _peer_stats_kernel(s1_ref, s2_ref, r2_ref, a2_ref, ns_ref, cc_ref):
    nrow = N_KEYS // SUBLANES
    t = s1_ref.shape[2]
    sub = lax.broadcasted_iota(jnp.int32, (SUBLANES, t), 0)
    for h in range(PEER_HEADS):
        s1_rows = [s1_ref[h, SUBLANES * k:SUBLANES * (k + 1), :] for k in range(nrow)]
        s2_rows = [s2_ref[h, SUBLANES * k:SUBLANES * (k + 1), :] for k in range(nrow)]
        v1 = _top_sorted(s1_rows, N_TOP)
        v2 = _top_sorted(s2_rows, N_TOP)
        cands = [v1[i] + v2[j] for i in range(N_TOP) for j in range(N_TOP) if (i + 1) * (j + 1) <= N_TOP]
        packed = []
        for k in range(0, len(cands), SUBLANES):
            blk = jnp.full((SUBLANES, t), NEG, F32)
            for s, cv in enumerate(cands[k:k + SUBLANES]):
                blk = jnp.where(sub == s, cv, blk)
            packed.append(blk)
        ids = [sub + SUBLANES * k for k in range(len(packed))]
        big = SUBLANES * len(packed)
        top = []
        for r in range(N_TOP):
            m = packed[0]
            for blk in packed[1:]:
                m = jnp.maximum(m, blk)
            m = jnp.max(m, axis=0, keepdims=True)
            top.append(m)
            sel = jnp.where(packed[0] == m, ids[0], big)
            for blk, idk in zip(packed[1:], ids[1:]):
                sel = jnp.minimum(sel, jnp.where(blk == m, idk, big))
            sel = jnp.min(sel, axis=0, keepdims=True)
            packed = [jnp.where(idk == sel, NEG, blk) for blk, idk in zip(packed, ids)]
        mx = top[0]
        z = jnp.zeros((1, t), F32)
        for r in range(PEER_TOPK):
            z = z + jnp.exp(top[r] - mx)
        tau = 0.5 * (top[PEER_TOPK - 1] + top[PEER_TOPK])
        cscale = 0.5 / z
        for k in range(0, nrow, 2):
            pair = slice(SUBLANES * k, SUBLANES * (k + 2))
            r2 = [_count_above(v2[:PEER_TOPK], s2_rows[k + q], True) for q in range(2)]
            r2_ref[h, pair, :] = jnp.concatenate(r2, axis=0).astype(BF16)
            a2 = [jnp.exp(s2_rows[k + q] - v2[0]) for q in range(2)]
            a2_ref[h, pair, :] = jnp.concatenate(a2, axis=0).astype(BF16)
        for k in range(nrow):
            blk = slice(SUBLANES * k, SUBLANES * (k + 1))
            ns_ref[h, blk, :] = _dup_bf16(_count_above(v2[:PEER_TOPK], tau - s1_rows[k], False))
            cc_ref[h, blk, :] = _dup_bf16(jnp.exp(s1_rows[k] - v1[0]) * cscale)


def _peer_select_kernel(x_ref, g_ref, sh_ref, sc_ref, wq_ref, k1_ref, k2_ref,
                        ht_ref, r2_ref, a2_ref, ns_ref, cc_ref, s1_ref, s2_ref):
    _peer_scores(x_ref, g_ref, sh_ref, sc_ref, wq_ref, k1_ref, k2_ref, ht_ref, s1_ref, s2_ref)
    _peer_stats_kernel(s1_ref, s2_ref, r2_ref, a2_ref, ns_ref, cc_ref)


def _peer_select(x2, g, sh, sc, wq, k1, k2, l):
    n, d = x2.shape
    tm = min(SELECT_TOKENS, l)
    per = l // tm
    shp = (PEER_HEADS, N_KEYS, n)
    blk = pl.BlockSpec((PEER_HEADS, N_KEYS, tm), lambda i: (0, 0, i))
    return pl.pallas_call(
        _peer_select_kernel,
        out_shape=[jax.ShapeDtypeStruct((d, n), F8),
                   jax.ShapeDtypeStruct(shp, BF16), jax.ShapeDtypeStruct(shp, BF16),
                   jax.ShapeDtypeStruct(shp, jnp.uint32), jax.ShapeDtypeStruct(shp, jnp.uint32)],
        grid=(n // tm,),
        in_specs=[pl.BlockSpec((tm, d), lambda i: (i, 0)),
                  pl.BlockSpec((1, d), lambda i: (0, 0)),
                  pl.BlockSpec((1, 1, d), lambda i: (i // per, 0, 0)),
                  pl.BlockSpec((1, 1, d), lambda i: (i // per, 0, 0)),
                  pl.BlockSpec(wq.shape, lambda i: (0, 0)),
                  pl.BlockSpec(k1.shape, lambda i: (0, 0)),
                  pl.BlockSpec(k2.shape, lambda i: (0, 0))],
        out_specs=[pl.BlockSpec((d, tm), lambda i: (0, i)), blk, blk, blk, blk],
        scratch_shapes=[pltpu.VMEM((PEER_HEADS, N_KEYS, tm), F32), pltpu.VMEM((PEER_HEADS, N_KEYS, tm), F32)],
        compiler_params=_params("arbitrary"),
        name="peer_select",
    )(x2, g, sh, sc, wq, k1, k2)


E_TILE = 1024
E_BLOCK = 2 * E_TILE
PEER_SCALE = 32.0


def _gelu_tanh_x2_scaled(s):
    c0 = math.sqrt(2.0 / math.pi) / PEER_SCALE
    c1 = math.sqrt(2.0 / math.pi) * 0.044715 / PEER_SCALE ** 3
    return s + s * jnp.tanh(s * (c0 + c1 * (s * s)))


ROWS_PER_TILE = E_TILE // N_KEYS


def _rows_bf16(word_row):
    words = word_row | jnp.zeros((SUBLANES, word_row.shape[1]), jnp.uint32)
    tile = pltpu.bitcast(words, BF16)
    return jnp.concatenate([tile] * (N_KEYS // tile.shape[0]), axis=0)


def _peer_mix_kernel(ht_ref, u_ref, vt_ref, r2_ref, a2_ref, nsp_ref, ccp_ref, nsc_ref, ccc_ref, x_ref, ga_ref,
                     gf_ref, o_ref, st0_ref, st1_ref, act0_ref, act1_ref, acc_ref, *, nblk, final_norm):
    m = pl.program_id(1)

    def scores(half, st_ref):
        st_ref[...] = jnp.dot(u_ref[half * E_TILE:(half + 1) * E_TILE, :], ht_ref[...],
                              preferred_element_type=F32).astype(BF16)

    def gate(st_ref, act_ref, ns_ref, cc_ref):
        for e in range(ROWS_PER_TILE):
            g = None
            for h in range(PEER_HEADS):
                nb = _rows_bf16(ns_ref[h, e:e + 1, :])
                cb = _rows_bf16(cc_ref[h, e:e + 1, :])
                a2 = a2_ref[h]
                term = jnp.where(r2_ref[h] < nb, a2, jnp.zeros_like(a2)) * cb
                g = term if g is None else g + term
            rows = slice(e * N_KEYS, (e + 1) * N_KEYS)
            act_ref[rows, :] = (_gelu_tanh_x2_scaled(st_ref[rows, :]) * g).astype(F8)

    def accumulate(half, act_ref):
        acc_ref[...] += jnp.dot(vt_ref[:, half * E_TILE:(half + 1) * E_TILE], act_ref[...],
                                preferred_element_type=F32)

    @pl.when(m == 0)
    def _():
        acc_ref[...] = jnp.zeros_like(acc_ref)
        scores(0, st0_ref)
        gate(st0_ref, act0_ref, nsc_ref, ccc_ref)
        scores(1, st1_ref)

    @pl.when((m > 0) & (m < nblk))
    def _():
        accumulate(0, act0_ref)
        gate(st1_ref, act1_ref, nsp_ref, ccp_ref)
        scores(0, st0_ref)
        accumulate(1, act1_ref)
        gate(st0_ref, act0_ref, nsc_ref, ccc_ref)
        scores(1, st1_ref)

    @pl.when(m == nblk)
    def _():
        accumulate(0, act0_ref)
        gate(st1_ref, act1_ref, nsp_ref, ccp_ref)
        accumulate(1, act1_ref)
        y = x_ref[...] + ga_ref[0] * (acc_ref[...].T * (1.0 / PEER_SCALE))
        if final_norm:
            y = y * lax.rsqrt(jnp.mean(y * y, axis=-1, keepdims=True) + EPS) * gf_ref[...]
        o_ref[...] = y


def _peer_mix(ht, u, vt, r2, a2, ns, cc, x2, ga, l, g_final=None):
    d, n = ht.shape
    n_exp = u.shape[0]
    t = min(TOKEN_TILE, l)
    per = l // t
    nblk = n_exp // E_BLOCK
    n_tiles = n_exp // E_TILE
    sblk = pl.BlockSpec((PEER_HEADS, N_KEYS, t), lambda i, m: (0, 0, i))
    prev = pl.BlockSpec((PEER_HEADS, ROWS_PER_TILE, t), lambda i, m: (0, jnp.maximum(2 * m - 1, 0), i))
    cur = pl.BlockSpec((PEER_HEADS, ROWS_PER_TILE, t), lambda i, m: (0, jnp.minimum(2 * m, n_tiles - 1), i))
    return pl.pallas_call(
        functools.partial(_peer_mix_kernel, nblk=nblk, final_norm=g_final is not None),
        out_shape=jax.ShapeDtypeStruct((n, d), F32),
        grid=(n // t, nblk + 1),
        in_specs=[pl.BlockSpec((d, t), lambda i, m: (0, i)),
                  pl.BlockSpec((E_BLOCK, d), lambda i, m: (jnp.minimum(m, nblk - 1), 0)),
                  pl.BlockSpec((d, E_BLOCK), lambda i, m: (0, jnp.maximum(m - 1, 0))),
                  sblk, sblk, prev, prev, cur, cur,
                  pl.BlockSpec((t, d), lambda i, m: (i, 0)),
                  pl.BlockSpec((1, 1, d), lambda i, m: (i // per, 0, 0)),
                  pl.BlockSpec((1, d), lambda i, m: (0, 0))],
        out_specs=pl.BlockSpec((t, d), lambda i, m: (i, 0)),
        scratch_shapes=[pltpu.VMEM((E_TILE, t), BF16), pltpu.VMEM((E_TILE, t), BF16),
                        pltpu.VMEM((E_TILE, t), F8), pltpu.VMEM((E_TILE, t), F8),
                        pltpu.VMEM((d, t), F32)],
        compiler_params=_params("arbitrary", "arbitrary"),
        name="peer_mix",
    )(ht, u, vt, r2, a2, ns, cc, ns, cc, x2, ga, jnp.ones((1, d), F32) if g_final is None else g_final)


def _pad_lanes(a):
    return jnp.pad(a, ((0, 0), (0, LANES - a.shape[1])))


def _token_mix_inputs(p, proj_hy, l):
    fs, fd, nyq = _hy_filter(l, p['hf_w1'], p['hf_b1'], p['hf_w2'], p['hf_b2'], p['hf_w3'], p['hf_freq'])
    return _hy_conv(proj_hy, p['hy_conv_w'], p['hy_conv_b'], fs, fd, nyq, p['hy_bias'])


def _peer(x, p, sh, sc, ga, g_final=None):
    b, l, d = x.shape
    x2 = x.reshape(b * l, d)
    ht, r2, a2, ns, cc = _peer_select(x2, p['g_norm2'], sh, sc, p['wq'], p['k1'], p['k2'], l)
    out = _peer_mix(ht, p['u'], p['vt'], r2, a2, ns, cc, x2, ga, l, g_final)
    return out.reshape(b, l, d)


def _layer(xl, xc, mod_l, mod_c, p, ctx_out, g_final=None):
    b = xl.shape[0]
    d = xl.shape[2]
    sh1, sc1, ga1, sh2, sc2, ga2 = [m.reshape(b, 1, d) for m in jnp.split(mod_l, 6, axis=-1)]
    csh1, csc1, cga1, csh2, csc2, cga2 = [jnp.broadcast_to(m.reshape(1, 1, d), (b, 1, d))
                                          for m in jnp.split(mod_c, 6, axis=-1)]
    w_in = p['w_in']
    w_hy, w_z, w_xbc, w_fn = (w_in[:, OFF_HY:OFF_Z], w_in[:, OFF_Z:OFF_XBC], w_in[:, OFF_XBC:OFF_DT],
                              w_in[:, OFF_FN:D_IN_PROJ])
    w_dt = _pad_lanes(w_in[:, OFF_DT:OFF_FN])
    g1 = p['g_norm1']
    all_w, all_dt = [w_hy, w_z, w_xbc, w_dt, w_fn], [BF16, F32, F32, F32, BF16]
    pl_hy, pl_z, pl_xbc, pl_dt, pl_fn = _in_proj(xl, g1, sh1, sc1, all_w, all_dt)
    if ctx_out:
        pc_hy, pc_z, pc_xbc, pc_dt, pc_fn = _in_proj(xc, g1, csh1, csc1, all_w, all_dt)
    else:
        pc_xbc, pc_dt = _in_proj(xc, g1, csh1, csc1, [w_xbc, w_dt], [F32, F32])

    dtb, arow, cw, cb = p['dtb_row'], p['a_row'], p['ssd_conv_w'], p['ssd_conv_b']
    h0 = jnp.zeros((b, SSD_GROUPS, SSD_STATE, SSD_INNER // SSD_GROUPS), F32)
    yc_f, yc_b, xs_c, hc_f, hc_b = _ssd_scan(pc_xbc, pc_dt, cw, cb, dtb, arow, h0, h0)
    yl_f, yl_b, xs_l, _, _ = _ssd_scan(pl_xbc, pl_dt, cw, cb, dtb, arow, hc_f, hc_b)

    def token_mix(x, ga, proj_hy, y_f, y_b, xs, z, proj_fn):
        y_hy = _token_mix_inputs(p, proj_hy, x.shape[1])
        y_fn = _fnet(proj_fn)
        return _out_proj(x, ga, y_hy, y_f, y_b, xs, z, y_fn, p['dsk_row'], p['ng_row'], p['wo1'], p['wo2'], p['wo3'])

    xl = token_mix(xl, ga1, pl_hy, yl_f, yl_b, xs_l, pl_z, pl_fn)
    xl = _peer(xl, p, sh2, sc2, ga2, g_final)
    if ctx_out:
        xc = token_mix(xc, cga1, pc_hy, yc_f, yc_b, xs_c, pc_z, pc_fn)
        xc = _peer(xc, p, csh2, csc2, cga2)
    return xl, xc


def kernel(x, c, ctx, c_ctx, w_ada, b_ada, g_norm1, g_norm2, w_in, hy_conv_w, hy_conv_b, hf_w1, hf_b1, hf_w2, hf_b2, hf_w3, hf_freq, hy_bias, ssd_conv_w, ssd_conv_b, ssd_dt_bias, ssd_a_log, ssd_d, ssd_norm_g, w_out, peer_wq, peer_k1, peer_k2, peer_u, peer_v, g_final):
    depth = w_ada.shape[0]
    b, l, d = x.shape
    rows = -(-(b + 1) // SUBLANES) * SUBLANES
    cc = jnp.concatenate([c, c_ctx[None, :], jnp.zeros((rows - b - 1, d), F32)], axis=0)
    mods = _ada_mod(cc, w_ada, b_ada)
    xl, xc = x, ctx
    for i in range(depth):
        wo = w_out[i].astype(BF16)
        p = {
            'g_norm1': g_norm1[i].reshape(1, d), 'g_norm2': g_norm2[i].reshape(1, d),
            'w_in': w_in[i].astype(BF16),
            'hy_conv_w': hy_conv_w[i], 'hy_conv_b': hy_conv_b[i],
            'hf_w1': hf_w1[i], 'hf_b1': hf_b1[i], 'hf_w2': hf_w2[i], 'hf_b2': hf_b2[i], 'hf_w3': hf_w3[i],
            'hf_freq': hf_freq[i], 'hy_bias': hy_bias[i],
            'ssd_conv_w': ssd_conv_w[i], 'ssd_conv_b': ssd_conv_b[i],
            'dtb_row': _pad_lanes(ssd_dt_bias[i].reshape(1, 2 * SSD_HEADS)),
            'a_row': _pad_lanes(-jnp.exp(ssd_a_log[i].astype(F32)).reshape(1, 2 * SSD_HEADS)),
            'dsk_row': jnp.repeat(ssd_d[i].astype(F32), SSD_HEAD_DIM).reshape(1, SSD_INNER),
            'ng_row': ssd_norm_g[i].reshape(1, SSD_INNER),
            'wo1': wo[:HY_WIDTH], 'wo2': wo[HY_WIDTH:HY_WIDTH + SSD_INNER], 'wo3': wo[HY_WIDTH + SSD_INNER:],
            'wq': peer_wq[i].astype(BF16), 'k1': peer_k1[i].astype(BF16), 'k2': peer_k2[i].astype(BF16),
            'u': (peer_u[i] * PEER_SCALE).astype(F8), 'vt': peer_v[i].astype(F8).T,
        }
        last = i == depth - 1
        xl, xc = _layer(xl, xc, mods[i, :b], mods[i, b], p, not last, g_final.reshape(1, d) if last else None)
    return xl
```

```python
import functools
import math

import jax
import jax.numpy as jnp
import numpy as np
from jax import lax
from jax.experimental import pallas as pl
from jax.experimental.pallas import tpu as pltpu

F32 = jnp.float32
BF16 = jnp.bfloat16
F8 = jnp.float8_e4m3fn
HI = lax.Precision.HIGHEST

EPS = 1e-6
LANES = 128
SUBLANES = 8
VMEM_LIMIT = 56 * 1024 * 1024

ADA_COL_TILE = 1536
TOKEN_TILE = 512
SELECT_TOKENS = 256
DFT_BIN_TILE = 256

HY_WIDTH = 256
HY_EMB = 33
HY_BANDS = (HY_EMB - 1) // 2
HY_DECAY_TARGET = 1e-2
HY_SHORT_DECAY_PCT = 0.3
HY_LONG_DECAY_PCT = 1.5
SSD_HEADS = 8
SSD_HEAD_DIM = 64
SSD_INNER = SSD_HEADS * SSD_HEAD_DIM
SSD_GROUPS = 2
SSD_STATE = 128
SSD_CHUNK = 128
SSD_XBC = SSD_INNER + 2 * SSD_GROUPS * SSD_STATE
FN_WIDTH = 256
FN_GROUPS = 4
FN_GROUP_DIM = FN_WIDTH // FN_GROUPS
PEER_HEADS = 8
PEER_TOPK = 16
N_KEYS = 128
PEER_HALF = 128
OFF_HY = 0
OFF_Z = OFF_HY + 3 * HY_WIDTH
OFF_XBC = OFF_Z + SSD_INNER
OFF_DT = OFF_XBC + SSD_XBC
OFF_FN = OFF_DT + 2 * SSD_HEADS
D_IN_PROJ = OFF_FN + FN_WIDTH


def _params(*sem):
    return pltpu.CompilerParams(dimension_semantics=sem, vmem_limit_bytes=VMEM_LIMIT)


def _silu(x):
    return x * jax.nn.sigmoid(x)


def _softplus(x):
    return jnp.maximum(x, 0.0) + jnp.log1p(jnp.exp(-jnp.abs(x)))


def _ada_kernel(c_ref, w_ref, b_ref, o_ref):
    s = _silu(c_ref[...])
    o_ref[0] = jnp.dot(s, w_ref[0], preferred_element_type=F32, precision=HI) + b_ref[0]


def _ada_mod(cc, w_ada, b_ada):
    depth, d, n = w_ada.shape
    r = cc.shape[0]
    tn = ADA_COL_TILE
    return pl.pallas_call(
        _ada_kernel,
        out_shape=jax.ShapeDtypeStruct((depth, r, n), F32),
        grid=(depth, n // tn),
        in_specs=[pl.BlockSpec((r, d), lambda l, j: (0, 0)),
                  pl.BlockSpec((1, d, tn), lambda l, j: (l, 0, j)),
                  pl.BlockSpec((1, 1, tn), lambda l, j: (l, 0, j))],
        out_specs=pl.BlockSpec((1, r, tn), lambda l, j: (l, 0, j)),
        compiler_params=_params("arbitrary", "arbitrary"),
        name="ada_mod",
    )(cc, w_ada, b_ada.reshape(depth, 1, n))


def _normmod(x, g, sh, sc):
    y = x * lax.rsqrt(jnp.mean(x * x, axis=-1, keepdims=True) + EPS) * g
    return y * (1.0 + sc) + sh


def _inproj_kernel(x_ref, g_ref, sh_ref, sc_ref, *rest, n_w):
    w_refs, o_refs = rest[:n_w], rest[n_w:]
    hb = _normmod(x_ref[0], g_ref[...], sh_ref[0], sc_ref[0]).astype(BF16)
    for w_ref, o_ref in zip(w_refs, o_refs):
        o_ref[0] = jnp.dot(hb, w_ref[...], preferred_element_type=F32).astype(o_ref.dtype)


def _in_proj(x, g, sh, sc, ws, dtypes):
    b, l, d = x.shape
    tm = min(TOKEN_TILE, l)
    return pl.pallas_call(
        functools.partial(_inproj_kernel, n_w=len(ws)),
        out_shape=[jax.ShapeDtypeStruct((b, l, w.shape[1]), dt) for w, dt in zip(ws, dtypes)],
        grid=(b, l // tm),
        in_specs=[pl.BlockSpec((1, tm, d), lambda i, j: (i, j, 0)),
                  pl.BlockSpec((1, d), lambda i, j: (0, 0)),
                  pl.BlockSpec((1, 1, d), lambda i, j: (i, 0, 0)),
                  pl.BlockSpec((1, 1, d), lambda i, j: (i, 0, 0))]
        + [pl.BlockSpec(w.shape, lambda i, j: (0, 0)) for w in ws],
        out_specs=[pl.BlockSpec((1, tm, w.shape[1]), lambda i, j: (i, j, 0)) for w in ws],
        compiler_params=_params("arbitrary", "arbitrary"),
        name="in_proj",
    )(x, g, sh, sc, *ws)


def _conv3(u, w, bias):
    l = u.shape[0]
    row = lax.broadcasted_iota(jnp.int32, u.shape, 0)
    um = jnp.where(row == 0, 0.0, pltpu.roll(u, 1, 0))
    up = jnp.where(row == l - 1, 0.0, pltpu.roll(u, l - 1, 0))
    return um * w[0:1, :] + u * w[1:2, :] + up * w[2:3, :] + bias


def _split3(x):
    hi = x.astype(BF16)
    r1 = x - hi.astype(F32)
    mid = r1.astype(BF16)
    lo = (r1 - mid.astype(F32)).astype(BF16)
    return hi, mid, lo


def _ssd_chunk(xbc, dt_raw, dtb, arow, st_ref, direction):
    q = SSD_CHUNK
    x = xbc[:, :SSD_INNER]
    gn = SSD_GROUPS * SSD_STATE
    bm = xbc[:, SSD_INNER:SSD_INNER + gn].astype(BF16)
    cm = xbc[:, SSD_INNER + gn:].astype(BF16)
    dtp = _softplus(dt_raw + dtb)
    da = dtp * arow
    ii = lax.broadcasted_iota(jnp.int32, (q, q), 0)
    jj = lax.broadcasted_iota(jnp.int32, (q, q), 1)
    tri = (ii >= jj) if direction == 0 else (ii <= jj)
    tri_b = tri.astype(BF16)
    cs = jnp.dot(jnp.concatenate([tri_b] * 3, axis=1), jnp.concatenate(_split3(da), axis=0),
                 preferred_element_type=F32)
    cs_t = cs.T
    lane = lax.broadcasted_iota(jnp.int32, (LANES, SSD_INNER), 0)
    chan = lax.broadcasted_iota(jnp.int32, (LANES, SSD_INNER), 1)
    head_of = lax.shift_right_logical(chan, int(math.log2(SSD_HEAD_DIM)))
    expand = (lane == direction * SSD_HEADS + head_of).astype(BF16)
    expand3 = jnp.concatenate([expand] * 3, axis=0)
    both = jnp.concatenate([jnp.concatenate(_split3(cs), axis=1), jnp.concatenate(_split3(dtp), axis=1)], axis=0)
    wide = jnp.dot(both, expand3, preferred_element_type=F32)
    acum, dtf = wide[:q], wide[q:]
    tot = acum[q - 1:q, :] if direction == 0 else acum[0:1, :]
    xdt = x * dtf
    ea = jnp.exp(acum)
    xdec = (jnp.exp(tot - acum) * xdt).astype(BF16)
    cd = jnp.exp(tot)
    xdt_b = xdt.astype(BF16)
    hg = SSD_HEADS // SSD_GROUPS
    gw = hg * SSD_HEAD_DIM
    outs = []
    for g in range(SSD_GROUPS):
        bg = bm[:, g * SSD_STATE:(g + 1) * SSD_STATE]
        cg = cm[:, g * SSD_STATE:(g + 1) * SSD_STATE]
        cb = lax.dot_general(cg, bg, (((1,), (1,)), ((), ())), preferred_element_type=F32)
        st = st_ref[g]
        y_g = jnp.dot(cg, st.astype(BF16), preferred_element_type=F32) * ea[:, g * gw:(g + 1) * gw]
        xg = xdt_b[:, g * gw:(g + 1) * gw]
        head_g = lax.shift_right_logical(lax.broadcasted_iota(jnp.int32, xg.shape, 1), int(math.log2(SSD_HEAD_DIM)))
        for k in range(hg):
            ln = direction * SSD_HEADS + g * hg + k
            seg = cs[:, ln:ln + 1] - cs_t[ln:ln + 1, :]
            lmat = jnp.where(tri, jnp.exp(jnp.where(tri, seg, 0.0)), 0.0)
            m = (cb * lmat).astype(BF16)
            y_g = y_g + jnp.dot(m, jnp.where(head_g == k, xg, jnp.zeros_like(xg)), preferred_element_type=F32)
        outs.append(y_g)
        sg = lax.dot_general(bg, xdec[:, g * gw:(g + 1) * gw], (((0,), (0,)), ((), ())),
                             preferred_element_type=F32)
        st_ref[g] = st * cd[:, g * gw:(g + 1) * gw] + sg
    return jnp.concatenate(outs, axis=1)


def _conv3_silu_chunk(cur_ref, before_ref, after_ref, w, bias, chunk, nc):
    cur = cur_ref[0]
    q = cur.shape[0]
    row = lax.broadcasted_iota(jnp.int32, cur.shape, 0)
    prev_row = jnp.where(chunk == 0, 0.0, before_ref[0, SUBLANES - 1:SUBLANES, :])
    next_row = jnp.where(chunk == nc - 1, 0.0, after_ref[0, 0:1, :])
    um = jnp.where(row == 0, prev_row, pltpu.roll(cur, 1, 0))
    up = jnp.where(row == q - 1, next_row, pltpu.roll(cur, q - 1, 0))
    return _silu(um * w[0:1, :] + cur * w[1:2, :] + up * w[2:3, :] + bias)


def _ssd_kernel(xf_ref, xf0_ref, xf1_ref, dtf_ref, xb_ref, xb0_ref, xb1_ref, dtb_ref, cw_ref, cb_ref,
                bias_ref, arow_ref, h0f_ref, h0b_ref, yf_ref, yb_ref, xs_ref, hlf_ref, hlb_ref,
                stf_ref, stb_ref, *, nc):
    c = pl.program_id(1)

    @pl.when(c == 0)
    def _():
        stf_ref[...] = h0f_ref[0]
        stb_ref[...] = h0b_ref[0]

    xf = _conv3_silu_chunk(xf_ref, xf0_ref, xf1_ref, cw_ref[...], cb_ref[...], c, nc)
    xs_ref[0] = xf[:, :SSD_INNER].astype(xs_ref.dtype)
    yf_ref[0] = _ssd_chunk(xf, dtf_ref[0], bias_ref[...], arow_ref[...], stf_ref, 0).astype(yf_ref.dtype)
    xb = _conv3_silu_chunk(xb_ref, xb0_ref, xb1_ref, cw_ref[...], cb_ref[...], nc - 1 - c, nc)
    yb_ref[0] = _ssd_chunk(xb, dtb_ref[0], bias_ref[...], arow_ref[...], stb_ref, 1).astype(yb_ref.dtype)

    @pl.when(c == nc - 1)
    def _():
        hlf_ref[0] = stf_ref[...]
        hlb_ref[0] = stb_ref[...]


def _ssd_scan(xbc, dt, conv_w, conv_b, dtb_row, a_row, h0f, h0b):
    b, l, _ = xbc.shape
    nc = l // SSD_CHUNK
    per = SSD_CHUNK // SUBLANES
    last = l // SUBLANES - 1
    fwd = lambda i, c: (i, c, 0)
    bwd = lambda i, c: (i, nc - 1 - c, 0)
    before = lambda chunk_of: (lambda i, c: (i, jnp.maximum(chunk_of(c) * per - 1, 0), 0))
    after = lambda chunk_of: (lambda i, c: (i, jnp.minimum(chunk_of(c) * per + per, last), 0))
    f_of, b_of = (lambda c: c), (lambda c: nc - 1 - c)
    halo = lambda imap: pl.BlockSpec((1, SUBLANES, SSD_XBC), imap)
    st_shape = (SSD_GROUPS, SSD_STATE, SSD_INNER // SSD_GROUPS)
    st_spec = pl.BlockSpec((1,) + st_shape, lambda i, c: (i, 0, 0, 0))
    row = pl.BlockSpec((1, LANES), lambda i, c: (0, 0))
    return pl.pallas_call(
        functools.partial(_ssd_kernel, nc=nc),
        out_shape=[jax.ShapeDtypeStruct((b, l, SSD_INNER), BF16)] * 3
        + [jax.ShapeDtypeStruct((b,) + st_shape, F32)] * 2,
        grid=(b, nc),
        in_specs=[pl.BlockSpec((1, SSD_CHUNK, SSD_XBC), fwd), halo(before(f_of)), halo(after(f_of)),
                  pl.BlockSpec((1, SSD_CHUNK, LANES), fwd),
                  pl.BlockSpec((1, SSD_CHUNK, SSD_XBC), bwd), halo(before(b_of)), halo(after(b_of)),
                  pl.BlockSpec((1, SSD_CHUNK, LANES), bwd),
                  pl.BlockSpec((3, SSD_XBC), lambda i, c: (0, 0)), pl.BlockSpec((1, SSD_XBC), lambda i, c: (0, 0)),
                  row, row, st_spec, st_spec],
        out_specs=[pl.BlockSpec((1, SSD_CHUNK, SSD_INNER), fwd), pl.BlockSpec((1, SSD_CHUNK, SSD_INNER), bwd),
                   pl.BlockSpec((1, SSD_CHUNK, SSD_INNER), fwd), st_spec, st_spec],
        scratch_shapes=[pltpu.VMEM(st_shape, F32), pltpu.VMEM(st_shape, F32)],
        compiler_params=_params("arbitrary", "arbitrary"),
        name="ssd_scan",
    )(xbc, xbc, xbc, dt, xbc, xbc, xbc, dt, conv_w, conv_b.reshape(1, SSD_XBC), dtb_row, a_row, h0f, h0b)


def _hy_filter_kernel(z_ref, w1_ref, b1_ref, w2_ref, b2_ref, w3_ref, fr_ref, win_ref, alt_ref,
                      fs_ref, fd_ref, nyq_ref):
    fr = fr_ref[...]
    h = jnp.sin(fr * (jnp.dot(z_ref[...], w1_ref[...], preferred_element_type=F32, precision=HI) + b1_ref[...]))
    h = jnp.sin(fr * (jnp.dot(h, w2_ref[...], preferred_element_type=F32, precision=HI) + b2_ref[...]))
    h = jnp.dot(h, w3_ref[...], preferred_element_type=F32, precision=HI)
    win = win_ref[...]
    row = lax.broadcasted_iota(jnp.int32, win.shape, 0)
    hf = h[:, :HY_WIDTH] * win
    hb = jnp.where(row == 0, 0.0, h[:, HY_WIDTH:] * win)
    nrm = lax.rsqrt(jnp.sum(hf * hf, axis=0, keepdims=True) + jnp.sum(hb * hb, axis=0, keepdims=True) + EPS)
    fs = (hf + hb) * nrm
    fs_ref[...] = fs
    fd_ref[...] = (hb - hf) * nrm
    nyq_ref[...] = jnp.sum(fs * alt_ref[...], axis=0, keepdims=True)


def _hy_filter(l, w1, b1, w2, b2, w3, freq):
    t = np.linspace(0.0, 1.0, l, dtype=np.float32)[:, None]
    w = (np.float32(2.0 * math.pi) * np.arange(l, dtype=np.float32)[:, None] / np.float32(l)).astype(np.float32)
    f = np.linspace(1e-4, HY_BANDS - 1, HY_BANDS, dtype=np.float32)[None, :]
    zf = jnp.asarray(f) * jnp.asarray(w)
    z = jnp.concatenate([jnp.asarray(t), jnp.cos(zf), -jnp.sin(zf)], axis=-1)
    z = jnp.pad(z, ((0, 0), (0, LANES - HY_EMB)))
    hid = w1.shape[1]
    w1p = jnp.pad(w1, ((0, LANES - HY_EMB), (0, LANES - hid)))
    w2p = jnp.pad(w2, ((0, LANES - hid), (0, LANES - hid)))
    w3p = jnp.pad(w3, ((0, LANES - hid), (0, 0)))
    pad_row = lambda v: jnp.pad(v.reshape(1, hid), ((0, 0), (0, LANES - hid)))
    max_decay = math.log(HY_DECAY_TARGET) / HY_SHORT_DECAY_PCT
    min_decay = math.log(HY_DECAY_TARGET) / HY_LONG_DECAY_PCT
    deltas = jnp.abs(jnp.linspace(min_decay, max_decay, HY_WIDTH, dtype=F32))
    win = jnp.exp(-jnp.asarray(t) * deltas)
    alt = jnp.asarray(np.where(np.arange(l) % 2 == 0, 1.0, -1.0).astype(np.float32)[:, None] * np.ones((1, HY_WIDTH), np.float32))
    return pl.pallas_call(
        _hy_filter_kernel,
        out_shape=[jax.ShapeDtypeStruct((l, HY_WIDTH), F32), jax.ShapeDtypeStruct((l, HY_WIDTH), F32),
                   jax.ShapeDtypeStruct((1, HY_WIDTH), F32)],
        compiler_params=pltpu.CompilerParams(vmem_limit_bytes=VMEM_LIMIT),
        name="hy_filter",
    )(z, w1p, pad_row(b1), w2p, pad_row(b2), w3p, pad_row(freq), win, alt)


def _mm_kernel(a_ref, b_ref, o_ref):
    o_ref[...] = jnp.dot(a_ref[...], b_ref[...], preferred_element_type=F32)


def _mm(a, b, tm):
    m, k = a.shape
    n = b.shape[1]
    return pl.pallas_call(
        _mm_kernel,
        out_shape=jax.ShapeDtypeStruct((m, n), F32),
        grid=(m // tm,),
        in_specs=[pl.BlockSpec((tm, k), lambda i: (i, 0)), pl.BlockSpec((k, n), lambda i: (0, 0))],
        out_specs=pl.BlockSpec((tm, n), lambda i: (i, 0)),
        compiler_params=_params("arbitrary"),
        name="mm",
    )(a, b)


def _dft_tables(l):
    n = 2 * l
    k = np.arange(l, dtype=np.int64)[:, None]
    s = np.arange(l, dtype=np.int64)[None, :]
    ang = (2.0 * np.pi / n) * ((k * s) % n).astype(np.float64)
    cos, sin = np.cos(ang), np.sin(ang)
    alt = np.where(np.arange(l) % 2 == 0, 1.0, -1.0)
    sin[0, :] = alt
    fwd = np.concatenate([cos, sin], axis=0)
    wgt = np.full((l, 1), 2.0)
    wgt[0, 0] = 1.0
    sin_i = -sin * wgt
    sin_i[0, :] = alt
    inv = np.concatenate([(cos * wgt).T, sin_i.T], axis=1) / n
    return fwd, inv


def _batch_rows(b):
    return 2 if b % 2 == 0 else 1


def _hy_conv_kernel(u_ref, cw_ref, cb_ref, ff_ref, fi_ref, ka_ref, kb_ref, kc_ref, bias_ref, o_ref,
                    ub_ref, uf_ref, x1_ref, acc_ref, *, nf):
    f = pl.program_id(1)
    w = HY_WIDTH
    rows = u_ref.shape[0]

    @pl.when(f == 0)
    def _():
        for r in range(rows):
            part = lambda k: _conv3(u_ref[r, :, k * w:(k + 1) * w].astype(F32), cw_ref[:, k * w:(k + 1) * w],
                                    cb_ref[:, k * w:(k + 1) * w])
            u = part(2) * part(0)
            cols = slice(r * w, (r + 1) * w)
            uf_ref[:, cols] = u
            ub_ref[:, cols] = u.astype(BF16)
            x1_ref[:, cols] = part(1)
        acc_ref[...] = jnp.zeros_like(acc_ref)

    uf = jnp.dot(ff_ref[0], ub_ref[...], preferred_element_type=F32)
    fb = uf.shape[0] // 2
    ur, us = uf[:fb], uf[fb:]
    ka, kb, kc = [jnp.concatenate([k_ref[...]] * rows, axis=1) for k_ref in (ka_ref, kb_ref, kc_ref)]
    p = jnp.concatenate([ur * ka + us * kb, ur * kb - us * kc], axis=0).astype(BF16)
    acc_ref[...] += jnp.dot(fi_ref[0], p, preferred_element_type=F32)

    @pl.when(f == nf - 1)
    def _():
        for r in range(rows):
            cols = slice(r * w, (r + 1) * w)
            o_ref[r] = (x1_ref[:, cols] * (acc_ref[:, cols] + uf_ref[:, cols] * bias_ref[...])).astype(o_ref.dtype)


def _hy_conv(proj, conv_w, conv_b, fs, fd, nyq, bias):
    b, l, _ = proj.shape
    rows = _batch_rows(b)
    fwd, inv = _dft_tables(l)
    fb = min(DFT_BIN_TILE, l)
    nf = l // fb
    ff = jnp.asarray(np.stack([np.concatenate([fwd[i * fb:(i + 1) * fb], fwd[l + i * fb:l + (i + 1) * fb]], axis=0)
                               for i in range(nf)]), dtype=BF16)
    fi = jnp.asarray(np.stack([np.concatenate([inv[:, i * fb:(i + 1) * fb], inv[:, l + i * fb:l + (i + 1) * fb]], axis=1)
                               for i in range(nf)]), dtype=BF16)
    fwd_b = jnp.asarray(fwd, dtype=BF16)
    tm = min(TOKEN_TILE, l)
    k_r = _mm(fwd_b[:l], fs.astype(BF16), tm)
    k_i = _mm(fwd_b[l:], fd.astype(BF16), tm)
    first = (jnp.arange(l) == 0)[:, None]
    ka = k_r
    kb = jnp.where(first, 0.0, k_i)
    kc = jnp.where(first, -nyq, k_r)
    return pl.pallas_call(
        functools.partial(_hy_conv_kernel, nf=nf),
        out_shape=jax.ShapeDtypeStruct((b, l, HY_WIDTH), BF16),
        grid=(b // rows, nf),
        in_specs=[pl.BlockSpec((rows, l, 3 * HY_WIDTH), lambda i, f: (i, 0, 0)),
                  pl.BlockSpec((3, 3 * HY_WIDTH), lambda i, f: (0, 0)),
                  pl.BlockSpec((1, 3 * HY_WIDTH), lambda i, f: (0, 0)),
                  pl.BlockSpec((1, 2 * fb, l), lambda i, f: (f, 0, 0)),
                  pl.BlockSpec((1, l, 2 * fb), lambda i, f: (f, 0, 0)),
                  pl.BlockSpec((fb, HY_WIDTH), lambda i, f: (f, 0)),
                  pl.BlockSpec((fb, HY_WIDTH), lambda i, f: (f, 0)),
                  pl.BlockSpec((fb, HY_WIDTH), lambda i, f: (f, 0)),
                  pl.BlockSpec((1, HY_WIDTH), lambda i, f: (0, 0))],
        out_specs=pl.BlockSpec((rows, l, HY_WIDTH), lambda i, f: (i, 0, 0)),
        scratch_shapes=[pltpu.VMEM((l, rows * HY_WIDTH), BF16), pltpu.VMEM((l, rows * HY_WIDTH), F32),
                        pltpu.VMEM((l, rows * HY_WIDTH), F32), pltpu.VMEM((l, rows * HY_WIDTH), F32)],
        compiler_params=_params("arbitrary", "arbitrary"),
        name="hy_conv",
    )(proj, conv_w, conv_b.reshape(1, 3 * HY_WIDTH), ff, fi, ka, kb, kc, bias.reshape(1, HY_WIDTH))


def _fnet_kernel(x_ref, cc_ref, sc_ref, m_ref, o_ref, xx_ref):
    j = pl.program_id(1)
    rows, l, c = x_ref.shape

    @pl.when(j == 0)
    def _():
        for r in range(rows):
            xb = x_ref[r].astype(BF16)
            cols = slice(r * c, (r + 1) * c)
            xx_ref[:l, cols] = jnp.dot(xb, cc_ref[...], preferred_element_type=F32).astype(BF16)
            xx_ref[l:, cols] = jnp.dot(xb, sc_ref[...], preferred_element_type=F32).astype(BF16)

    y = jnp.dot(m_ref[...], xx_ref[...], preferred_element_type=F32)
    for r in range(rows):
        o_ref[r] = y[:, r * c:(r + 1) * c].astype(o_ref.dtype)


def _fnet(x):
    b, l, c = x.shape
    rows = _batch_rows(b)
    gd = FN_GROUP_DIM
    kc = np.arange(gd)[:, None] * np.arange(gd)[None, :]
    ang_c = 2.0 * np.pi * (kc % gd) / gd
    eye = np.eye(FN_GROUPS)
    cblk = jnp.asarray(np.kron(eye, np.cos(ang_c)), dtype=BF16)
    sblk = jnp.asarray(np.kron(eye, np.sin(ang_c)), dtype=BF16)
    kl = (np.arange(l, dtype=np.int64)[:, None] * np.arange(l, dtype=np.int64)[None, :]) % l
    ang_l = 2.0 * np.pi * kl / l
    scale = 1.0 / math.sqrt(l * gd)
    mat = jnp.asarray(np.concatenate([np.cos(ang_l), -np.sin(ang_l)], axis=1) * scale, dtype=BF16)
    tr = min(TOKEN_TILE, l)
    return pl.pallas_call(
        _fnet_kernel,
        out_shape=jax.ShapeDtypeStruct((b, l, c), BF16),
        grid=(b // rows, l // tr),
        in_specs=[pl.BlockSpec((rows, l, c), lambda i, r: (i, 0, 0)),
                  pl.BlockSpec((c, c), lambda i, r: (0, 0)),
                  pl.BlockSpec((c, c), lambda i, r: (0, 0)),
                  pl.BlockSpec((tr, 2 * l), lambda i, r: (r, 0))],
        out_specs=pl.BlockSpec((rows, tr, c), lambda i, r: (i, r, 0)),
        scratch_shapes=[pltpu.VMEM((2 * l, rows * c), BF16)],
        compiler_params=_params("arbitrary", "arbitrary"),
        name="fnet",
    )(x, cblk, sblk, mat)


def _outproj_kernel(x_ref, ga_ref, yhy_ref, yf_ref, yb_ref, xs_ref, z_ref, yfn_ref, dsk_ref, ng_ref,
                    w1_ref, w2_ref, w3_ref, o_ref):
    y = yf_ref[0].astype(F32) + yb_ref[0].astype(F32) + dsk_ref[...] * xs_ref[0].astype(F32)
    y = y * _silu(z_ref[0].astype(F32))
    gw = SSD_INNER // SSD_GROUPS
    parts = []
    for g in range(SSD_GROUPS):
        yg = y[:, g * gw:(g + 1) * gw]
        parts.append(yg * lax.rsqrt(jnp.mean(yg * yg, axis=-1, keepdims=True) + EPS))
    yn = jnp.concatenate(parts, axis=1) * ng_ref[...]
    out = jnp.dot(yhy_ref[0].astype(BF16), w1_ref[...], preferred_element_type=F32)
    out += jnp.dot(yn.astype(BF16), w2_ref[...], preferred_element_type=F32)
    out += jnp.dot(yfn_ref[0].astype(BF16), w3_ref[...], preferred_element_type=F32)
    o_ref[0] = x_ref[0] + ga_ref[0] * out


def _out_proj(x, ga, y_hy, y_f, y_b, xbc, z, y_fn, dsk_row, ng_row, w1, w2, w3):
    b, l, d = x.shape
    tm = min(TOKEN_TILE, l)
    tok = lambda n: pl.BlockSpec((1, tm, n), lambda i, j: (i, j, 0))
    const = lambda a: pl.BlockSpec(a.shape, lambda i, j: (0,) * a.ndim)
    return pl.pallas_call(
        _outproj_kernel,
        out_shape=jax.ShapeDtypeStruct((b, l, d), F32),
        grid=(b, l // tm),
        in_specs=[tok(d), pl.BlockSpec((1, 1, d), lambda i, j: (i, 0, 0)), tok(HY_WIDTH), tok(SSD_INNER),
                  tok(SSD_INNER), tok(SSD_INNER), tok(SSD_INNER), tok(FN_WIDTH), const(dsk_row), const(ng_row),
                  const(w1), const(w2), const(w3)],
        out_specs=tok(d),
        compiler_params=_params("arbitrary", "arbitrary"),
        name="out_proj",
    )(x, ga, y_hy, y_f, y_b, xbc, z, y_fn, dsk_row, ng_row, w1, w2, w3)


def _peer_scores(x_ref, g_ref, sh_ref, sc_ref, wq_ref, k1_ref, k2_ref, ht_ref, s1_ref, s2_ref):
    hm = _normmod(x_ref[...], g_ref[...], sh_ref[0], sc_ref[0])
    ht_ref[...] = hm.T.astype(F8)
    qb = jnp.dot(hm.astype(BF16), wq_ref[...], preferred_element_type=F32).astype(BF16)
    nt = (((1,), (1,)), ((), ()))
    for h in range(PEER_HEADS):
        q1 = qb[:, (2 * h) * PEER_HALF:(2 * h + 1) * PEER_HALF]
        q2 = qb[:, (2 * h + 1) * PEER_HALF:(2 * h + 2) * PEER_HALF]
        s1_ref[h] = lax.dot_general(k1_ref[...], q1, nt, preferred_element_type=F32)
        s2_ref[h] = lax.dot_general(k2_ref[...], q2, nt, preferred_element_type=F32)


def _sorting_network(n):
    pairs = []

    def merge(lo, m, r):
        step = 2 * r
        if step < m:
            merge(lo, m, step)
            merge(lo + r, m, step)
            for i in range(lo + r, lo + m - r, step):
                pairs.append((i, i + r))
        else:
            pairs.append((lo, lo + r))

    def sort(lo, m):
        if m > 1:
            half = m // 2
            sort(lo, half)
            sort(lo + half, half)
            merge(lo, m, 1)

    sort(0, n)
    return tuple(pairs)


NEG = -3.0e38
N_TOP = PEER_TOPK + 1


def _top_sorted(rows, n_out):
    rows = list(rows)
    for i, j in _sorting_network(len(rows)):
        rows[i], rows[j] = jnp.maximum(rows[i], rows[j]), jnp.minimum(rows[i], rows[j])
    sub = lax.broadcasted_iota(jnp.int32, rows[0].shape, 0)
    out = []
    for r in range(n_out):
        m = jnp.max(rows[0], axis=0, keepdims=True)
        out.append(m)
        first = jnp.min(jnp.where(rows[0] == m, sub, SUBLANES), axis=0, keepdims=True)
        hit = sub == first
        keep = min(len(rows), n_out - r - 1)
        rows = [jnp.where(hit, rows[k + 1] if k + 1 < len(rows) else NEG, rows[k]) for k in range(keep)]
    return out


def _count_above(vals, x, strict):
    assert len(vals) == 16
    test = (lambda v: v > x) if strict else (lambda v: v >= x)
    c8 = test(vals[7])
    c4 = test(jnp.where(c8, vals[11], vals[3]))
    c2 = test(jnp.where(c8, jnp.where(c4, vals[13], vals[9]), jnp.where(c4, vals[5], vals[1])))
    upper = jnp.where(c4, jnp.where(c2, vals[14], vals[12]), jnp.where(c2, vals[10], vals[8]))
    lower = jnp.where(c4, jnp.where(c2, vals[6], vals[4]), jnp.where(c2, vals[2], vals[0]))
    c1 = test(jnp.where(c8, upper, lower))
    r = (jnp.where(c8, 8.0, 0.0) + jnp.where(c4, 4.0, 0.0)) + (jnp.where(c2, 2.0, 0.0) + jnp.where(c1, 1.0, 0.0))
    return jnp.where(test(vals[15]), 16.0, r)


def _dup_bf16(x):
    hi = pltpu.bitcast(x.astype(BF16).astype(F32), jnp.uint32)
    return hi | (hi >> 16)


def _peer_stats_kernel(s1_ref, s2_ref, r2_ref, a2_ref, ns_ref, cc_ref):
    nrow = N_KEYS // SUBLANES
    t = s1_ref.shape[2]
    sub = lax.broadcasted_iota(jnp.int32, (SUBLANES, t), 0)
    for h in range(PEER_HEADS):
        s1_rows = [s1_ref[h, SUBLANES * k:SUBLANES * (k + 1), :] for k in range(nrow)]
        s2_rows = [s2_ref[h, SUBLANES * k:SUBLANES * (k + 1), :] for k in range(nrow)]
        v1 = _top_sorted(s1_rows, N_TOP)
        v2 = _top_sorted(s2_rows, N_TOP)
        cands = [v1[i] + v2[j] for i in range(N_TOP) for j in range(N_TOP) if (i + 1) * (j + 1) <= N_TOP]
        packed = []
        for k in range(0, len(cands), SUBLANES):
            blk = jnp.full((SUBLANES, t), NEG, F32)
            for s, cv in enumerate(cands[k:k + SUBLANES]):
                blk = jnp.where(sub == s, cv, blk)
            packed.append(blk)
        ids = [sub + SUBLANES * k for k in range(len(packed))]
        big = SUBLANES * len(packed)
        top = []
        for r in range(N_TOP):
            m = packed[0]
            for blk in packed[1:]:
                m = jnp.maximum(m, blk)
            m = jnp.max(m, axis=0, keepdims=True)
            top.append(m)
            sel = jnp.where(packed[0] == m, ids[0], big)
            for blk, idk in zip(packed[1:], ids[1:]):
                sel = jnp.minimum(sel, jnp.where(blk == m, idk, big))
            sel = jnp.min(sel, axis=0, keepdims=True)
            packed = [jnp.where(idk == sel, NEG, blk) for blk, idk in zip(packed, ids)]
        mx = top[0]
        z = jnp.zeros((1, t), F32)
        for r in range(PEER_TOPK):
            z = z + jnp.exp(top[r] - mx)
        tau = 0.5 * (top[PEER_TOPK - 1] + top[PEER_TOPK])
        cscale = 0.5 / z
        for k in range(0, nrow, 2):
            pair = slice(SUBLANES * k, SUBLANES * (k + 2))
            r2 = [_count_above(v2[:PEER_TOPK], s2_rows[k + q], True) for q in range(2)]
            r2_ref[h, pair, :] = jnp.concatenate(r2, axis=0).astype(BF16)
            a2 = [jnp.exp(s2_rows[k + q] - v2[0]) for q in range(2)]
            a2_ref[h, pair, :] = jnp.concatenate(a2, axis=0).astype(BF16)
        for k in range(nrow):
            blk = slice(SUBLANES * k, SUBLANES * (k + 1))
            ns_ref[h, blk, :] = _dup_bf16(_count_above(v2[:PEER_TOPK], tau - s1_rows[k], False))
            cc_ref[h, blk, :] = _dup_bf16(jnp.exp(s1_rows[k] - v1[0]) * cscale)


def _peer_select_kernel(x_ref, g_ref, sh_ref, sc_ref, wq_ref, k1_ref, k2_ref,
                        ht_ref, r2_ref, a2_ref, ns_ref, cc_ref, s1_ref, s2_ref):
    _peer_scores(x_ref, g_ref, sh_ref, sc_ref, wq_ref, k1_ref, k2_ref, ht_ref, s1_ref, s2_ref)
    _peer_stats_kernel(s1_ref, s2_ref, r2_ref, a2_ref, ns_ref, cc_ref)


def _peer_select(x2, g, sh, sc, wq, k1, k2, l):
    n, d = x2.shape
    tm = min(SELECT_TOKENS, l)
    per = l // tm
    shp = (PEER_HEADS, N_KEYS, n)
    blk = pl.BlockSpec((PEER_HEADS, N_KEYS, tm), lambda i: (0, 0, i))
    return pl.pallas_call(
        _peer_select_kernel,
        out_shape=[jax.ShapeDtypeStruct((d, n), F8),
                   jax.ShapeDtypeStruct(shp, BF16), jax.ShapeDtypeStruct(shp, BF16),
                   jax.ShapeDtypeStruct(shp, jnp.uint32), jax.ShapeDtypeStruct(shp, jnp.uint32)],
        grid=(n // tm,),
        in_specs=[pl.BlockSpec((tm, d), lambda i: (i, 0)),
                  pl.BlockSpec((1, d), lambda i: (0, 0)),
                  pl.BlockSpec((1, 1, d), lambda i: (i // per, 0, 0)),
                  pl.BlockSpec((1, 1, d), lambda i: (i // per, 0, 0)),
                  pl.BlockSpec(wq.shape, lambda i: (0, 0)),
                  pl.BlockSpec(k1.shape, lambda i: (0, 0)),
                  pl.BlockSpec(k2.shape, lambda i: (0, 0))],
        out_specs=[pl.BlockSpec((d, tm), lambda i: (0, i)), blk, blk, blk, blk],
        scratch_shapes=[pltpu.VMEM((PEER_HEADS, N_KEYS, tm), F32), pltpu.VMEM((PEER_HEADS, N_KEYS, tm), F32)],
        compiler_params=_params("arbitrary"),
        name="peer_select",
    )(x2, g, sh, sc, wq, k1, k2)


E_TILE = 1024
E_BLOCK = 2 * E_TILE
PEER_SCALE = 32.0


def _gelu_tanh_x2_scaled(s):
    c0 = math.sqrt(2.0 / math.pi) / PEER_SCALE
    c1 = math.sqrt(2.0 / math.pi) * 0.044715 / PEER_SCALE ** 3
    return s + s * jnp.tanh(s * (c0 + c1 * (s * s)))


ROWS_PER_TILE = E_TILE // N_KEYS


def _rows_bf16(word_row):
    words = word_row | jnp.zeros((SUBLANES, word_row.shape[1]), jnp.uint32)
    tile = pltpu.bitcast(words, BF16)
    return jnp.concatenate([tile] * (N_KEYS // tile.shape[0]), axis=0)


def _peer_mix_kernel(ht_ref, u_ref, vt_ref, r2_ref, a2_ref, nsp_ref, ccp_ref, nsc_ref, ccc_ref, x_ref, ga_ref,
                     gf_ref, o_ref, st0_ref, st1_ref, act0_ref, act1_ref, acc_ref, *, nblk, final_norm):
    m = pl.program_id(1)

    def scores(half, st_ref):
        st_ref[...] = jnp.dot(u_ref[half * E_TILE:(half + 1) * E_TILE, :], ht_ref[...],
                              preferred_element_type=F32).astype(BF16)

    def gate(st_ref, act_ref, ns_ref, cc_ref):
        for e in range(ROWS_PER_TILE):
            g = None
            for h in range(PEER_HEADS):
                nb = _rows_bf16(ns_ref[h, e:e + 1, :])
                cb = _rows_bf16(cc_ref[h, e:e + 1, :])
                a2 = a2_ref[h]
                term = jnp.where(r2_ref[h] < nb, a2, jnp.zeros_like(a2)) * cb
                g = term if g is None else g + term
            rows = slice(e * N_KEYS, (e + 1) * N_KEYS)
            act_ref[rows, :] = (_gelu_tanh_x2_scaled(st_ref[rows, :]) * g).astype(F8)

    def accumulate(half, act_ref):
        acc_ref[...] += jnp.dot(vt_ref[:, half * E_TILE:(half + 1) * E_TILE], act_ref[...],
                                preferred_element_type=F32)

    @pl.when(m == 0)
    def _():
        acc_ref[...] = jnp.zeros_like(acc_ref)
        scores(0, st0_ref)
        gate(st0_ref, act0_ref, nsc_ref, ccc_ref)
        scores(1, st1_ref)

    @pl.when((m > 0) & (m < nblk))
    def _():
        accumulate(0, act0_ref)
        gate(st1_ref, act1_ref, nsp_ref, ccp_ref)
        scores(0, st0_ref)
        accumulate(1, act1_ref)
        gate(st0_ref, act0_ref, nsc_ref, ccc_ref)
        scores(1, st1_ref)

    @pl.when(m == nblk)
    def _():
        accumulate(0, act0_ref)
        gate(st1_ref, act1_ref, nsp_ref, ccp_ref)
        accumulate(1, act1_ref)
        y = x_ref[...] + ga_ref[0] * (acc_ref[...].T * (1.0 / PEER_SCALE))
        if final_norm:
            y = y * lax.rsqrt(jnp.mean(y * y, axis=-1, keepdims=True) + EPS) * gf_ref[...]
        o_ref[...] = y


def _peer_mix(ht, u, vt, r2, a2, ns, cc, x2, ga, l, g_final=None):
    d, n = ht.shape
    n_exp = u.shape[0]
    t = min(TOKEN_TILE, l)
    per = l // t
    nblk = n_exp // E_BLOCK
    n_tiles = n_exp // E_TILE
    sblk = pl.BlockSpec((PEER_HEADS, N_KEYS, t), lambda i, m: (0, 0, i))
    prev = pl.BlockSpec((PEER_HEADS, ROWS_PER_TILE, t), lambda i, m: (0, jnp.maximum(2 * m - 1, 0), i))
    cur = pl.BlockSpec((PEER_HEADS, ROWS_PER_TILE, t), lambda i, m: (0, jnp.minimum(2 * m, n_tiles - 1), i))
    return pl.pallas_call(
        functools.partial(_peer_mix_kernel, nblk=nblk, final_norm=g_final is not None),
        out_shape=jax.ShapeDtypeStruct((n, d), F32),
        grid=(n // t, nblk + 1),
        in_specs=[pl.BlockSpec((d, t), lambda i, m: (0, i)),
                  pl.BlockSpec((E_BLOCK, d), lambda i, m: (jnp.minimum(m, nblk - 1), 0)),
                  pl.BlockSpec((d, E_BLOCK), lambda i, m: (0, jnp.maximum(m - 1, 0))),
                  sblk, sblk, prev, prev, cur, cur,
                  pl.BlockSpec((t, d), lambda i, m: (i, 0)),
                  pl.BlockSpec((1, 1, d), lambda i, m: (i // per, 0, 0)),
                  pl.BlockSpec((1, d), lambda i, m: (0, 0))],
        out_specs=pl.BlockSpec((t, d), lambda i, m: (i, 0)),
        scratch_shapes=[pltpu.VMEM((E_TILE, t), BF16), pltpu.VMEM((E_TILE, t), BF16),
                        pltpu.VMEM((E_TILE, t), F8), pltpu.VMEM((E_TILE, t), F8),
                        pltpu.VMEM((d, t), F32)],
        compiler_params=_params("arbitrary", "arbitrary"),
        name="peer_mix",
    )(ht, u, vt, r2, a2, ns, cc, ns, cc, x2, ga, jnp.ones((1, d), F32) if g_final is None else g_final)


def _pad_lanes(a):
    return jnp.pad(a, ((0, 0), (0, LANES - a.shape[1])))


def _token_mix_inputs(p, proj_hy, l):
    fs, fd, nyq = _hy_filter(l, p['hf_w1'], p['hf_b1'], p['hf_w2'], p['hf_b2'], p['hf_w3'], p['hf_freq'])
    return _hy_conv(proj_hy, p['hy_conv_w'], p['hy_conv_b'], fs, fd, nyq, p['hy_bias'])


def _peer(x, p, sh, sc, ga, g_final=None):
    b, l, d = x.shape
    x2 = x.reshape(b * l, d)
    ht, r2, a2, ns, cc = _peer_select(x2, p['g_norm2'], sh, sc, p['wq'], p['k1'], p['k2'], l)
    out = _peer_mix(ht, p['u'], p['vt'], r2, a2, ns, cc, x2, ga, l, g_final)
    return out.reshape(b, l, d)


def _layer(xl, xc, mod_l, mod_c, p, ctx_out, g_final=None):
    b = xl.shape[0]
    d = xl.shape[2]
    sh1, sc1, ga1, sh2, sc2, ga2 = [m.reshape(b, 1, d) for m in jnp.split(mod_l, 6, axis=-1)]
    csh1, csc1, cga1, csh2, csc2, cga2 = [jnp.broadcast_to(m.reshape(1, 1, d), (b, 1, d))
                                          for m in jnp.split(mod_c, 6, axis=-1)]
    w_in = p['w_in']
    w_hy, w_z, w_xbc, w_fn = (w_in[:, OFF_HY:OFF_Z], w_in[:, OFF_Z:OFF_XBC], w_in[:, OFF_XBC:OFF_DT],
                              w_in[:, OFF_FN:D_IN_PROJ])
    w_dt = _pad_lanes(w_in[:, OFF_DT:OFF_FN])
    g1 = p['g_norm1']
    all_w, all_dt = [w_hy, w_z, w_xbc, w_dt, w_fn], [BF16, BF16, F32, F32, BF16]
    pl_hy, pl_z, pl_xbc, pl_dt, pl_fn = _in_proj(xl, g1, sh1, sc1, all_w, all_dt)
    if ctx_out:
        pc_hy, pc_z, pc_xbc, pc_dt, pc_fn = _in_proj(xc, g1, csh1, csc1, all_w, all_dt)
    else:
        pc_xbc, pc_dt = _in_proj(xc, g1, csh1, csc1, [w_xbc, w_dt], [F32, F32])

    dtb, arow, cw, cb = p['dtb_row'], p['a_row'], p['ssd_conv_w'], p['ssd_conv_b']
    h0 = jnp.zeros((b, SSD_GROUPS, SSD_STATE, SSD_INNER // SSD_GROUPS), F32)
    yc_f, yc_b, xs_c, hc_f, hc_b = _ssd_scan(pc_xbc, pc_dt, cw, cb, dtb, arow, h0, h0)
    yl_f, yl_b, xs_l, _, _ = _ssd_scan(pl_xbc, pl_dt, cw, cb, dtb, arow, hc_f, hc_b)

    def token_mix(x, ga, proj_hy, y_f, y_b, xs, z, proj_fn):
        y_hy = _token_mix_inputs(p, proj_hy, x.shape[1])
        y_fn = _fnet(proj_fn)
        return _out_proj(x, ga, y_hy, y_f, y_b, xs, z, y_fn, p['dsk_row'], p['ng_row'], p['wo1'], p['wo2'], p['wo3'])

    xl = token_mix(xl, ga1, pl_hy, yl_f, yl_b, xs_l, pl_z, pl_fn)
    xl = _peer(xl, p, sh2, sc2, ga2, g_final)
    if ctx_out:
        xc = token_mix(xc, cga1, pc_hy, yc_f, yc_b, xs_c, pc_z, pc_fn)
        xc = _peer(xc, p, csh2, csc2, cga2)
    return xl, xc


def kernel(x, c, ctx, c_ctx, w_ada, b_ada, g_norm1, g_norm2, w_in, hy_conv_w, hy_conv_b, hf_w1, hf_b1, hf_w2, hf_b2, hf_w3, hf_freq, hy_bias, ssd_conv_w, ssd_conv_b, ssd_dt_bias, ssd_a_log, ssd_d, ssd_norm_g, w_out, peer_wq, peer_k1, peer_k2, peer_u, peer_v, g_final):
    depth = w_ada.shape[0]
    b, l, d = x.shape
    rows = -(-(b + 1) // SUBLANES) * SUBLANES
    cc = jnp.concatenate([c, c_ctx[None, :], jnp.zeros((rows - b - 1, d), F32)], axis=0)
    mods = _ada_mod(cc, w_ada, b_ada)
    xl, xc = x, ctx
    for i in range(depth):
        wo = w_out[i].astype(BF16)
        p = {
            'g_norm1': g_norm1[i].reshape(1, d), 'g_norm2': g_norm2[i].reshape(1, d),
            'w_in': w_in[i].astype(BF16),
            'hy_conv_w': hy_conv_w[i], 'hy_conv_b': hy_conv_b[i],
            'hf_w1': hf_w1[i], 'hf_b1': hf_b1[i], 'hf_w2': hf_w2[i], 'hf_b2': hf_b2[i], 'hf_w3': hf_w3[i],
            'hf_freq': hf_freq[i], 'hy_bias': hy_bias[i],
            'ssd_conv_w': ssd_conv_w[i], 'ssd_conv_b': ssd_conv_b[i],
            'dtb_row': _pad_lanes(ssd_dt_bias[i].reshape(1, 2 * SSD_HEADS)),
            'a_row': _pad_lanes(-jnp.exp(ssd_a_log[i].astype(F32)).reshape(1, 2 * SSD_HEADS)),
            'dsk_row': jnp.repeat(ssd_d[i].astype(F32), SSD_HEAD_DIM).reshape(1, SSD_INNER),
            'ng_row': ssd_norm_g[i].reshape(1, SSD_INNER),
            'wo1': wo[:HY_WIDTH], 'wo2': wo[HY_WIDTH:HY_WIDTH + SSD_INNER], 'wo3': wo[HY_WIDTH + SSD_INNER:],
            'wq': peer_wq[i].astype(BF16), 'k1': peer_k1[i].astype(BF16), 'k2': peer_k2[i].astype(BF16),
            'u': (peer_u[i] * PEER_SCALE).astype(F8), 'vt': peer_v[i].astype(F8).T,
        }
        last = i == depth - 1
        xl, xc = _layer(xl, xc, mods[i, :b], mods[i, b], p, not last, g_final.reshape(1, d) if last else None)
    return xl
```
